```python
import jax, jax.numpy as jnp
from jax import lax
import numpy as np

D_MODEL = 2048
BATCH = 8
SEQ = 4096
DEPTH = 2

N_MIXERS = 2
N_A = (DEPTH + 1) // 2
N_B = DEPTH // 2
CHUNK = 128
A_WIDTH = D_MODEL
A_GROUPS = 16
A_GROUP_DIM = A_WIDTH // A_GROUPS
B_HEADS = 16
B_HEAD_DIM = D_MODEL // B_HEADS
Q_BLOCK = 128
FFN_HIDDEN = ((-(-8 * D_MODEL // 3) + 255) // 256) * 256
N_MOD = 6
EPS = 1e-6

kernel_name = "hybrid_sgu_fox_adaln_trunk"


def rms_norm(x, g):
    xf = x.astype(jnp.float32)
    y = xf * lax.rsqrt(jnp.mean(xf * xf, axis=-1, keepdims=True) + EPS)
    return (y * g.astype(jnp.float32)).astype(x.dtype)


def layer_norm(x, g, b):
    xf = x.astype(jnp.float32)
    mu = jnp.mean(xf, axis=-1, keepdims=True)
    var = jnp.mean(jnp.square(xf - mu), axis=-1, keepdims=True)
    y = (xf - mu) * lax.rsqrt(var + EPS)
    return (y * g.astype(jnp.float32) + b.astype(jnp.float32)).astype(x.dtype)


def modulate(h, shift, scale):
    return h * (1 + scale[:, None, :]) + shift[:, None, :]


def chunk_sgu_mixer(h, w_in, b_in, ln_g, ln_b, w_s, b_s, w_out):
    B, S, _ = h.shape
    z = jax.nn.gelu(h @ w_in + b_in, approximate=False)
    u, v = jnp.split(z, 2, axis=-1)
    v = layer_norm(v, ln_g, ln_b)
    causal = jnp.tril(jnp.ones((CHUNK, CHUNK), dtype=bool))
    w = jnp.where(causal[None], w_s, 0).astype(v.dtype)
    vc = v.reshape(B, S // CHUNK, CHUNK, A_GROUPS, A_GROUP_DIM)
    sv = jnp.einsum('gts,bnsgc->bntgc', w, vc) + b_s.T[None, None, :, :, None].astype(v.dtype)
    y = u * sv.reshape(B, S, A_WIDTH)
    return y @ w_out


def forgetting_attention(h, w_in, b_f, w_out):
    B, S, D = h.shape
    proj = h @ w_in
    q, k, v, f_logit = jnp.split(proj, [D, 2 * D, 3 * D], axis=-1)
    q = q.reshape(B, S, B_HEADS, B_HEAD_DIM).transpose(0, 2, 1, 3)
    k = k.reshape(B, S, B_HEADS, B_HEAD_DIM).transpose(0, 2, 1, 3)
    v = v.reshape(B, S, B_HEADS, B_HEAD_DIM).transpose(0, 2, 1, 3)
    log_f = jax.nn.log_sigmoid(f_logit.astype(jnp.float32) + b_f.astype(jnp.float32))
    F = jnp.cumsum(log_f, axis=1).transpose(0, 2, 1)
    n_blk = S // Q_BLOCK
    q_blocks = q.reshape(B, B_HEADS, n_blk, Q_BLOCK, B_HEAD_DIM).transpose(2, 0, 1, 3, 4)
    F_blocks = F.reshape(B, B_HEADS, n_blk, Q_BLOCK).transpose(2, 0, 1, 3)
    k_pos = jnp.arange(S)
    scale = 1.0 / float(np.sqrt(B_HEAD_DIM))

    def attend_block(args):
        q_blk, F_blk, i = args
        q_pos = i * Q_BLOCK + jnp.arange(Q_BLOCK)
        s = jnp.einsum('bhqd,bhkd->bhqk', q_blk, k).astype(jnp.float32) * scale
        s = s + F_blk[..., None] - F[:, :, None, :]
        s = jnp.where(k_pos[None, :] <= q_pos[:, None], s, -jnp.inf)
        p = jax.nn.softmax(s, axis=-1).astype(v.dtype)
        return jnp.einsum('bhqk,bhkd->bhqd', p, v)

    o = lax.map(attend_block, (q_blocks, F_blocks, jnp.arange(n_blk)))
    o = o.transpose(1, 0, 3, 2, 4).reshape(B, S, D)
    return o @ w_out


def swiglu_ffn(h, w_gate, w_up, w_down):
    return (jax.nn.silu(h @ w_gate) * (h @ w_up)) @ w_down


def _fwd_setup_inputs(seed: int = 0) -> dict:
    key = jax.random.key(seed)
    ks = jax.random.split(key, 24)
    D, F_H, AW = D_MODEL, FFN_HIDDEN, A_WIDTH
    nrm = lambda k, shape, s: jax.random.normal(k, shape, jnp.float32) * s
    x = nrm(ks[0], (BATCH, SEQ, D), 1.0)
    c = nrm(ks[1], (BATCH, D), 1.0)
    ada_w = nrm(ks[2], (DEPTH, D, N_MOD * D), 0.5 * D ** -0.5)
    ada_b = nrm(ks[3], (DEPTH, N_MOD * D), 0.02)
    norm_mix_g = 1.0 + nrm(ks[4], (DEPTH, D), 0.02)
    norm_ffn_g = 1.0 + nrm(ks[5], (DEPTH, D), 0.02)
    a_w_in = nrm(ks[6], (N_A, D, 2 * AW), D ** -0.5)
    a_b_in = nrm(ks[7], (N_A, 2 * AW), 0.02)
    a_ln_g = 1.0 + nrm(ks[8], (N_A, AW), 0.02)
    a_ln_b = nrm(ks[9], (N_A, AW), 0.02)
    a_w_s = nrm(ks[10], (N_A, A_GROUPS, CHUNK, CHUNK), CHUNK ** -0.5)
    a_b_s = 1.0 + nrm(ks[11], (N_A, A_GROUPS, CHUNK), 0.02)
    a_w_out = nrm(ks[12], (N_A, AW, D), AW ** -0.5)
    b_w_qkv = nrm(ks[13], (N_B, D, 3 * D), D ** -0.5)
    b_w_f = nrm(ks[14], (N_B, D, B_HEADS), 0.5 * D ** -0.5)
    b_w_in = jnp.concatenate([b_w_qkv, b_w_f], axis=-1)
    b_b_f = jax.random.uniform(ks[15], (N_B, B_HEADS), jnp.float32, 1.0, 6.0)
    b_w_out = nrm(ks[16], (N_B, D, D), D ** -0.5)
    ffn_w_gate = nrm(ks[17], (DEPTH, D, F_H), D ** -0.5)
    ffn_w_up = nrm(ks[18], (DEPTH, D, F_H), D ** -0.5)
    ffn_w_down = nrm(ks[19], (DEPTH, F_H, D), F_H ** -0.5)
    final_g = 1.0 + nrm(ks[20], (D,), 0.02)
    return {"x": x, "c": c, "ada_w": ada_w, "ada_b": ada_b,
            "norm_mix_g": norm_mix_g, "norm_ffn_g": norm_ffn_g,
            "a_w_in": a_w_in, "a_b_in": a_b_in, "a_ln_g": a_ln_g, "a_ln_b": a_ln_b,
            "a_w_s": a_w_s, "a_b_s": a_b_s, "a_w_out": a_w_out,
            "b_w_in": b_w_in, "b_b_f": b_b_f, "b_w_out": b_w_out,
            "ffn_w_gate": ffn_w_gate, "ffn_w_up": ffn_w_up, "ffn_w_down": ffn_w_down,
            "final_g": final_g}


def _fwd_reference(x, c, ada_w, ada_b, norm_mix_g, norm_ffn_g,
              a_w_in, a_b_in, a_ln_g, a_ln_b, a_w_s, a_b_s, a_w_out,
              b_w_in, b_b_f, b_w_out,
              ffn_w_gate, ffn_w_up, ffn_w_down, final_g):
    c_act = jax.nn.silu(c)
    for i in range(DEPTH):
        mod = c_act @ ada_w[i] + ada_b[i]
        sh1, sc1, g1, sh2, sc2, g2 = jnp.split(mod, N_MOD, axis=-1)
        h = modulate(rms_norm(x, norm_mix_g[i]), sh1, sc1)
        j = i // N_MIXERS
        if i % N_MIXERS == 0:
            y = chunk_sgu_mixer(h, a_w_in[j], a_b_in[j], a_ln_g[j], a_ln_b[j],
                                a_w_s[j], a_b_s[j], a_w_out[j])
        else:
            y = forgetting_attention(h, b_w_in[j], b_b_f[j], b_w_out[j])
        x = x + g1[:, None, :] * y
        h = modulate(rms_norm(x, norm_ffn_g[i]), sh2, sc2)
        x = x + g2[:, None, :] * swiglu_ffn(h, ffn_w_gate[i], ffn_w_up[i], ffn_w_down[i])
    return rms_norm(x, final_g)


import jax as _jax
import jax.numpy as _jnp

TWIN_FORMAT = 'train_step'
FWD_PARAMS = ['x', 'c', 'ada_w', 'ada_b', 'norm_mix_g', 'norm_ffn_g', 'a_w_in', 'a_b_in', 'a_ln_g', 'a_ln_b', 'a_w_s', 'a_b_s', 'a_w_out', 'b_w_in', 'b_b_f', 'b_w_out', 'ffn_w_gate', 'ffn_w_up', 'ffn_w_down', 'final_g']
TWIN_WEIGHTS = ['ada_w', 'ada_b', 'norm_mix_g', 'norm_ffn_g', 'a_w_in', 'a_b_in', 'a_ln_g', 'a_ln_b', 'a_w_s', 'a_b_s', 'a_w_out', 'b_w_in', 'b_b_f', 'b_w_out', 'ffn_w_gate', 'ffn_w_up', 'ffn_w_down', 'final_g']
TWIN_DIFF_INPUT = 'x'
TWIN_INPUTS = ['x', 'c', 'ada_w', 'ada_b', 'norm_mix_g', 'norm_ffn_g', 'a_w_in', 'a_b_in', 'a_ln_g', 'a_ln_b', 'a_w_s', 'a_b_s', 'a_w_out', 'b_w_in', 'b_b_f', 'b_w_out', 'ffn_w_gate', 'ffn_w_up', 'ffn_w_down', 'final_g', 'loss_target', 'm_ada_w', 'm_ada_b', 'm_norm_mix_g', 'm_norm_ffn_g', 'm_a_w_in', 'm_a_b_in', 'm_a_ln_g', 'm_a_ln_b', 'm_a_w_s', 'm_a_b_s', 'm_a_w_out', 'm_b_w_in', 'm_b_b_f', 'm_b_w_out', 'm_ffn_w_gate', 'm_ffn_w_up', 'm_ffn_w_down', 'm_final_g', 'v_ada_w', 'v_ada_b', 'v_norm_mix_g', 'v_norm_ffn_g', 'v_a_w_in', 'v_a_b_in', 'v_a_ln_g', 'v_a_ln_b', 'v_a_w_s', 'v_a_b_s', 'v_a_w_out', 'v_b_w_in', 'v_b_b_f', 'v_b_w_out', 'v_ffn_w_gate', 'v_ffn_w_up', 'v_ffn_w_down', 'v_final_g']
TWIN_OUTPUTS = ['loss', 'grad_x', 'grad_ada_w', 'grad_ada_b', 'grad_norm_mix_g', 'grad_norm_ffn_g', 'grad_a_w_in', 'grad_a_b_in', 'grad_a_ln_g', 'grad_a_ln_b', 'grad_a_w_s', 'grad_a_b_s', 'grad_a_w_out', 'grad_b_w_in', 'grad_b_b_f', 'grad_b_w_out', 'grad_ffn_w_gate', 'grad_ffn_w_up', 'grad_ffn_w_down', 'grad_final_g', 'delta_ada_w', 'delta_ada_b', 'delta_norm_mix_g', 'delta_norm_ffn_g', 'delta_a_w_in', 'delta_a_b_in', 'delta_a_ln_g', 'delta_a_ln_b', 'delta_a_w_s', 'delta_a_b_s', 'delta_a_w_out', 'delta_b_w_in', 'delta_b_b_f', 'delta_b_w_out', 'delta_ffn_w_gate', 'delta_ffn_w_up', 'delta_ffn_w_down', 'delta_final_g', 'new_m_ada_w', 'new_m_ada_b', 'new_m_norm_mix_g', 'new_m_norm_ffn_g', 'new_m_a_w_in', 'new_m_a_b_in', 'new_m_a_ln_g', 'new_m_a_ln_b', 'new_m_a_w_s', 'new_m_a_b_s', 'new_m_a_w_out', 'new_m_b_w_in', 'new_m_b_b_f', 'new_m_b_w_out', 'new_m_ffn_w_gate', 'new_m_ffn_w_up', 'new_m_ffn_w_down', 'new_m_final_g', 'new_v_ada_w', 'new_v_ada_b', 'new_v_norm_mix_g', 'new_v_norm_ffn_g', 'new_v_a_w_in', 'new_v_a_b_in', 'new_v_a_ln_g', 'new_v_a_ln_b', 'new_v_a_w_s', 'new_v_a_b_s', 'new_v_a_w_out', 'new_v_b_w_in', 'new_v_b_b_f', 'new_v_b_w_out', 'new_v_ffn_w_gate', 'new_v_ffn_w_up', 'new_v_ffn_w_down', 'new_v_final_g']
TWIN_LEAF_KINDS = {'loss': 'loss', 'grad_x': 'grad_x', 'grad_ada_w': 'grad_w', 'grad_ada_b': 'grad_w', 'grad_norm_mix_g': 'grad_w', 'grad_norm_ffn_g': 'grad_w', 'grad_a_w_in': 'grad_w', 'grad_a_b_in': 'grad_w', 'grad_a_ln_g': 'grad_w', 'grad_a_ln_b': 'grad_w', 'grad_a_w_s': 'grad_w', 'grad_a_b_s': 'grad_w', 'grad_a_w_out': 'grad_w', 'grad_b_w_in': 'grad_w', 'grad_b_b_f': 'grad_w', 'grad_b_w_out': 'grad_w', 'grad_ffn_w_gate': 'grad_w', 'grad_ffn_w_up': 'grad_w', 'grad_ffn_w_down': 'grad_w', 'grad_final_g': 'grad_w', 'delta_ada_w': 'delta_w', 'delta_ada_b': 'delta_w', 'delta_norm_mix_g': 'delta_w', 'delta_norm_ffn_g': 'delta_w', 'delta_a_w_in': 'delta_w', 'delta_a_b_in': 'delta_w', 'delta_a_ln_g': 'delta_w', 'delta_a_ln_b': 'delta_w', 'delta_a_w_s': 'delta_w', 'delta_a_b_s': 'delta_w', 'delta_a_w_out': 'delta_w', 'delta_b_w_in': 'delta_w', 'delta_b_b_f': 'delta_w', 'delta_b_w_out': 'delta_w', 'delta_ffn_w_gate': 'delta_w', 'delta_ffn_w_up': 'delta_w', 'delta_ffn_w_down': 'delta_w', 'delta_final_g': 'delta_w', 'new_m_ada_w': 'new_m', 'new_m_ada_b': 'new_m', 'new_m_norm_mix_g': 'new_m', 'new_m_norm_ffn_g': 'new_m', 'new_m_a_w_in': 'new_m', 'new_m_a_b_in': 'new_m', 'new_m_a_ln_g': 'new_m', 'new_m_a_ln_b': 'new_m', 'new_m_a_w_s': 'new_m', 'new_m_a_b_s': 'new_m', 'new_m_a_w_out': 'new_m', 'new_m_b_w_in': 'new_m', 'new_m_b_b_f': 'new_m', 'new_m_b_w_out': 'new_m', 'new_m_ffn_w_gate': 'new_m', 'new_m_ffn_w_up': 'new_m', 'new_m_ffn_w_down': 'new_m', 'new_m_final_g': 'new_m', 'new_v_ada_w': 'new_v', 'new_v_ada_b': 'new_v', 'new_v_norm_mix_g': 'new_v', 'new_v_norm_ffn_g': 'new_v', 'new_v_a_w_in': 'new_v', 'new_v_a_b_in': 'new_v', 'new_v_a_ln_g': 'new_v', 'new_v_a_ln_b': 'new_v', 'new_v_a_w_s': 'new_v', 'new_v_a_b_s': 'new_v', 'new_v_a_w_out': 'new_v', 'new_v_b_w_in': 'new_v', 'new_v_b_b_f': 'new_v', 'new_v_b_w_out': 'new_v', 'new_v_ffn_w_gate': 'new_v', 'new_v_ffn_w_up': 'new_v', 'new_v_ffn_w_down': 'new_v', 'new_v_final_g': 'new_v'}


def _forward(args):
    return _fwd_reference(*[args[k] for k in FWD_PARAMS])


def _output_shape():
    def fwd():
        inp = _fwd_setup_inputs(0)
        return _fwd_reference(*[inp[k] for k in FWD_PARAMS])
    out = _jax.eval_shape(fwd)
    return out.shape, out.dtype

N_MICROBATCH = 1
ADAM_LR = 0.001
ADAM_B1 = 0.9
ADAM_B2 = 0.999
ADAM_EPS = 1e-08
ADAM_WD = 0.01
ADAM_STEP = 10
PER_EXAMPLE_BATCH_AXIS = {'x': 0, 'c': 0, 'loss_target': 0}
SHARED_INPUTS = []
_WEIGHT_DTYPES = {'ada_w': _jnp.float32, 'ada_b': _jnp.float32, 'norm_mix_g': _jnp.float32, 'norm_ffn_g': _jnp.float32, 'a_w_in': _jnp.float32, 'a_b_in': _jnp.float32, 'a_ln_g': _jnp.float32, 'a_ln_b': _jnp.float32, 'a_w_s': _jnp.float32, 'a_b_s': _jnp.float32, 'a_w_out': _jnp.float32, 'b_w_in': _jnp.float32, 'b_b_f': _jnp.float32, 'b_w_out': _jnp.float32, 'ffn_w_gate': _jnp.float32, 'ffn_w_up': _jnp.float32, 'ffn_w_down': _jnp.float32, 'final_g': _jnp.float32}
MOMENT_SCALE = {'ada_w': 2.493646e-02, 'ada_b': 4.307930e-02, 'norm_mix_g': 2.119296e-02, 'norm_ffn_g': 2.602284e-02, 'a_w_in': 2.047511e-02, 'a_b_in': 1.939316e-02, 'a_ln_g': 1.375722e-02, 'a_ln_b': 1.309807e-02, 'a_w_s': 1.357023e-02, 'a_b_s': 1.913850e-02, 'a_w_out': 2.394727e-02, 'b_w_in': 8.475540e-03, 'b_b_f': 3.739111e-02, 'b_w_out': 1.102101e-02, 'ffn_w_gate': 1.123126e-02, 'ffn_w_up': 1.087492e-02, 'ffn_w_down': 1.806017e-02, 'final_g': 1.602019e+01}


def _to_microbatches(a, axis):
    t = _jnp.moveaxis(a, axis, 0)
    t = t.reshape((N_MICROBATCH, t.shape[0] // N_MICROBATCH) + t.shape[1:])
    return _jnp.moveaxis(t, 1, axis + 1)


def setup_inputs(seed: int = 0) -> dict:
    inp = _fwd_setup_inputs(seed)
    key = _jax.random.fold_in(_jax.random.key(seed), 7919)
    shape, _ = _output_shape()
    out = dict(inp)
    out["loss_target"] = _jax.random.normal(_jax.random.fold_in(key, 0), shape, _jnp.float32)
    for i, name in enumerate(TWIN_WEIGHTS):
        w = inp[name].astype(_jnp.float32)
        if MOMENT_SCALE is None:
            s = _jnp.sqrt(_jnp.mean(_jnp.square(w)) + 1e-30)
        else:
            s = MOMENT_SCALE[name]
        km, kv = _jax.random.split(_jax.random.fold_in(key, i + 1))
        out[name] = w
        out["m_" + name] = s * _jax.random.normal(km, w.shape, _jnp.float32)
        out["v_" + name] = (s * s) * _jax.random.uniform(kv, w.shape, _jnp.float32, 0.5, 1.5)
    if N_MICROBATCH > 1:
        for name, axis in PER_EXAMPLE_BATCH_AXIS.items():
            out[name] = _to_microbatches(out[name], axis)
    return {'x': out['x'], 'c': out['c'], 'ada_w': out['ada_w'], 'ada_b': out['ada_b'], 'norm_mix_g': out['norm_mix_g'], 'norm_ffn_g': out['norm_ffn_g'], 'a_w_in': out['a_w_in'], 'a_b_in': out['a_b_in'], 'a_ln_g': out['a_ln_g'], 'a_ln_b': out['a_ln_b'], 'a_w_s': out['a_w_s'], 'a_b_s': out['a_b_s'], 'a_w_out': out['a_w_out'], 'b_w_in': out['b_w_in'], 'b_b_f': out['b_b_f'], 'b_w_out': out['b_w_out'], 'ffn_w_gate': out['ffn_w_gate'], 'ffn_w_up': out['ffn_w_up'], 'ffn_w_down': out['ffn_w_down'], 'final_g': out['final_g'], 'loss_target': out['loss_target'], 'm_ada_w': out['m_ada_w'], 'm_ada_b': out['m_ada_b'], 'm_norm_mix_g': out['m_norm_mix_g'], 'm_norm_ffn_g': out['m_norm_ffn_g'], 'm_a_w_in': out['m_a_w_in'], 'm_a_b_in': out['m_a_b_in'], 'm_a_ln_g': out['m_a_ln_g'], 'm_a_ln_b': out['m_a_ln_b'], 'm_a_w_s': out['m_a_w_s'], 'm_a_b_s': out['m_a_b_s'], 'm_a_w_out': out['m_a_w_out'], 'm_b_w_in': out['m_b_w_in'], 'm_b_b_f': out['m_b_b_f'], 'm_b_w_out': out['m_b_w_out'], 'm_ffn_w_gate': out['m_ffn_w_gate'], 'm_ffn_w_up': out['m_ffn_w_up'], 'm_ffn_w_down': out['m_ffn_w_down'], 'm_final_g': out['m_final_g'], 'v_ada_w': out['v_ada_w'], 'v_ada_b': out['v_ada_b'], 'v_norm_mix_g': out['v_norm_mix_g'], 'v_norm_ffn_g': out['v_norm_ffn_g'], 'v_a_w_in': out['v_a_w_in'], 'v_a_b_in': out['v_a_b_in'], 'v_a_ln_g': out['v_a_ln_g'], 'v_a_ln_b': out['v_a_ln_b'], 'v_a_w_s': out['v_a_w_s'], 'v_a_b_s': out['v_a_b_s'], 'v_a_w_out': out['v_a_w_out'], 'v_b_w_in': out['v_b_w_in'], 'v_b_b_f': out['v_b_b_f'], 'v_b_w_out': out['v_b_w_out'], 'v_ffn_w_gate': out['v_ffn_w_gate'], 'v_ffn_w_up': out['v_ffn_w_up'], 'v_ffn_w_down': out['v_ffn_w_down'], 'v_final_g': out['v_final_g']}


def _loss(weights, diff, rest, loss_target):
    with _jax.named_scope("forward"):
        args = {**rest, TWIN_DIFF_INPUT: diff, **{k: w.astype(_WEIGHT_DTYPES[k]) for k, w in weights.items()}}
        y = _forward(args)
    with _jax.named_scope("loss_head"):
        err = _jnp.square(y.astype(_jnp.float32) - loss_target)
        return 0.5 * _jnp.sum(_jnp.mean(err, axis=-1)) if err.ndim else 0.5 * err


def _adamw(w, g, m, v):
    m = ADAM_B1 * m + (1.0 - ADAM_B1) * g
    v = ADAM_B2 * v + (1.0 - ADAM_B2) * _jnp.square(g)
    m_hat = m / (1.0 - ADAM_B1 ** ADAM_STEP)
    v_hat = v / (1.0 - ADAM_B2 ** ADAM_STEP)
    delta = -ADAM_LR * (m_hat / (_jnp.sqrt(v_hat) + ADAM_EPS) + ADAM_WD * w)
    return delta, m, v


def reference(x, c, ada_w, ada_b, norm_mix_g, norm_ffn_g, a_w_in, a_b_in, a_ln_g, a_ln_b, a_w_s, a_b_s, a_w_out, b_w_in, b_b_f, b_w_out, ffn_w_gate, ffn_w_up, ffn_w_down, final_g, loss_target, m_ada_w, m_ada_b, m_norm_mix_g, m_norm_ffn_g, m_a_w_in, m_a_b_in, m_a_ln_g, m_a_ln_b, m_a_w_s, m_a_b_s, m_a_w_out, m_b_w_in, m_b_b_f, m_b_w_out, m_ffn_w_gate, m_ffn_w_up, m_ffn_w_down, m_final_g, v_ada_w, v_ada_b, v_norm_mix_g, v_norm_ffn_g, v_a_w_in, v_a_b_in, v_a_ln_g, v_a_ln_b, v_a_w_s, v_a_b_s, v_a_w_out, v_b_w_in, v_b_b_f, v_b_w_out, v_ffn_w_gate, v_ffn_w_up, v_ffn_w_down, v_final_g):
    given = dict(x=x, c=c, ada_w=ada_w, ada_b=ada_b, norm_mix_g=norm_mix_g, norm_ffn_g=norm_ffn_g, a_w_in=a_w_in, a_b_in=a_b_in, a_ln_g=a_ln_g, a_ln_b=a_ln_b, a_w_s=a_w_s, a_b_s=a_b_s, a_w_out=a_w_out, b_w_in=b_w_in, b_b_f=b_b_f, b_w_out=b_w_out, ffn_w_gate=ffn_w_gate, ffn_w_up=ffn_w_up, ffn_w_down=ffn_w_down, final_g=final_g, loss_target=loss_target, m_ada_w=m_ada_w, m_ada_b=m_ada_b, m_norm_mix_g=m_norm_mix_g, m_norm_ffn_g=m_norm_ffn_g, m_a_w_in=m_a_w_in, m_a_b_in=m_a_b_in, m_a_ln_g=m_a_ln_g, m_a_ln_b=m_a_ln_b, m_a_w_s=m_a_w_s, m_a_b_s=m_a_b_s, m_a_w_out=m_a_w_out, m_b_w_in=m_b_w_in, m_b_b_f=m_b_b_f, m_b_w_out=m_b_w_out, m_ffn_w_gate=m_ffn_w_gate, m_ffn_w_up=m_ffn_w_up, m_ffn_w_down=m_ffn_w_down, m_final_g=m_final_g, v_ada_w=v_ada_w, v_ada_b=v_ada_b, v_norm_mix_g=v_norm_mix_g, v_norm_ffn_g=v_norm_ffn_g, v_a_w_in=v_a_w_in, v_a_b_in=v_a_b_in, v_a_ln_g=v_a_ln_g, v_a_ln_b=v_a_ln_b, v_a_w_s=v_a_w_s, v_a_b_s=v_a_b_s, v_a_w_out=v_a_w_out, v_b_w_in=v_b_w_in, v_b_b_f=v_b_b_f, v_b_w_out=v_b_w_out, v_ffn_w_gate=v_ffn_w_gate, v_ffn_w_up=v_ffn_w_up, v_ffn_w_down=v_ffn_w_down, v_final_g=v_final_g)
    weights = {n: given[n] for n in TWIN_WEIGHTS}
    shared = {n: given[n] for n in SHARED_INPUTS}
    per_example = {n: given[n] for n in ['x', 'c']}
    grad_fn = _jax.value_and_grad(_loss, argnums=(0, 1))

    def one_microbatch(ex, loss_target):
        ex = dict(ex)
        diff = ex.pop(TWIN_DIFF_INPUT)
        return grad_fn(weights, diff, {**shared, **ex}, loss_target)

    if N_MICROBATCH == 1:
        loss, (grad_w, grad_x) = one_microbatch(per_example, given["loss_target"])
    else:
        def body(carry, xs):
            loss_sum, grad_sum = carry
            l_k, (gw_k, gx_k) = one_microbatch(xs[0], xs[1])
            with _jax.named_scope("update"):
                return (loss_sum + l_k, _jax.tree.map(_jnp.add, grad_sum, gw_k)), gx_k

        init = (_jnp.zeros((), _jnp.float32), _jax.tree.map(_jnp.zeros_like, weights))
        (loss, grad_w), grad_x = _jax.lax.scan(body, init, (per_example, given["loss_target"]))
    with _jax.named_scope("update"):
        delta_w, new_m, new_v = {}, {}, {}
        for n in TWIN_WEIGHTS:
            delta_w[n], new_m[n], new_v[n] = _adamw(weights[n], grad_w[n], given["m_" + n], given["v_" + n])
    return (loss, grad_x, *[grad_w[n] for n in TWIN_WEIGHTS], *[delta_w[n] for n in TWIN_WEIGHTS],
            *[new_m[n] for n in TWIN_WEIGHTS], *[new_v[n] for n in TWIN_WEIGHTS])
```

```python
import functools
import math

import jax
import jax.numpy as jnp
from jax import lax
from jax.experimental import pallas as pl
from jax.experimental.pallas import tpu as pltpu

F32 = jnp.float32
BF16 = jnp.bfloat16
MESH = pl.DeviceIdType.MESH
ANY = pl.BlockSpec(memory_space=pl.ANY)

N_DEV = 8
N_CHIP = 4
LANE = 128
SUBLANE = 8
CHUNK = 128
VMEM_LIMIT_BYTES = 52 * 1024 * 1024
NORM_EPS = 1e-6
NEG = -1e30
PACK_W = 1024

ADAM_LR = 0.001
ADAM_B1 = 0.9
ADAM_B2 = 0.999
ADAM_EPS = 1e-08
ADAM_WD = 0.01
ADAM_STEP = 10

NN = ((1,), (0,))
NT = ((1,), (1,))
TN = ((0,), (0,))


def _params(*sem):
    return pltpu.CompilerParams(dimension_semantics=sem or None, vmem_limit_bytes=VMEM_LIMIT_BYTES)


def _tile(n, target, align=LANE):
    best = None
    for t in range(align, min(n, target) + 1, align):
        if n % t == 0:
            best = t
    return best or n


def _dot(a, b, dims):
    return lax.dot_general(a, b, (dims, ((), ())), preferred_element_type=F32)


def _place():
    return lax.axis_index("x"), lax.axis_index("y"), lax.axis_index("c")


def _mm(name, grid, lhs, rhs, dims, outs, epi=None, extras=(), summed=True, k_axis=None):
    n_a, n_b, n_e, n_o = len(lhs), len(rhs), len(extras), len(outs)
    n_acc = 1 if summed else n_b
    nk = grid[k_axis] if k_axis is not None else 1
    acc_shape = tuple(d for d in outs[0][2] if d is not None)

    def body(*refs):
        a_refs, b_refs = refs[:n_a], refs[n_a:n_a + n_b]
        e_refs = refs[n_a + n_b:n_a + n_b + n_e]
        o_refs = refs[n_a + n_b + n_e:n_a + n_b + n_e + n_o]
        acc_refs = refs[n_a + n_b + n_e + n_o:]

        def products():
            a_vals = [r[...].astype(BF16) for r in a_refs]
            ps = [_dot(a_vals[p % n_a], b_refs[p][...].astype(BF16), dims) for p in range(n_b)]
            return [functools.reduce(lambda u, w: u + w, ps)] if summed else ps

        def finish(accs):
            res = epi(accs, [e[...] for e in e_refs]) if epi is not None else accs
            for o_ref, r in zip(o_refs, res):
                o_ref[...] = r.astype(o_ref.dtype)

        if nk == 1:
            finish(products())
        else:
            k = pl.program_id(k_axis)

            @pl.when(k == 0)
            def _():
                for acc in acc_refs:
                    acc[...] = jnp.zeros_like(acc)

            for acc, p in zip(acc_refs, products()):
                acc[...] += p

            @pl.when(k == nk - 1)
            def _():
                finish([acc[...] for acc in acc_refs])

    sem = ["parallel"] * len(grid)
    if k_axis is not None:
        sem[k_axis] = "arbitrary"
    res = pl.pallas_call(
        body, name=name, grid=grid,
        in_specs=[pl.BlockSpec(bs, im) for _, bs, im in (*lhs, *rhs, *extras)],
        out_specs=[pl.BlockSpec(bs, im) for _, _, bs, im in outs],
        out_shape=[jax.ShapeDtypeStruct(s, d) for s, d, _, _ in outs],
        scratch_shapes=[pltpu.VMEM(acc_shape, F32)] * (n_acc if nk > 1 else 0),
        compiler_params=_params(*sem),
    )(*[a for a, _, _ in (*lhs, *rhs, *extras)])
    return res


def _rowwise(name, fn, tiled, whole, out_tiled, out_sums, tm):
    rows = tiled[0].shape[0]
    n_t, n_w, n_o, n_s = len(tiled), len(whole), len(out_tiled), len(out_sums)

    def body(*refs):
        t_refs, w_refs = refs[:n_t], refs[n_t:n_t + n_w]
        o_refs = refs[n_t + n_w:n_t + n_w + n_o]
        s_refs = refs[n_t + n_w + n_o:]
        outs, sums = fn([r[...] for r in t_refs], [r[...] for r in w_refs])
        for o_ref, val in zip(o_refs, outs):
            o_ref[...] = val.astype(o_ref.dtype)

        @pl.when(pl.program_id(0) == 0)
        def _():
            for s_ref in s_refs:
                s_ref[...] = jnp.zeros_like(s_ref)

        for s_ref, val in zip(s_refs, sums):
            s_ref[...] += val

    full = lambda a: pl.BlockSpec(a.shape, lambda i, nd=a.ndim: (0,) * nd)
    res = pl.pallas_call(
        body, name=name, grid=(rows // tm,),
        in_specs=[pl.BlockSpec((tm, a.shape[1]), lambda i: (i, 0)) for a in tiled] + [full(a) for a in whole],
        out_specs=[pl.BlockSpec((tm, n), lambda i: (i, 0)) for n, _ in out_tiled]
        + [pl.BlockSpec(s, lambda i, nd=len(s): (0,) * nd) for s in out_sums],
        out_shape=[jax.ShapeDtypeStruct((rows, n), d) for n, d in out_tiled]
        + [jax.ShapeDtypeStruct(s, F32) for s in out_sums],
        compiler_params=_params("arbitrary"),
    )(*tiled, *whole)
    return res


def _colsum(v):
    return jnp.sum(v, axis=0, keepdims=True)


def _rms_parts(x):
    inv = lax.rsqrt(jnp.mean(x * x, axis=-1, keepdims=True) + NORM_EPS)
    return inv, x * inv


def _rms_mod(name, x, g, sc, sh, tm):
    def fn(t, w):
        _, xhat = _rms_parts(t[0])
        return [xhat * w[0] * (1.0 + w[1]) + w[2]], []
    return _rowwise(name, fn, [x], [g, sc, sh], [(x.shape[1], BF16)], [], tm)[0]


def _rms_mod_bwd(name, x, dh, dres, g, sc, tm):
    d = x.shape[1]

    def fn(t, w):
        x_, dh_, dres_ = t
        g_, sc_ = w
        inv, xhat = _rms_parts(x_)
        dn = dh_ * (1.0 + sc_)
        dxhat = dn * g_
        dx = dres_ + inv * (dxhat - xhat * jnp.mean(dxhat * xhat, axis=-1, keepdims=True))
        return [dx], [_colsum(dh_), _colsum(dh_ * (xhat * g_)), _colsum(dn * xhat)]
    return _rowwise(name, fn, [x, dh, dres], [g, sc], [(d, F32)], [(1, d)] * 3, tm)


def _gate_bwd(name, dx, y, gate, tm):
    d = dx.shape[1]

    def fn(t, w):
        return [t[0] * w[0]], [_colsum(t[0] * t[1].astype(F32))]
    return _rowwise(name, fn, [dx, y], [gate], [(d, BF16)], [(1, d)], tm)


def _final_loss(name, x, target, g, tm):
    d = x.shape[1]

    def fn(t, w):
        inv, xhat = _rms_parts(t[0])
        err = xhat * w[0] - t[1]
        dout = err * (1.0 / d)
        dxhat = dout * w[0]
        dx = inv * (dxhat - xhat * jnp.mean(dxhat * xhat, axis=-1, keepdims=True))
        return [dx], [_colsum(err * err) * (0.5 / d), _colsum(dout * xhat)]
    return _rowwise(name, fn, [x, target], [g], [(d, F32)], [(1, d)] * 2, tm)


def _silu_rows(name, c):
    def fn(t, w):
        return [t[0] * jax.nn.sigmoid(t[0])], []
    return _rowwise(name, fn, [c], [], [(c.shape[1], F32)], [], c.shape[0])[0]


def _gelu(x):
    return 0.5 * x * (1.0 + lax.erf(x * (1.0 / math.sqrt(2.0))))


def _gelu_grad(x):
    cdf = 0.5 * (1.0 + lax.erf(x * (1.0 / math.sqrt(2.0))))
    return cdf + x * jnp.exp(-0.5 * x * x) * (1.0 / math.sqrt(2.0 * math.pi))


def _layer_norm_parts(v):
    mu = jnp.mean(v, axis=-1, keepdims=True)
    cen = v - mu
    rstd = lax.rsqrt(jnp.mean(cen * cen, axis=-1, keepdims=True) + NORM_EPS)
    return rstd, cen * rstd


def _causal(n):
    return lax.broadcasted_iota(jnp.int32, (n, n), 0) >= lax.broadcasted_iota(jnp.int32, (n, n), 1)


def _sgu_fwd(name, pre, w_s, b_s_t, ln_g, ln_b):
    t_len, d2 = pre.shape
    d = d2 // 2
    groups = w_s.shape[0]

    def body(pre_ref, w_ref, bs_ref, g_ref, b_ref, yy_ref):
        z = _gelu(pre_ref[...])
        u, v = z[:, :d], z[:, d:]
        _, vhat = _layer_norm_parts(v)
        vn = (vhat * g_ref[...] + b_ref[...]).astype(BF16)
        mask = _causal(CHUNK)
        bs = bs_ref[...]
        for g in range(groups):
            cols = slice(g * CHUNK, (g + 1) * CHUNK)
            w = jnp.where(mask, w_ref[g], 0.0).astype(BF16)
            sv = _dot(w, vn[:, cols], NN) + bs[:, g:g + 1]
            yy_ref[:, cols] = (u[:, cols] * sv).astype(BF16)

    full = lambda a: pl.BlockSpec(a.shape, lambda i, nd=a.ndim: (0,) * nd)
    return pl.pallas_call(
        body, name=name, grid=(t_len // CHUNK,),
        in_specs=[pl.BlockSpec((CHUNK, d2), lambda i: (i, 0)), full(w_s), full(b_s_t), full(ln_g), full(ln_b)],
        out_specs=pl.BlockSpec((CHUNK, d), lambda i: (i, 0)),
        out_shape=jax.ShapeDtypeStruct((t_len, d), BF16),
        compiler_params=_params("parallel"),
    )(pre, w_s, b_s_t, ln_g, ln_b)


def _sgu_bwd(name, pre, dyy, w_s, b_s_t, ln_g, ln_b):
    t_len, d2 = pre.shape
    d = d2 // 2
    groups = w_s.shape[0]

    def body(pre_ref, dyy_ref, w_ref, bs_ref, g_ref, b_ref, dpre_ref, dw_ref, dbs_ref, dg_ref, db_ref, dbin_ref, dvn_ref):
        @pl.when(pl.program_id(0) == 0)
        def _():
            for r in (dw_ref, dbs_ref, dg_ref, db_ref, dbin_ref):
                r[...] = jnp.zeros_like(r)

        pre_v = pre_ref[...]
        z = _gelu(pre_v)
        u, v = z[:, :d], z[:, d:]
        rstd, vhat = _layer_norm_parts(v)
        vn = (vhat * g_ref[...] + b_ref[...]).astype(BF16)
        mask = _causal(CHUNK)
        bs = bs_ref[...]
        lane = lax.broadcasted_iota(jnp.int32, (CHUNK, LANE), 1)
        dbs = jnp.zeros((CHUNK, LANE), F32)
        for g in range(groups):
            cols = slice(g * CHUNK, (g + 1) * CHUNK)
            w = jnp.where(mask, w_ref[g], 0.0).astype(BF16)
            sv = _dot(w, vn[:, cols], NN) + bs[:, g:g + 1]
            dyy_g = dyy_ref[:, cols]
            dpre_ref[:, cols] = (dyy_g * sv * _gelu_grad(pre_v[:, cols])).astype(BF16)
            dsv = dyy_g * u[:, cols]
            dbs = jnp.where(lane == g, jnp.sum(dsv, axis=1, keepdims=True), dbs)
            dsv_b = dsv.astype(BF16)
            dw_ref[g] += jnp.where(mask, _dot(dsv_b, vn[:, cols], NT), 0.0)
            dvn_ref[:, cols] = _dot(w, dsv_b, TN)
        dbs_ref[...] += dbs
        dvn = dvn_ref[...]
        dg_ref[...] += _colsum(dvn * vhat)
        db_ref[...] += _colsum(dvn)
        dvhat = dvn * g_ref[...]
        dv = rstd * (dvhat - jnp.mean(dvhat, axis=-1, keepdims=True)
                     - vhat * jnp.mean(dvhat * vhat, axis=-1, keepdims=True))
        dpre_ref[:, d:] = (dv * _gelu_grad(pre_v[:, d:])).astype(BF16)
        dbin_ref[...] += _colsum(dpre_ref[...].astype(F32))

    full = lambda a: pl.BlockSpec(a.shape, lambda i, nd=a.ndim: (0,) * nd)
    acc = lambda s: pl.BlockSpec(s, lambda i, nd=len(s): (0,) * nd)
    sums = [(groups, CHUNK, CHUNK), (CHUNK, LANE), (1, d), (1, d), (1, d2)]
    return pl.pallas_call(
        body, name=name, grid=(t_len // CHUNK,),
        in_specs=[pl.BlockSpec((CHUNK, d2), lambda i: (i, 0)), pl.BlockSpec((CHUNK, d), lambda i: (i, 0)),
                  full(w_s), full(b_s_t), full(ln_g), full(ln_b)],
        out_specs=[pl.BlockSpec((CHUNK, d2), lambda i: (i, 0))] + [acc(s) for s in sums],
        out_shape=[jax.ShapeDtypeStruct((t_len, d2), BF16)] + [jax.ShapeDtypeStruct(s, F32) for s in sums],
        scratch_shapes=[pltpu.VMEM((CHUNK, d), F32)],
        compiler_params=_params("arbitrary"),
    )(pre, dyy, w_s, b_s_t, ln_g, ln_b)


def _whole(rows, cols):
    return pl.BlockSpec((rows, cols), lambda i: (0, 0))


def _cum_matrix(reverse):
    r = lax.broadcasted_iota(jnp.int32, (CHUNK, CHUNK), 0)
    c = lax.broadcasted_iota(jnp.int32, (CHUNK, CHUNK), 1)
    return jnp.where((r <= c) if reverse else (r >= c), 1.0, 0.0).astype(F32)


def _forget_cumsum(name, logits, bias):
    t_len = logits.shape[0]

    def body(fl_ref, b_ref, f_ref, ft_ref):
        tri = _cum_matrix(False)

        def step(n, carry):
            off = pl.multiple_of(n * CHUNK, CHUNK)
            xv = fl_ref[pl.ds(off, CHUNK), :] + b_ref[...]
            log_f = jnp.minimum(xv, 0.0) - jnp.log1p(jnp.exp(-jnp.abs(xv)))
            cs = jnp.dot(tri, log_f, precision=lax.Precision.HIGHEST, preferred_element_type=F32) + carry
            f_ref[pl.ds(off, CHUNK), :] = cs
            ft_ref[:, pl.ds(off, CHUNK)] = cs.T
            return cs[CHUNK - 1:CHUNK, :]

        lax.fori_loop(0, t_len // CHUNK, step, jnp.zeros((1, LANE), F32))

    return pl.pallas_call(
        body, name=name, grid=(1,),
        in_specs=[_whole(t_len, LANE), _whole(1, LANE)],
        out_specs=[_whole(t_len, LANE), _whole(LANE, t_len)],
        out_shape=[jax.ShapeDtypeStruct((t_len, LANE), F32), jax.ShapeDtypeStruct((LANE, t_len), F32)],
        compiler_params=_params("arbitrary"),
    )(logits, bias)


def _forget_bwd(name, d_cum, logits, bias):
    t_len = logits.shape[0]
    n_chunks = t_len // CHUNK

    def body(dc_ref, fl_ref, b_ref, dl_ref, db_ref, run_ref):
        @pl.when(pl.program_id(0) == 0)
        def _():
            run_ref[...] = jnp.zeros_like(run_ref)
            db_ref[...] = jnp.zeros_like(db_ref)

        rc = jnp.dot(_cum_matrix(True), dc_ref[...], precision=lax.Precision.HIGHEST,
                     preferred_element_type=F32) + run_ref[0:1, :]
        dl = rc * jax.nn.sigmoid(-(fl_ref[...] + b_ref[...]))
        dl_ref[...] = dl
        db_ref[...] += _colsum(dl)
        run_ref[...] = jnp.broadcast_to(rc[0:1, :], run_ref.shape)

    back = pl.BlockSpec((CHUNK, LANE), lambda i: (n_chunks - 1 - i, 0))
    return pl.pallas_call(
        body, name=name, grid=(n_chunks,),
        in_specs=[back, back, _whole(1, LANE)],
        out_specs=[back, _whole(1, LANE)],
        out_shape=[jax.ShapeDtypeStruct((t_len, LANE), F32), jax.ShapeDtypeStruct((1, LANE), F32)],
        scratch_shapes=[pltpu.VMEM((SUBLANE, LANE), F32)],
        compiler_params=_params("arbitrary"),
    )(d_cum, logits, bias)


def _head_column(f_tile, head):
    lane = lax.broadcasted_iota(jnp.int32, f_tile.shape, 1)
    return jnp.sum(jnp.where(lane == head, f_tile, 0.0), axis=1, keepdims=True)


def _attn_fwd(name, qkv, f_cum, f_keys, heads, tq):
    t_len = qkv.shape[0]
    scale = 1.0 / math.sqrt(CHUNK)

    def body(q_ref, k_ref, v_ref, f_ref, fk_ref, o_ref, lse_ref):
        head, i = pl.program_id(0), pl.program_id(1)
        q = q_ref[...]
        fq = _head_column(f_ref[...], head)
        row = i * tq + lax.broadcasted_iota(jnp.int32, (tq, tq), 0)
        col = lax.broadcasted_iota(jnp.int32, (tq, tq), 1)

        def step(j, carry):
            m, l, acc = carry
            off = pl.multiple_of(j * tq, tq)
            s = _dot(q, k_ref[pl.ds(off, tq), :], NT) * scale + fq - fk_ref[:, pl.ds(off, tq)]
            s = jnp.where(col + off <= row, s, NEG)
            m_new = jnp.maximum(m, jnp.max(s, axis=1, keepdims=True))
            p = jnp.exp(s - m_new)
            alpha = jnp.exp(m - m_new)
            l = alpha * l + jnp.sum(p, axis=1, keepdims=True)
            acc = alpha * acc + _dot(p.astype(BF16), v_ref[pl.ds(off, tq), :], NN)
            return m_new, l, acc

        init = (jnp.full((tq, 1), NEG, F32), jnp.zeros((tq, 1), F32), jnp.zeros((tq, CHUNK), F32))
        m, l, acc = lax.fori_loop(0, i + 1, step, init)
        o_ref[...] = (acc / l).astype(BF16)
        lse_ref[...] = jnp.broadcast_to(m + jnp.log(l), (tq, LANE))

    return pl.pallas_call(
        body, name=name, grid=(heads, t_len // tq),
        in_specs=[pl.BlockSpec((tq, CHUNK), lambda h, i: (i, h)),
                  pl.BlockSpec((t_len, CHUNK), lambda h, i: (0, heads + h)),
                  pl.BlockSpec((t_len, CHUNK), lambda h, i: (0, 2 * heads + h)),
                  pl.BlockSpec((tq, LANE), lambda h, i: (i, 0)),
                  pl.BlockSpec((None, 1, t_len), lambda h, i: (h, 0, 0))],
        out_specs=[pl.BlockSpec((tq, CHUNK), lambda h, i: (i, h)),
                   pl.BlockSpec((None, tq, LANE), lambda h, i: (h, i, 0))],
        out_shape=[jax.ShapeDtypeStruct((t_len, heads * CHUNK), BF16),
                   jax.ShapeDtypeStruct((heads, t_len, LANE), F32)],
        compiler_params=_params("parallel", "parallel"),
    )(qkv, qkv, qkv, f_cum, f_keys)


def _attn_delta(name, d_o, o, heads, tq):
    t_len = o.shape[0]

    def body(do_ref, o_ref, dl_ref):
        dl = jnp.sum(do_ref[...].astype(F32) * o_ref[...].astype(F32), axis=1, keepdims=True)
        dl_ref[...] = jnp.broadcast_to(dl, (tq, LANE))

    return pl.pallas_call(
        body, name=name, grid=(heads, t_len // tq),
        in_specs=[pl.BlockSpec((tq, CHUNK), lambda h, i: (i, h))] * 2,
        out_specs=pl.BlockSpec((None, tq, LANE), lambda h, i: (h, i, 0)),
        out_shape=jax.ShapeDtypeStruct((heads, t_len, LANE), F32),
        compiler_params=_params("parallel", "parallel"),
    )(d_o, o)


def _attn_bwd(name, qkv, d_o, f_cum, f_keys, lse, delta, heads, tq):
    t_len = qkv.shape[0]
    d = heads * CHUNK
    n_q = t_len // tq
    scale = 1.0 / math.sqrt(CHUNK)

    def body(q_ref, k_ref, v_ref, do_ref, f_ref, fk_ref, lse_ref, dl_ref, dq_ref, dk_ref, dv_ref, dfq_ref, dfk_ref):
        head, j = pl.program_id(0), pl.program_id(1)

        @pl.when(j == 0)
        def _():
            dq_ref[...] = jnp.zeros_like(dq_ref)
            dfq_ref[...] = jnp.zeros_like(dfq_ref)

        k, v, fk = k_ref[...], v_ref[...], fk_ref[...]
        col = j * tq + lax.broadcasted_iota(jnp.int32, (tq, tq), 1)
        row = lax.broadcasted_iota(jnp.int32, (tq, tq), 0)

        def step(i, carry):
            dk, dv, dfk = carry
            off = pl.multiple_of(i * tq, tq)
            rows = pl.ds(off, tq)
            q, do = q_ref[rows, :], do_ref[rows, :]
            fq = _head_column(f_ref[rows, :], head)
            s = _dot(q, k, NT) * scale + fq - fk
            s = jnp.where(col <= row + off, s, NEG)
            p = jnp.exp(s - lse_ref[rows, :][:, 0:1])
            ds = p * (_dot(do, v, NT) - dl_ref[rows, :][:, 0:1])
            ds_b = ds.astype(BF16)
            dq_ref[rows, :] += _dot(ds_b, k, NN) * scale
            dfq_ref[rows, :] += jnp.broadcast_to(jnp.sum(ds, axis=1, keepdims=True), (tq, LANE))
            return (dk + _dot(ds_b, q, TN), dv + _dot(p.astype(BF16), do, TN), dfk + _colsum(ds))

        zero = jnp.zeros((tq, CHUNK), F32)
        dk, dv, dfk = lax.fori_loop(j, n_q, step, (zero, zero, jnp.zeros((1, tq), F32)))
        dk_ref[...] = (dk * scale).astype(BF16)
        dv_ref[...] = dv.astype(BF16)
        dfk_ref[...] = dfk

    whole_head = lambda c0: pl.BlockSpec((t_len, CHUNK), lambda h, j: (0, c0 + h))
    per_head = pl.BlockSpec((None, t_len, LANE), lambda h, j: (h, 0, 0))
    return pl.pallas_call(
        body, name=name, grid=(heads, n_q),
        in_specs=[whole_head(0),
                  pl.BlockSpec((tq, CHUNK), lambda h, j: (j, heads + h)),
                  pl.BlockSpec((tq, CHUNK), lambda h, j: (j, 2 * heads + h)),
                  whole_head(0),
                  pl.BlockSpec((t_len, LANE), lambda h, j: (0, 0)),
                  pl.BlockSpec((None, 1, tq), lambda h, j: (h, 0, j)),
                  per_head, per_head],
        out_specs=[whole_head(0),
                   pl.BlockSpec((tq, CHUNK), lambda h, j: (j, h)),
                   pl.BlockSpec((tq, CHUNK), lambda h, j: (j, h)),
                   per_head,
                   pl.BlockSpec((None, 1, tq), lambda h, j: (h, 0, j))],
        out_shape=[jax.ShapeDtypeStruct((t_len, d), F32), jax.ShapeDtypeStruct((t_len, d), BF16),
                   jax.ShapeDtypeStruct((t_len, d), BF16), jax.ShapeDtypeStruct((heads, t_len, LANE), F32),
                   jax.ShapeDtypeStruct((heads, 1, t_len), F32)],
        compiler_params=_params("parallel", "arbitrary"),
    )(qkv, qkv, qkv, d_o, f_cum, f_keys, lse, delta)


def _ada_grad(name, c_act_t, dmod, tm):
    d = c_act_t.shape[0]
    n_layer, n_b, n_col = dmod.shape

    def body(c_ref, dm_ref, o_ref):
        c, dm = c_ref[...], dm_ref[...]
        acc = c[:, 0:1] * dm[0:1, :]
        for b in range(1, N_DEV):
            acc = acc + c[:, b:b + 1] * dm[b:b + 1, :]
        o_ref[...] = acc

    return pl.pallas_call(
        body, name=name, grid=(n_layer, d // tm),
        in_specs=[pl.BlockSpec((tm, LANE), lambda l, i: (i, 0)),
                  pl.BlockSpec((None, n_b, n_col), lambda l, i: (l, 0, 0))],
        out_specs=pl.BlockSpec((None, tm, n_col), lambda l, i: (l, i, 0)),
        out_shape=jax.ShapeDtypeStruct((n_layer, d, n_col), F32),
        compiler_params=_params("parallel", "parallel"),
    )(c_act_t, dmod)


def _adamw(name, w, parts, m, v):
    rows, cols = w.shape
    n_parts = parts.shape[0]
    tr = _tile(rows, max(SUBLANE, (1 << 19) // cols), SUBLANE)

    def body(w_ref, p_ref, m_ref, v_ref, g_ref, d_ref, mo_ref, vo_ref):
        g = p_ref[0].astype(F32)
        for p in range(1, n_parts):
            g = g + p_ref[p].astype(F32)
        m_new = ADAM_B1 * m_ref[...] + (1.0 - ADAM_B1) * g
        v_new = ADAM_B2 * v_ref[...] + (1.0 - ADAM_B2) * jnp.square(g)
        m_hat = m_new / (1.0 - ADAM_B1 ** ADAM_STEP)
        v_hat = v_new / (1.0 - ADAM_B2 ** ADAM_STEP)
        g_ref[...] = g
        d_ref[...] = -ADAM_LR * (m_hat / (jnp.sqrt(v_hat) + ADAM_EPS) + ADAM_WD * w_ref[...])
        mo_ref[...] = m_new
        vo_ref[...] = v_new

    blk = pl.BlockSpec((tr, cols), lambda i: (i, 0))
    return pl.pallas_call(
        body, name=name, grid=(rows // tr,),
        in_specs=[blk, pl.BlockSpec((n_parts, tr, cols), lambda i: (0, i, 0)), blk, blk],
        out_specs=[blk] * 4,
        out_shape=[jax.ShapeDtypeStruct((rows, cols), F32)] * 4,
        compiler_params=_params("parallel"),
    )(w, parts, m, v)


def _dev_index(p):
    return 4 * p[0] + 2 * p[1] + p[2]


def _other_chips(mx, my):
    return [(1 - mx, my), (mx, 1 - my), (1 - mx, 1 - my)]


def _all_gather_small(name, x):
    rows, cols = x.shape

    def body(x_ref, o_ref, send_sems, recv_sems):
        mx, my, mc = _place()
        me = _dev_index((mx, my, mc))
        o_ref[me] = x_ref[...]

        def copy(dist, slot, peer):
            return pltpu.make_async_remote_copy(
                src_ref=x_ref, dst_ref=o_ref.at[slot], send_sem=send_sems.at[dist - 1], recv_sem=recv_sems.at[dist - 1],
                device_id=(peer // 4, (peer // 2) % 2, peer % 2), device_id_type=MESH)

        sends = [copy(dist, me, (me + dist) % N_DEV) for dist in range(1, N_DEV)]
        for cp in sends:
            cp.start()
        for dist in range(1, N_DEV):
            src = (me + N_DEV - dist) % N_DEV
            copy(dist, src, src).wait_recv()
        for cp in sends:
            cp.wait_send()

    return pl.pallas_call(
        body, name=name,
        out_shape=jax.ShapeDtypeStruct((N_DEV, rows, cols), x.dtype),
        in_specs=[pl.BlockSpec(memory_space=pltpu.VMEM)],
        out_specs=pl.BlockSpec(memory_space=pltpu.VMEM),
        scratch_shapes=[pltpu.SemaphoreType.DMA((N_DEV - 1,)), pltpu.SemaphoreType.DMA((N_DEV - 1,))],
        compiler_params=_params(),
    )(x)


def _all_gather_big(name, shards):
    n = len(shards)

    def body(*refs):
        x_refs, o_refs = refs[:n], refs[n:2 * n]
        send_sems, recv_sems, local_sems = refs[2 * n:]
        mx, my, mc = _place()
        me, sibling = (mx, my, mc), (mx, my, 1 - mc)
        chips = _other_chips(mx, my)

        def copy(a, k, block, to, src=None):
            dst = o_refs[a].at[_dev_index(block)]
            return pltpu.make_async_remote_copy(
                src_ref=dst if src is None else src, dst_ref=dst,
                send_sem=send_sems.at[7 * a + k], recv_sem=recv_sems.at[7 * a + k],
                device_id=to, device_id_type=MESH)

        mine, first, passed = [], [], []
        for a in range(n):
            own = pltpu.make_async_copy(x_refs[a], o_refs[a].at[_dev_index(me)], local_sems.at[a])
            own.start()
            mine.append(own)
            first.append(copy(a, 0, me, sibling, src=x_refs[a]))
            first += [copy(a, 1 + j, me, (*chip, mc), src=x_refs[a]) for j, chip in enumerate(chips)]
        for cp in first:
            cp.start()
        for j, chip in enumerate(chips):
            for a in range(n):
                copy(a, 1 + j, (*chip, mc), me).wait_recv()
                fwd = copy(a, 4 + j, (*chip, mc), sibling)
                fwd.start()
                passed.append(fwd)
        for a in range(n):
            copy(a, 0, sibling, me).wait_recv()
            for j, chip in enumerate(chips):
                copy(a, 4 + j, (*chip, 1 - mc), me).wait_recv()
        for cp in first + passed:
            cp.wait_send()
        for own in mine:
            own.wait()

    return pl.pallas_call(
        body, name=name,
        out_shape=[jax.ShapeDtypeStruct((N_DEV, *s.shape), s.dtype) for s in shards],
        in_specs=[ANY] * n, out_specs=[ANY] * n,
        scratch_shapes=[pltpu.SemaphoreType.DMA((7 * n,)), pltpu.SemaphoreType.DMA((7 * n,)),
                        pltpu.SemaphoreType.DMA((n,))],
        compiler_params=_params(),
    )(*shards)


def _exchange_sibling(name, grads):
    n = len(grads)

    def body(*refs):
        g_refs, r_refs = refs[:n], refs[n:2 * n]
        send_sems, recv_sems = refs[2 * n:]
        mx, my, mc = _place()

        def copy(a, q):
            return pltpu.make_async_remote_copy(
                src_ref=g_refs[a].at[q, 1 - mc], dst_ref=r_refs[a].at[q],
                send_sem=send_sems.at[N_CHIP * a + q], recv_sem=recv_sems.at[N_CHIP * a + q],
                device_id=(mx, my, 1 - mc), device_id_type=MESH)

        copies = [copy(a, q) for a in range(n) for q in range(N_CHIP)]
        for cp in copies:
            cp.start()
        for cp in copies:
            cp.wait_recv()
        for cp in copies:
            cp.wait_send()

    return pl.pallas_call(
        body, name=name,
        out_shape=[jax.ShapeDtypeStruct((N_CHIP, *g.shape[2:]), g.dtype) for g in grads],
        in_specs=[ANY] * n, out_specs=[ANY] * n,
        scratch_shapes=[pltpu.SemaphoreType.DMA((N_CHIP * n,)), pltpu.SemaphoreType.DMA((N_CHIP * n,))],
        compiler_params=_params(),
    )(*grads)


def _add_sibling(name, grad, recv, core):
    _, _, rows, cols = grad.shape
    tr = _tile(rows, max(2 * SUBLANE, (1 << 19) // cols), 2 * SUBLANE)

    def body(c_ref, g_ref, r_ref, o_ref):
        o_ref[...] = (g_ref[...].astype(F32) + r_ref[...].astype(F32)).astype(o_ref.dtype)

    return pl.pallas_call(
        body, name=name,
        grid_spec=pltpu.PrefetchScalarGridSpec(
            num_scalar_prefetch=1, grid=(N_CHIP, rows // tr),
            in_specs=[pl.BlockSpec((None, None, tr, cols), lambda q, i, c: (q, c[0], i, 0)),
                      pl.BlockSpec((None, tr, cols), lambda q, i, c: (q, i, 0))],
            out_specs=pl.BlockSpec((None, tr, cols), lambda q, i, c: (q, i, 0))),
        out_shape=jax.ShapeDtypeStruct(recv.shape, recv.dtype),
        compiler_params=_params("parallel", "parallel"),
    )(core, grad, recv)


def _exchange_chips(name, parts):
    n = len(parts)

    def body(*refs):
        p_refs, r_refs = refs[:n], refs[n:2 * n]
        send_sems, recv_sems, local_sems = refs[2 * n:]
        mx, my, mc = _place()
        my_chip = 2 * mx + my
        chips = _other_chips(mx, my)

        def copy(a, j, chip, block, slot):
            return pltpu.make_async_remote_copy(
                src_ref=p_refs[a].at[block], dst_ref=r_refs[a].at[slot],
                send_sem=send_sems.at[3 * a + j], recv_sem=recv_sems.at[3 * a + j],
                device_id=(*chip, mc), device_id_type=MESH)

        mine = [pltpu.make_async_copy(p_refs[a].at[my_chip], r_refs[a].at[my_chip], local_sems.at[a]) for a in range(n)]
        sends = [copy(a, j, chip, 2 * chip[0] + chip[1], my_chip) for a in range(n) for j, chip in enumerate(chips)]
        for cp in mine + sends:
            cp.start()
        for a in range(n):
            for j, chip in enumerate(chips):
                copy(a, j, chip, my_chip, 2 * chip[0] + chip[1]).wait_recv()
        for cp in sends:
            cp.wait_send()
        for cp in mine:
            cp.wait()

    return pl.pallas_call(
        body, name=name,
        out_shape=[jax.ShapeDtypeStruct(p.shape, p.dtype) for p in parts],
        in_specs=[ANY] * n, out_specs=[ANY] * n,
        scratch_shapes=[pltpu.SemaphoreType.DMA((3 * n,)), pltpu.SemaphoreType.DMA((3 * n,)),
                        pltpu.SemaphoreType.DMA((n,))],
        compiler_params=_params(),
    )(*parts)


def _row(v):
    return v.reshape(1, -1)


def _out_proj(name, act, w_out, x, gate, tm, tn):
    t_len, d = x.shape
    k = act.shape[1]
    return _mm(name, (t_len // tm, d // tn),
               [(act, (tm, k), lambda i, j: (i, 0))], [(w_out, (k, tn), lambda i, j: (0, j))], NN,
               [((t_len, d), BF16, (tm, tn), lambda i, j: (i, j)), ((t_len, d), F32, (tm, tn), lambda i, j: (i, j))],
               epi=lambda accs, e: [accs[0], e[0] + e[1] * accs[0]],
               extras=[(x, (tm, tn), lambda i, j: (i, j)), (gate, (1, tn), lambda i, j: (0, j))])


def _proj_bwd(name, dy, w_out, dtype, tm, tn):
    t_len, d = dy.shape
    k = w_out.shape[0]
    return _mm(name, (t_len // tm, k // tn),
               [(dy, (tm, d), lambda i, j: (i, 0))], [(w_out, (tn, d), lambda i, j: (j, 0))], NT,
               [((t_len, k), dtype, (tm, tn), lambda i, j: (i, j))])[0]


def _weight_grad(name, act, dy, tm, tn):
    t_len, k = act.shape
    n = dy.shape[1]
    return _mm(name, (k // tm, n // tn),
               [(act, (t_len, tm), lambda i, j: (0, i))], [(dy, (t_len, tn), lambda i, j: (0, j))], TN,
               [((k, n), BF16, (tm, tn), lambda i, j: (i, j))])[0]


def _ffn_fwd(tag, x1, mod, g_norm, w_gate, w_up, w_down, tm):
    t_len, d = x1.shape
    fs = w_gate.shape[2]
    sh2, sc2, g2 = mod[3], mod[4], mod[5]
    h2 = _rms_mod(f"{tag}_ffn_norm", x1, g_norm, sc2, sh2, tm)
    hidden = ((N_DEV, t_len, fs), BF16, (None, tm, fs), lambda j, i: (j, i, 0))

    def swiglu(accs, _):
        a, b = accs
        return [a, b, a * jax.nn.sigmoid(a) * b]

    a, b, s = _mm(f"{tag}_ffn_up", (N_DEV, t_len // tm),
                  [(h2, (tm, d), lambda j, i: (i, 0))],
                  [(w_gate, (None, d, fs), lambda j, i: (j, 0, 0)), (w_up, (None, d, fs), lambda j, i: (j, 0, 0))],
                  NN, [hidden] * 3, epi=swiglu, summed=False)
    f, x2 = _mm(f"{tag}_ffn_down", (t_len // tm, 1, N_DEV),
                [(s, (None, tm, fs), lambda i, j, k: (k, i, 0))], [(w_down, (None, fs, d), lambda i, j, k: (k, 0, 0))],
                NN,
                [((t_len, d), BF16, (tm, d), lambda i, j, k: (i, 0)), ((t_len, d), F32, (tm, d), lambda i, j, k: (i, 0))],
                epi=lambda accs, e: [accs[0], e[0] + e[1] * accs[0]],
                extras=[(x1, (tm, d), lambda i, j, k: (i, 0)), (g2, (1, d), lambda i, j, k: (0, 0))], k_axis=2)
    return x2, (h2, a, b, s, f)


def _ffn_bwd(tag, dx2, x1, saved, mod, g_norm, w_gate, w_up, w_down, tm):
    t_len, d = x1.shape
    fs = w_gate.shape[2]
    h2, a, b, s, f = saved
    sc2, g2 = mod[4], mod[5]
    df, dg2 = _gate_bwd(f"{tag}_ffn_gate_bwd", dx2, f, g2, tm)
    hidden = ((N_DEV, t_len, fs), BF16, (None, tm, fs), lambda j, i: (j, i, 0))
    hid_in = lambda arr: (arr, (None, tm, fs), lambda j, i: (j, i, 0))

    def swiglu_bwd(accs, e):
        a_, b_ = e[0].astype(F32), e[1].astype(F32)
        sig = jax.nn.sigmoid(a_)
        return [accs[0] * b_ * sig * (1.0 + a_ * (1.0 - sig)), accs[0] * a_ * sig]

    da, db = _mm(f"{tag}_ffn_down_bwd", (N_DEV, t_len // tm),
                 [(df, (tm, d), lambda j, i: (i, 0))], [(w_down, (None, fs, d), lambda j, i: (j, 0, 0))], NT,
                 [hidden] * 2, epi=swiglu_bwd, extras=[hid_in(a), hid_in(b)])
    tn = _tile(d, 512)
    d_wd = _mm(f"{tag}_ffn_wdown_grad", (N_DEV, d // tn),
               [(s, (None, t_len, fs), lambda j, i: (j, 0, 0))], [(df, (t_len, tn), lambda j, i: (0, i))], TN,
               [((N_DEV, fs, d), BF16, (None, fs, tn), lambda j, i: (j, 0, i))])[0]
    w_grad = ((N_DEV, d, fs), BF16, (None, tn, fs), lambda j, i: (j, i, 0))
    d_wg, d_wu = _mm(f"{tag}_ffn_wup_grad", (N_DEV, d // tn),
                     [(h2, (t_len, tn), lambda j, i: (0, i))],
                     [(da, (None, t_len, fs), lambda j, i: (j, 0, 0)), (db, (None, t_len, fs), lambda j, i: (j, 0, 0))],
                     TN, [w_grad] * 2, summed=False)
    dh2 = _mm(f"{tag}_ffn_up_bwd", (t_len // tm, 1, N_DEV),
              [(da, (None, tm, fs), lambda i, j, k: (k, i, 0)), (db, (None, tm, fs), lambda i, j, k: (k, i, 0))],
              [(w_gate, (None, d, fs), lambda i, j, k: (k, 0, 0)), (w_up, (None, d, fs), lambda i, j, k: (k, 0, 0))],
              NT, [((t_len, d), F32, (tm, d), lambda i, j, k: (i, 0))], k_axis=2)[0]
    dx1, dsh2, dsc2, dgn = _rms_mod_bwd(f"{tag}_ffn_norm_bwd", x1, dh2, dx2, g_norm, sc2, tm // 2)
    return dx1, (d_wg, d_wu, d_wd), (dsh2, dsc2, dg2, dgn)


def kernel(x, c, ada_w, ada_b, norm_mix_g, norm_ffn_g, a_w_in, a_b_in, a_ln_g, a_ln_b, a_w_s, a_b_s, a_w_out, b_w_in, b_b_f, b_w_out, ffn_w_gate, ffn_w_up, ffn_w_down, final_g, loss_target, m_ada_w, m_ada_b, m_norm_mix_g, m_norm_ffn_g, m_a_w_in, m_a_b_in, m_a_ln_g, m_a_ln_b, m_a_w_s, m_a_b_s, m_a_w_out, m_b_w_in, m_b_b_f, m_b_w_out, m_ffn_w_gate, m_ffn_w_up, m_ffn_w_down, m_final_g, v_ada_w, v_ada_b, v_norm_mix_g, v_norm_ffn_g, v_a_w_in, v_a_b_in, v_a_ln_g, v_a_ln_b, v_a_w_s, v_a_b_s, v_a_w_out, v_b_w_in, v_b_b_f, v_b_w_out, v_ffn_w_gate, v_ffn_w_up, v_ffn_w_down, v_final_g):
    t_len, d = x.shape[1], x.shape[2]
    heads = d // CHUNK
    groups = a_w_s.shape[1]
    d_mod = 6 * d
    mod_cols = ada_w.shape[2]
    tm = _tile(t_len, 512)
    tn = _tile(d, 512)
    tq = _tile(t_len, 256)
    mx, my, mc = _place()
    me = _dev_index((mx, my, mc))
    x0, target = x[0], loss_target[0]

    shards = [a_w_in[0], a_w_out[0], b_w_in[0], b_w_out[0],
              ffn_w_gate[0], ffn_w_up[0], ffn_w_down[0], ffn_w_gate[1], ffn_w_up[1], ffn_w_down[1]]
    gathered = _all_gather_big("gather_weights", [s.astype(BF16) for s in shards])
    w_a_in, w_a_out, w_b_in, w_b_out = gathered[:4]
    w_ffn = [gathered[4:7], gathered[7:10]]
    w_a_out = w_a_out.reshape(d, d)
    w_b_out = w_b_out.reshape(d, d)
    w_b_full = w_b_in.transpose(1, 0, 2).reshape(d, 3 * d + heads)
    w_qkv = w_b_full[:, :3 * d]
    w_f = jnp.pad(w_b_full[:, 3 * d:], ((0, 0), (0, LANE - heads)))

    c_all = _all_gather_small("gather_c", jnp.pad(c, ((0, SUBLANE - 1), (0, 0))))[:, 0, :]
    c_act = _silu_rows("silu_c", jnp.pad(c_all, ((0, 2 * SUBLANE - N_DEV), (0, 0))))
    mod_part = _mm("mod_matmul", (2, mod_cols // _tile(mod_cols, 512)),
                   [(c_act, (2 * SUBLANE, d), lambda l, j: (0, 0))],
                   [(ada_w, (None, d, _tile(mod_cols, 512)), lambda l, j: (l, 0, j))], NN,
                   [((2, 2 * SUBLANE, mod_cols), F32, (None, 2 * SUBLANE, _tile(mod_cols, 512)), lambda l, j: (l, 0, j))])[0]
    mod_all = _all_gather_small("gather_mod", mod_part[:, :N_DEV, :].reshape(2 * N_DEV, mod_cols))
    mod_mine = lax.dynamic_index_in_dim(mod_all.reshape(N_DEV, 2, N_DEV, mod_cols), me, axis=2, keepdims=False)
    mod = mod_mine.transpose(1, 0, 2).reshape(2, d_mod) + ada_b
    mods = [[_row(mod[l, k * d:(k + 1) * d]) for k in range(6)] for l in range(2)]

    g_mix0, g_ffn0 = _row(norm_mix_g[0]), _row(norm_ffn_g[0])
    na = w_a_in.shape[2]
    h_a = _rms_mod("l0_mix_norm", x0, g_mix0, mods[0][1], mods[0][0], tm)
    pre = _mm("l0_sgu_in", (t_len // tm, N_DEV),
              [(h_a, (tm, d), lambda i, j: (i, 0))], [(w_a_in, (None, d, na), lambda i, j: (j, 0, 0))], NN,
              [((t_len, 2 * d), F32, (tm, na), lambda i, j: (i, j))],
              epi=lambda accs, e: [accs[0] + e[0]], extras=[(a_b_in, (1, na), lambda i, j: (0, j))])[0]
    w_s, b_s_t = a_w_s[0], jnp.pad(a_b_s[0].T, ((0, 0), (0, LANE - groups)))
    ln_g, ln_b = a_ln_g, a_ln_b
    yy = _sgu_fwd("l0_sgu_mix", pre, w_s, b_s_t, ln_g, ln_b)
    y_a, x1 = _out_proj("l0_sgu_out", yy, w_a_out, x0, mods[0][2], tm, tn)
    x2, ffn0_saved = _ffn_fwd("l0", x1, mods[0], g_ffn0, *w_ffn[0], tm)

    g_mix1, g_ffn1 = _row(norm_mix_g[1]), _row(norm_ffn_g[1])
    h_b = _rms_mod("l1_mix_norm", x2, g_mix1, mods[1][1], mods[1][0], tm)
    qkv = _mm("l1_qkv", (t_len // tm, 3 * d // tn),
              [(h_b, (tm, d), lambda i, j: (i, 0))], [(w_qkv, (d, tn), lambda i, j: (0, j))], NN,
              [((t_len, 3 * d), BF16, (tm, tn), lambda i, j: (i, j))])[0]
    f_logit = _mm("l1_forget_logit", (t_len // tm, 1),
                  [(h_b, (tm, d), lambda i, j: (i, 0))], [(w_f, (d, LANE), lambda i, j: (0, 0))], NN,
                  [((t_len, LANE), F32, (tm, LANE), lambda i, j: (i, 0))])[0]
    b_f = jnp.pad(b_b_f, ((0, 0), (0, LANE - heads)))
    f_cum, f_cum_t = _forget_cumsum("l1_forget_cumsum", f_logit, b_f)
    f_keys = f_cum_t[:heads].reshape(heads, 1, t_len)
    o, lse = _attn_fwd("l1_attn", qkv, f_cum, f_keys, heads, tq)
    y_b, x3 = _out_proj("l1_attn_out", o, w_b_out, x2, mods[1][2], tm, tn)
    x4, ffn1_saved = _ffn_fwd("l1", x3, mods[1], g_ffn1, *w_ffn[1], tm)

    dx4, loss_cols, d_final_g = _final_loss("loss_head", x4, target, _row(final_g), tm // 2)
    loss = lax.psum(jnp.sum(loss_cols), ("x", "y", "c"))

    dx3, ffn1_grads, (dsh2_1, dsc2_1, dg2_1, dgf_1) = _ffn_bwd("l1", dx4, x3, ffn1_saved, mods[1], g_ffn1, *w_ffn[1], tm)
    dy_b, dg1_1 = _gate_bwd("l1_attn_gate_bwd", dx3, y_b, mods[1][2], tm)
    d_o = _proj_bwd("l1_attn_out_bwd", dy_b, w_b_out, BF16, tm, tn)
    d_w_b_out = _weight_grad("l1_attn_wout_grad", o, dy_b, tn, tn)
    delta = _attn_delta("l1_attn_delta", d_o, o, heads, tq)
    dq, dk, dv, dfq, dfk = _attn_bwd("l1_attn_bwd", qkv, d_o, f_cum, f_keys, lse, delta, heads, tq)
    dqkv = jnp.concatenate([dq.astype(BF16), dk, dv], axis=1)
    d_cum = jnp.pad((dfq[:, :, 0] - dfk[:, 0, :]).T, ((0, 0), (0, LANE - heads)))
    d_logit, d_b_f = _forget_bwd("l1_forget_bwd", d_cum, f_logit, b_f)
    d_logit = d_logit.astype(BF16)
    d_w_qkv = _weight_grad("l1_wqkv_grad", h_b, dqkv, tn, tn)
    d_w_f = _weight_grad("l1_wf_grad", h_b, d_logit, tn, LANE)
    dh_b = _mm("l1_qkv_bwd", (t_len // tm, d // tn),
               [(dqkv, (tm, 3 * d), lambda i, j: (i, 0)), (d_logit, (tm, LANE), lambda i, j: (i, 0))],
               [(w_qkv, (tn, 3 * d), lambda i, j: (j, 0)), (w_f, (tn, LANE), lambda i, j: (j, 0))], NT,
               [((t_len, d), F32, (tm, tn), lambda i, j: (i, j))])[0]
    dx2, dsh1_1, dsc1_1, dgm_1 = _rms_mod_bwd("l1_mix_norm_bwd", x2, dh_b, dx3, g_mix1, mods[1][1], tm // 2)
    qs = b_w_in.shape[2]
    d_w_b_in = jnp.concatenate([d_w_qkv, d_w_f[:, :heads]], axis=1).reshape(d, N_DEV, qs).transpose(1, 0, 2)

    dx1, ffn0_grads, (dsh2_0, dsc2_0, dg2_0, dgf_0) = _ffn_bwd("l0", dx2, x1, ffn0_saved, mods[0], g_ffn0, *w_ffn[0], tm)
    dy_a, dg1_0 = _gate_bwd("l0_sgu_gate_bwd", dx1, y_a, mods[0][2], tm)
    dyy = _proj_bwd("l0_sgu_out_bwd", dy_a, w_a_out, F32, tm, tn)
    d_w_a_out = _weight_grad("l0_sgu_wout_grad", yy, dy_a, tn, tn)
    dpre, d_w_s, d_b_s_t, d_ln_g, d_ln_b, d_b_in = _sgu_bwd("l0_sgu_mix_bwd", pre, dyy, w_s, b_s_t, ln_g, ln_b)
    d_w_a_in = _mm("l0_sgu_win_grad", (N_DEV, d // tn),
                   [(h_a, (t_len, tn), lambda j, i: (0, i))], [(dpre, (t_len, na), lambda j, i: (0, j))], TN,
                   [((N_DEV, d, na), BF16, (None, tn, na), lambda j, i: (j, i, 0))])[0]
    dh_a = _mm("l0_sgu_in_bwd", (t_len // tm, d // tn, N_DEV),
               [(dpre, (tm, na), lambda i, j, k: (i, k))], [(w_a_in, (None, tn, na), lambda i, j, k: (k, j, 0))], NT,
               [((t_len, d), F32, (tm, tn), lambda i, j, k: (i, j))], k_axis=2)[0]
    dx0, dsh1_0, dsc1_0, dgm_0 = _rms_mod_bwd("l0_mix_norm_bwd", x0, dh_a, dx1, g_mix0, mods[0][1], tm // 2)

    wide = lambda g: g.reshape(N_CHIP, 2, *g.shape[1:])
    big_grads = [d_w_a_in, d_w_a_out.reshape(N_DEV, d // N_DEV, d), d_w_b_in, d_w_b_out.reshape(N_DEV, d // N_DEV, d),
                 *ffn0_grads, *ffn1_grads]
    big_grads = [wide(g) for g in big_grads]
    from_sibling = _exchange_sibling("reduce_sibling", big_grads)
    core = mc.astype(jnp.int32).reshape(1)
    chip_parts = [_add_sibling(f"add_sibling_{n}", g, r, core) for n, (g, r) in enumerate(zip(big_grads, from_sibling))]
    from_chips = _exchange_chips("reduce_chips", chip_parts)
    big = [(a_w_in[0], m_a_w_in[0], v_a_w_in[0]), (a_w_out[0], m_a_w_out[0], v_a_w_out[0]),
           (b_w_in[0], m_b_w_in[0], v_b_w_in[0]), (b_w_out[0], m_b_w_out[0], v_b_w_out[0]),
           (ffn_w_gate[0], m_ffn_w_gate[0], v_ffn_w_gate[0]), (ffn_w_up[0], m_ffn_w_up[0], v_ffn_w_up[0]),
           (ffn_w_down[0], m_ffn_w_down[0], v_ffn_w_down[0]),
           (ffn_w_gate[1], m_ffn_w_gate[1], v_ffn_w_gate[1]), (ffn_w_up[1], m_ffn_w_up[1], v_ffn_w_up[1]),
           (ffn_w_down[1], m_ffn_w_down[1], v_ffn_w_down[1])]
    big_out = [_adamw(f"adamw_big_{n}", w, parts, m, v) for n, ((w, m, v), parts) in enumerate(zip(big, from_chips))]
    r_a_in, r_a_out, r_b_in, r_b_out = [[t[None] for t in res] for res in big_out[:4]]
    r_gate, r_up, r_down = [[jnp.stack([p, q]) for p, q in zip(big_out[4 + k], big_out[7 + k])] for k in range(3)]

    dmod = jnp.concatenate([dsh1_0, dsc1_0, dg1_0, dsh2_0, dsc2_0, dg2_0,
                            dsh1_1, dsc1_1, dg1_1, dsh2_1, dsc2_1, dg2_1], axis=1)
    small_grads = [dmod, jnp.concatenate([dgm_0, dgm_1], axis=1), jnp.concatenate([dgf_0, dgf_1], axis=1),
                   d_b_in, d_ln_g, d_ln_b, d_w_s, d_b_s_t[:, :groups].T, d_b_f[:, :heads], d_final_g]
    small_w = [ada_b, norm_mix_g, norm_ffn_g, a_b_in, a_ln_g, a_ln_b, a_w_s, a_b_s, b_b_f, final_g]
    small_m = [m_ada_b, m_norm_mix_g, m_norm_ffn_g, m_a_b_in, m_a_ln_g, m_a_ln_b, m_a_w_s, m_a_b_s, m_b_b_f, m_final_g]
    small_v = [v_ada_b, v_norm_mix_g, v_norm_ffn_g, v_a_b_in, v_a_ln_g, v_a_ln_b, v_a_w_s, v_a_b_s, v_b_b_f, v_final_g]
    n_small = sum(w.size for w in small_w)
    pack_rows = -(-n_small // (PACK_W * SUBLANE)) * SUBLANE

    def pack(arrs):
        flat = jnp.concatenate([a.reshape(-1) for a in arrs])
        return jnp.pad(flat, (0, pack_rows * PACK_W - n_small)).reshape(pack_rows, PACK_W)

    def unpack(packed):
        flat, out, pos = packed.reshape(-1), [], 0
        for w in small_w:
            out.append(flat[pos:pos + w.size].reshape(w.shape))
            pos += w.size
        return out

    all_small = _all_gather_small("gather_small_grads", pack(small_grads))
    small_out = [unpack(t) for t in _adamw("adamw_small", pack(small_w), all_small, pack(small_m), pack(small_v))]

    dmod_all = all_small.reshape(N_DEV, -1)[:, :2 * d_mod].reshape(N_DEV, 2, d_mod)
    dmod_cols = lax.dynamic_slice_in_dim(dmod_all, me * mod_cols, mod_cols, axis=2).transpose(1, 0, 2)
    c_act_t = jnp.pad(c_act[:N_DEV].T, ((0, 0), (0, LANE - N_DEV)))
    g_ada = _ada_grad("ada_w_grad", c_act_t, dmod_cols, _tile(d, 256))
    r_ada = [t.reshape(ada_w.shape) for t in _adamw(
        "adamw_ada_w", ada_w.reshape(2 * d, mod_cols), g_ada.reshape(1, 2 * d, mod_cols),
        m_ada_w.reshape(2 * d, mod_cols), v_ada_w.reshape(2 * d, mod_cols))]

    def leaf(k):
        s = small_out[k]
        return [r_ada[k], s[0], s[1], s[2], r_a_in[k], s[3], s[4], s[5], s[6], s[7], r_a_out[k],
                r_b_in[k], s[8], r_b_out[k], r_gate[k], r_up[k], r_down[k], s[9]]

    return (loss, dx0[None], *leaf(0), *leaf(1), *leaf(2), *leaf(3))
```

```python
import functools
import math

import jax
import jax.numpy as jnp
from jax import lax
from jax.experimental import pallas as pl
from jax.experimental.pallas import tpu as pltpu

F32 = jnp.float32
BF16 = jnp.bfloat16
MESH = pl.DeviceIdType.MESH
ANY = pl.BlockSpec(memory_space=pl.ANY)
HBM = pl.BlockSpec(memory_space=pltpu.HBM)
SEM = pl.BlockSpec(memory_space=pltpu.SEMAPHORE)
VMEM_SPEC = pl.BlockSpec(memory_space=pltpu.VMEM)
EFFECT = pltpu.SideEffectType.DATAFLOW_SIDE_EFFECTING

N_DEV = 8
N_CHIP = 4
LANE = 128
SUBLANE = 8
CHUNK = 128
VMEM_LIMIT_BYTES = 52 * 1024 * 1024
NORM_EPS = 1e-6
NEG = -1e30
PACK_W = 1024

ADAM_LR = 0.001
ADAM_B1 = 0.9
ADAM_B2 = 0.999
ADAM_EPS = 1e-08
ADAM_WD = 0.01
ADAM_STEP = 10

NN = ((1,), (0,))
NT = ((1,), (1,))
TN = ((0,), (0,))


def _params(*sem):
    return pltpu.CompilerParams(dimension_semantics=sem or None, vmem_limit_bytes=VMEM_LIMIT_BYTES)


def _tile(n, target, align=LANE):
    best = None
    for t in range(align, min(n, target) + 1, align):
        if n % t == 0:
            best = t
    return best or n


def _dot(a, b, dims):
    return lax.dot_general(a, b, (dims, ((), ())), preferred_element_type=F32)


def _place():
    return lax.axis_index("x"), lax.axis_index("y"), lax.axis_index("c")


def _mm(name, grid, lhs, rhs, dims, outs, epi=None, extras=(), summed=True, k_axis=None):
    n_a, n_b, n_e, n_o = len(lhs), len(rhs), len(extras), len(outs)
    n_acc = 1 if summed else n_b
    nk = grid[k_axis] if k_axis is not None else 1
    acc_shape = tuple(d for d in outs[0][2] if d is not None)

    def body(*refs):
        a_refs, b_refs = refs[:n_a], refs[n_a:n_a + n_b]
        e_refs = refs[n_a + n_b:n_a + n_b + n_e]
        o_refs = refs[n_a + n_b + n_e:n_a + n_b + n_e + n_o]
        acc_refs = refs[n_a + n_b + n_e + n_o:]

        def products():
            a_vals = [r[...].astype(BF16) for r in a_refs]
            ps = [_dot(a_vals[p % n_a], b_refs[p][...].astype(BF16), dims) for p in range(n_b)]
            return [functools.reduce(lambda u, w: u + w, ps)] if summed else ps

        def finish(accs):
            res = epi(accs, [e[...] for e in e_refs]) if epi is not None else accs
            for o_ref, r in zip(o_refs, res):
                o_ref[...] = r.astype(o_ref.dtype)

        if nk == 1:
            finish(products())
        else:
            k = pl.program_id(k_axis)

            @pl.when(k == 0)
            def _():
                for acc in acc_refs:
                    acc[...] = jnp.zeros_like(acc)

            for acc, p in zip(acc_refs, products()):
                acc[...] += p

            @pl.when(k == nk - 1)
            def _():
                finish([acc[...] for acc in acc_refs])

    sem = ["parallel"] * len(grid)
    if k_axis is not None:
        sem[k_axis] = "arbitrary"
    res = pl.pallas_call(
        body, name=name, grid=grid,
        in_specs=[pl.BlockSpec(bs, im) for _, bs, im in (*lhs, *rhs, *extras)],
        out_specs=[pl.BlockSpec(bs, im) for _, _, bs, im in outs],
        out_shape=[jax.ShapeDtypeStruct(s, d) for s, d, _, _ in outs],
        scratch_shapes=[pltpu.VMEM(acc_shape, F32)] * (n_acc if nk > 1 else 0),
        compiler_params=_params(*sem),
    )(*[a for a, _, _ in (*lhs, *rhs, *extras)])
    return res


def _rowwise(name, fn, tiled, whole, out_tiled, out_sums, tm):
    rows = tiled[0].shape[0]
    n_t, n_w, n_o, n_s = len(tiled), len(whole), len(out_tiled), len(out_sums)

    def body(*refs):
        t_refs, w_refs = refs[:n_t], refs[n_t:n_t + n_w]
        o_refs = refs[n_t + n_w:n_t + n_w + n_o]
        s_refs = refs[n_t + n_w + n_o:]
        outs, sums = fn([r[...] for r in t_refs], [r[...] for r in w_refs])
        for o_ref, val in zip(o_refs, outs):
            o_ref[...] = val.astype(o_ref.dtype)

        @pl.when(pl.program_id(0) == 0)
        def _():
            for s_ref in s_refs:
                s_ref[...] = jnp.zeros_like(s_ref)

        for s_ref, val in zip(s_refs, sums):
            s_ref[...] += val

    full = lambda a: pl.BlockSpec(a.shape, lambda i, nd=a.ndim: (0,) * nd)
    res = pl.pallas_call(
        body, name=name, grid=(rows // tm,),
        in_specs=[pl.BlockSpec((tm, a.shape[1]), lambda i: (i, 0)) for a in tiled] + [full(a) for a in whole],
        out_specs=[pl.BlockSpec((tm, n), lambda i: (i, 0)) for n, _ in out_tiled]
        + [pl.BlockSpec(s, lambda i, nd=len(s): (0,) * nd) for s in out_sums],
        out_shape=[jax.ShapeDtypeStruct((rows, n), d) for n, d in out_tiled]
        + [jax.ShapeDtypeStruct(s, F32) for s in out_sums],
        compiler_params=_params("arbitrary"),
    )(*tiled, *whole)
    return res


def _colsum(v):
    return jnp.sum(v, axis=0, keepdims=True)


def _rms_parts(x):
    inv = lax.rsqrt(jnp.mean(x * x, axis=-1, keepdims=True) + NORM_EPS)
    return inv, x * inv


def _rms_mod(name, x, g, sc, sh, tm):
    def fn(t, w):
        _, xhat = _rms_parts(t[0])
        return [xhat * w[0] * (1.0 + w[1]) + w[2]], []
    return _rowwise(name, fn, [x], [g, sc, sh], [(x.shape[1], BF16)], [], tm)[0]


def _rms_mod_bwd(name, x, dh, dres, g, sc, tm):
    d = x.shape[1]

    def fn(t, w):
        x_, dh_, dres_ = t
        g_, sc_ = w
        inv, xhat = _rms_parts(x_)
        dn = dh_ * (1.0 + sc_)
        dxhat = dn * g_
        dx = dres_ + inv * (dxhat - xhat * jnp.mean(dxhat * xhat, axis=-1, keepdims=True))
        return [dx], [_colsum(dh_), _colsum(dh_ * (xhat * g_)), _colsum(dn * xhat)]
    return _rowwise(name, fn, [x, dh, dres], [g, sc], [(d, F32)], [(1, d)] * 3, tm)


def _gate_bwd(name, dx, y, gate, tm):
    d = dx.shape[1]

    def fn(t, w):
        return [t[0] * w[0]], [_colsum(t[0] * t[1].astype(F32))]
    return _rowwise(name, fn, [dx, y], [gate], [(d, BF16)], [(1, d)], tm)


def _final_loss(name, x, target, g, tm):
    d = x.shape[1]

    def fn(t, w):
        inv, xhat = _rms_parts(t[0])
        err = xhat * w[0] - t[1]
        dout = err * (1.0 / d)
        dxhat = dout * w[0]
        dx = inv * (dxhat - xhat * jnp.mean(dxhat * xhat, axis=-1, keepdims=True))
        return [dx], [_colsum(err * err) * (0.5 / d), _colsum(dout * xhat)]
    return _rowwise(name, fn, [x, target], [g], [(d, F32)], [(1, d)] * 2, tm)


def _silu_rows(name, c):
    def fn(t, w):
        return [t[0] * jax.nn.sigmoid(t[0])], []
    return _rowwise(name, fn, [c], [], [(c.shape[1], F32)], [], c.shape[0])[0]


def _gelu(x):
    return 0.5 * x * (1.0 + lax.erf(x * (1.0 / math.sqrt(2.0))))


def _gelu_grad(x):
    cdf = 0.5 * (1.0 + lax.erf(x * (1.0 / math.sqrt(2.0))))
    return cdf + x * jnp.exp(-0.5 * x * x) * (1.0 / math.sqrt(2.0 * math.pi))


def _layer_norm_parts(v):
    mu = jnp.mean(v, axis=-1, keepdims=True)
    cen = v - mu
    rstd = lax.rsqrt(jnp.mean(cen * cen, axis=-1, keepdims=True) + NORM_EPS)
    return rstd, cen * rstd


def _causal(n):
    return lax.broadcasted_iota(jnp.int32, (n, n), 0) >= lax.broadcasted_iota(jnp.int32, (n, n), 1)


def _sgu_fwd(name, pre, w_s, b_s_t, ln_g, ln_b):
    t_len, d2 = pre.shape
    d = d2 // 2
    groups = w_s.shape[0]

    def body(pre_ref, w_ref, bs_ref, g_ref, b_ref, yy_ref):
        z = _gelu(pre_ref[...])
        u, v = z[:, :d], z[:, d:]
        _, vhat = _layer_norm_parts(v)
        vn = (vhat * g_ref[...] + b_ref[...]).astype(BF16)
        mask = _causal(CHUNK)
        bs = bs_ref[...]
        for g in range(groups):
            cols = slice(g * CHUNK, (g + 1) * CHUNK)
            w = jnp.where(mask, w_ref[g], 0.0).astype(BF16)
            sv = _dot(w, vn[:, cols], NN) + bs[:, g:g + 1]
            yy_ref[:, cols] = (u[:, cols] * sv).astype(BF16)

    full = lambda a: pl.BlockSpec(a.shape, lambda i, nd=a.ndim: (0,) * nd)
    return pl.pallas_call(
        body, name=name, grid=(t_len // CHUNK,),
        in_specs=[pl.BlockSpec((CHUNK, d2), lambda i: (i, 0)), full(w_s), full(b_s_t), full(ln_g), full(ln_b)],
        out_specs=pl.BlockSpec((CHUNK, d), lambda i: (i, 0)),
        out_shape=jax.ShapeDtypeStruct((t_len, d), BF16),
        compiler_params=_params("parallel"),
    )(pre, w_s, b_s_t, ln_g, ln_b)


def _sgu_bwd(name, pre, dyy, w_s, b_s_t, ln_g, ln_b):
    t_len, d2 = pre.shape
    d = d2 // 2
    groups = w_s.shape[0]

    def body(pre_ref, dyy_ref, w_ref, bs_ref, g_ref, b_ref, dpre_ref, dw_ref, dbs_ref, dg_ref, db_ref, dbin_ref, dvn_ref):
        @pl.when(pl.program_id(0) == 0)
        def _():
            for r in (dw_ref, dbs_ref, dg_ref, db_ref, dbin_ref):
                r[...] = jnp.zeros_like(r)

        pre_v = pre_ref[...]
        z = _gelu(pre_v)
        u, v = z[:, :d], z[:, d:]
        rstd, vhat = _layer_norm_parts(v)
        vn = (vhat * g_ref[...] + b_ref[...]).astype(BF16)
        mask = _causal(CHUNK)
        bs = bs_ref[...]
        lane = lax.broadcasted_iota(jnp.int32, (CHUNK, LANE), 1)
        dbs = jnp.zeros((CHUNK, LANE), F32)
        for g in range(groups):
            cols = slice(g * CHUNK, (g + 1) * CHUNK)
            w = jnp.where(mask, w_ref[g], 0.0).astype(BF16)
            sv = _dot(w, vn[:, cols], NN) + bs[:, g:g + 1]
            dyy_g = dyy_ref[:, cols]
            dpre_ref[:, cols] = (dyy_g * sv * _gelu_grad(pre_v[:, cols])).astype(BF16)
            dsv = dyy_g * u[:, cols]
            dbs = jnp.where(lane == g, jnp.sum(dsv, axis=1, keepdims=True), dbs)
            dsv_b = dsv.astype(BF16)
            dw_ref[g] += jnp.where(mask, _dot(dsv_b, vn[:, cols], NT), 0.0)
            dvn_ref[:, cols] = _dot(w, dsv_b, TN)
        dbs_ref[...] += dbs
        dvn = dvn_ref[...]
        dg_ref[...] += _colsum(dvn * vhat)
        db_ref[...] += _colsum(dvn)
        dvhat = dvn * g_ref[...]
        dv = rstd * (dvhat - jnp.mean(dvhat, axis=-1, keepdims=True)
                     - vhat * jnp.mean(dvhat * vhat, axis=-1, keepdims=True))
        dpre_ref[:, d:] = (dv * _gelu_grad(pre_v[:, d:])).astype(BF16)
        dbin_ref[...] += _colsum(dpre_ref[...].astype(F32))

    full = lambda a: pl.BlockSpec(a.shape, lambda i, nd=a.ndim: (0,) * nd)
    acc = lambda s: pl.BlockSpec(s, lambda i, nd=len(s): (0,) * nd)
    sums = [(groups, CHUNK, CHUNK), (CHUNK, LANE), (1, d), (1, d), (1, d2)]
    return pl.pallas_call(
        body, name=name, grid=(t_len // CHUNK,),
        in_specs=[pl.BlockSpec((CHUNK, d2), lambda i: (i, 0)), pl.BlockSpec((CHUNK, d), lambda i: (i, 0)),
                  full(w_s), full(b_s_t), full(ln_g), full(ln_b)],
        out_specs=[pl.BlockSpec((CHUNK, d2), lambda i: (i, 0))] + [acc(s) for s in sums],
        out_shape=[jax.ShapeDtypeStruct((t_len, d2), BF16)] + [jax.ShapeDtypeStruct(s, F32) for s in sums],
        scratch_shapes=[pltpu.VMEM((CHUNK, d), F32)],
        compiler_params=_params("arbitrary"),
    )(pre, dyy, w_s, b_s_t, ln_g, ln_b)


def _whole(rows, cols):
    return pl.BlockSpec((rows, cols), lambda i: (0, 0))


def _cum_matrix(reverse):
    r = lax.broadcasted_iota(jnp.int32, (CHUNK, CHUNK), 0)
    c = lax.broadcasted_iota(jnp.int32, (CHUNK, CHUNK), 1)
    return jnp.where((r <= c) if reverse else (r >= c), 1.0, 0.0).astype(F32)


def _forget_cumsum(name, logits, bias):
    t_len = logits.shape[0]

    def body(fl_ref, b_ref, f_ref, ft_ref):
        tri = _cum_matrix(False)

        def step(n, carry):
            off = pl.multiple_of(n * CHUNK, CHUNK)
            xv = fl_ref[pl.ds(off, CHUNK), :] + b_ref[...]
            log_f = jnp.minimum(xv, 0.0) - jnp.log1p(jnp.exp(-jnp.abs(xv)))
            cs = jnp.dot(tri, log_f, precision=lax.Precision.HIGHEST, preferred_element_type=F32) + carry
            f_ref[pl.ds(off, CHUNK), :] = cs
            ft_ref[:, pl.ds(off, CHUNK)] = cs.T
            return cs[CHUNK - 1:CHUNK, :]

        lax.fori_loop(0, t_len // CHUNK, step, jnp.zeros((1, LANE), F32))

    return pl.pallas_call(
        body, name=name, grid=(1,),
        in_specs=[_whole(t_len, LANE), _whole(1, LANE)],
        out_specs=[_whole(t_len, LANE), _whole(LANE, t_len)],
        out_shape=[jax.ShapeDtypeStruct((t_len, LANE), F32), jax.ShapeDtypeStruct((LANE, t_len), F32)],
        compiler_params=_params("arbitrary"),
    )(logits, bias)


def _forget_bwd(name, d_cum, logits, bias):
    t_len = logits.shape[0]
    n_chunks = t_len // CHUNK

    def body(dc_ref, fl_ref, b_ref, dl_ref, db_ref, run_ref):
        @pl.when(pl.program_id(0) == 0)
        def _():
            run_ref[...] = jnp.zeros_like(run_ref)
            db_ref[...] = jnp.zeros_like(db_ref)

        rc = jnp.dot(_cum_matrix(True), dc_ref[...], precision=lax.Precision.HIGHEST,
                     preferred_element_type=F32) + run_ref[0:1, :]
        dl = rc * jax.nn.sigmoid(-(fl_ref[...] + b_ref[...]))
        dl_ref[...] = dl
        db_ref[...] += _colsum(dl)
        run_ref[...] = jnp.broadcast_to(rc[0:1, :], run_ref.shape)

    back = pl.BlockSpec((CHUNK, LANE), lambda i: (n_chunks - 1 - i, 0))
    return pl.pallas_call(
        body, name=name, grid=(n_chunks,),
        in_specs=[back, back, _whole(1, LANE)],
        out_specs=[back, _whole(1, LANE)],
        out_shape=[jax.ShapeDtypeStruct((t_len, LANE), F32), jax.ShapeDtypeStruct((1, LANE), F32)],
        scratch_shapes=[pltpu.VMEM((SUBLANE, LANE), F32)],
        compiler_params=_params("arbitrary"),
    )(d_cum, logits, bias)


def _head_column(f_tile, head):
    lane = lax.broadcasted_iota(jnp.int32, f_tile.shape, 1)
    return jnp.sum(jnp.where(lane == head, f_tile, 0.0), axis=1, keepdims=True)


def _attn_fwd(name, qkv, f_cum, f_keys, heads, tq):
    t_len = qkv.shape[0]
    scale = 1.0 / math.sqrt(CHUNK)

    def body(q_ref, k_ref, v_ref, f_ref, fk_ref, o_ref, lse_ref):
        head, i = pl.program_id(0), pl.program_id(1)
        q = q_ref[...]
        fq = _head_column(f_ref[...], head)
        row = i * tq + lax.broadcasted_iota(jnp.int32, (tq, tq), 0)
        col = lax.broadcasted_iota(jnp.int32, (tq, tq), 1)

        def step(j, carry):
            m, l, acc = carry
            off = pl.multiple_of(j * tq, tq)
            s = _dot(q, k_ref[pl.ds(off, tq), :], NT) * scale + fq - fk_ref[:, pl.ds(off, tq)]
            s = jnp.where(col + off <= row, s, NEG)
            m_new = jnp.maximum(m, jnp.max(s, axis=1, keepdims=True))
            p = jnp.exp(s - m_new)
            alpha = jnp.exp(m - m_new)
            l = alpha * l + jnp.sum(p, axis=1, keepdims=True)
            acc = alpha * acc + _dot(p.astype(BF16), v_ref[pl.ds(off, tq), :], NN)
            return m_new, l, acc

        init = (jnp.full((tq, 1), NEG, F32), jnp.zeros((tq, 1), F32), jnp.zeros((tq, CHUNK), F32))
        m, l, acc = lax.fori_loop(0, i + 1, step, init)
        o_ref[...] = (acc / l).astype(BF16)
        lse_ref[...] = jnp.broadcast_to(m + jnp.log(l), (tq, LANE))

    return pl.pallas_call(
        body, name=name, grid=(heads, t_len // tq),
        in_specs=[pl.BlockSpec((tq, CHUNK), lambda h, i: (i, h)),
                  pl.BlockSpec((t_len, CHUNK), lambda h, i: (0, heads + h)),
                  pl.BlockSpec((t_len, CHUNK), lambda h, i: (0, 2 * heads + h)),
                  pl.BlockSpec((tq, LANE), lambda h, i: (i, 0)),
                  pl.BlockSpec((None, 1, t_len), lambda h, i: (h, 0, 0))],
        out_specs=[pl.BlockSpec((tq, CHUNK), lambda h, i: (i, h)),
                   pl.BlockSpec((None, tq, LANE), lambda h, i: (h, i, 0))],
        out_shape=[jax.ShapeDtypeStruct((t_len, heads * CHUNK), BF16),
                   jax.ShapeDtypeStruct((heads, t_len, LANE), F32)],
        compiler_params=_params("parallel", "parallel"),
    )(qkv, qkv, qkv, f_cum, f_keys)


def _attn_delta(name, d_o, o, heads, tq):
    t_len = o.shape[0]

    def body(do_ref, o_ref, dl_ref):
        dl = jnp.sum(do_ref[...].astype(F32) * o_ref[...].astype(F32), axis=1, keepdims=True)
        dl_ref[...] = jnp.broadcast_to(dl, (tq, LANE))

    return pl.pallas_call(
        body, name=name, grid=(heads, t_len // tq),
        in_specs=[pl.BlockSpec((tq, CHUNK), lambda h, i: (i, h))] * 2,
        out_specs=pl.BlockSpec((None, tq, LANE), lambda h, i: (h, i, 0)),
        out_shape=jax.ShapeDtypeStruct((heads, t_len, LANE), F32),
        compiler_params=_params("parallel", "parallel"),
    )(d_o, o)


def _attn_bwd(name, qkv, d_o, f_cum, f_keys, lse, delta, heads, tq):
    t_len = qkv.shape[0]
    d = heads * CHUNK
    n_q = t_len // tq
    scale = 1.0 / math.sqrt(CHUNK)

    def body(q_ref, k_ref, v_ref, do_ref, f_ref, fk_ref, lse_ref, dl_ref, dq_ref, dk_ref, dv_ref, dfq_ref, dfk_ref,
             dq_acc):
        head, j = pl.program_id(0), pl.program_id(1)

        @pl.when(j == 0)
        def _():
            dq_acc[...] = jnp.zeros_like(dq_acc)

        @pl.when((j == 0) & (head == 0))
        def _():
            dfq_ref[...] = jnp.zeros_like(dfq_ref)

        k, v, fk = k_ref[...], v_ref[...], fk_ref[...]
        col = j * tq + lax.broadcasted_iota(jnp.int32, (tq, tq), 1)
        row = lax.broadcasted_iota(jnp.int32, (tq, tq), 0)
        lane = lax.broadcasted_iota(jnp.int32, (tq, LANE), 1)

        def step(i, carry):
            dk, dv, dfk = carry
            off = pl.multiple_of(i * tq, tq)
            rows = pl.ds(off, tq)
            q, do = q_ref[rows, :], do_ref[rows, :]
            fq = _head_column(f_ref[rows, :], head)
            s = _dot(q, k, NT) * scale + fq - fk
            s = jnp.where(col <= row + off, s, NEG)
            p = jnp.exp(s - lse_ref[rows, :][:, 0:1])
            ds = p * (_dot(do, v, NT) - dl_ref[rows, :][:, 0:1])
            ds_b = ds.astype(BF16)
            dq_acc[rows, :] += _dot(ds_b, k, NN)
            dfq_ref[rows, :] += jnp.where(lane == head, jnp.sum(ds, axis=1, keepdims=True), 0.0)
            return (dk + _dot(ds_b, q, TN), dv + _dot(p.astype(BF16), do, TN), dfk + _colsum(ds))

        zero = jnp.zeros((tq, CHUNK), F32)
        dk, dv, dfk = lax.fori_loop(j, n_q, step, (zero, zero, jnp.zeros((1, tq), F32)))
        dk_ref[...] = (dk * scale).astype(BF16)
        dv_ref[...] = dv.astype(BF16)
        dfk_ref[...] = dfk

        @pl.when(j == n_q - 1)
        def _():
            dq_ref[...] = (dq_acc[...] * scale).astype(BF16)

    whole_head = lambda c0: pl.BlockSpec((t_len, CHUNK), lambda h, j: (0, c0 + h))
    per_head = pl.BlockSpec((None, t_len, LANE), lambda h, j: (h, 0, 0))
    key_block = lambda c0: pl.BlockSpec((tq, CHUNK), lambda h, j: (j, c0 + h))
    return pl.pallas_call(
        body, name=name, grid=(heads, n_q),
        in_specs=[whole_head(0), key_block(heads), key_block(2 * heads), whole_head(0),
                  pl.BlockSpec((t_len, LANE), lambda h, j: (0, 0)),
                  pl.BlockSpec((None, 1, tq), lambda h, j: (h, 0, j)),
                  per_head, per_head],
        out_specs=[whole_head(0), key_block(0), key_block(0),
                   pl.BlockSpec((t_len, LANE), lambda h, j: (0, 0)),
                   pl.BlockSpec((None, 1, tq), lambda h, j: (h, 0, j))],
        out_shape=[jax.ShapeDtypeStruct((t_len, d), BF16), jax.ShapeDtypeStruct((t_len, d), BF16),
                   jax.ShapeDtypeStruct((t_len, d), BF16), jax.ShapeDtypeStruct((t_len, LANE), F32),
                   jax.ShapeDtypeStruct((heads, 1, t_len), F32)],
        scratch_shapes=[pltpu.VMEM((t_len, CHUNK), F32)],
        compiler_params=_params("arbitrary", "arbitrary"),
    )(qkv, qkv, qkv, d_o, f_cum, f_keys, lse, delta)


def _ada_grad(name, c_act_t, dmod, tm):
    d = c_act_t.shape[0]
    n_layer, n_b, n_col = dmod.shape

    def body(c_ref, dm_ref, o_ref):
        c, dm = c_ref[...], dm_ref[...]
        acc = c[:, 0:1] * dm[0:1, :]
        for b in range(1, N_DEV):
            acc = acc + c[:, b:b + 1] * dm[b:b + 1, :]
        o_ref[...] = acc

    return pl.pallas_call(
        body, name=name, grid=(n_layer, d // tm),
        in_specs=[pl.BlockSpec((tm, LANE), lambda l, i: (i, 0)),
                  pl.BlockSpec((None, n_b, n_col), lambda l, i: (l, 0, 0))],
        out_specs=pl.BlockSpec((None, tm, n_col), lambda l, i: (l, i, 0)),
        out_shape=jax.ShapeDtypeStruct((n_layer, d, n_col), F32),
        compiler_params=_params("parallel", "parallel"),
    )(c_act_t, dmod)


def _adamw(name, w, parts, m, v):
    n_layer, rows, cols = w.shape
    per_layer = isinstance(parts, (list, tuple))
    parts = list(parts) if per_layer else [parts]
    tr = _tile(rows, max(2 * SUBLANE, (1 << 19) // cols), 2 * SUBLANE)

    def body(*refs):
        w_ref, m_ref, v_ref = refs[:3]
        p_refs = refs[3:3 + len(parts)]
        g_ref, d_ref, mo_ref, vo_ref = refs[3 + len(parts):]
        layer = pl.program_id(0)
        g = None
        for n, p_ref in enumerate(p_refs):
            g_n = p_ref[0].astype(F32)
            for p in range(1, p_ref.shape[0]):
                g_n = g_n + p_ref[p].astype(F32)
            g = g_n if g is None else jnp.where(layer == n, g_n, g)
        m_new = ADAM_B1 * m_ref[...] + (1.0 - ADAM_B1) * g
        v_new = ADAM_B2 * v_ref[...] + (1.0 - ADAM_B2) * jnp.square(g)
        m_hat = m_new / (1.0 - ADAM_B1 ** ADAM_STEP)
        v_hat = v_new / (1.0 - ADAM_B2 ** ADAM_STEP)
        g_ref[...] = g
        d_ref[...] = -ADAM_LR * (m_hat / (jnp.sqrt(v_hat) + ADAM_EPS) + ADAM_WD * w_ref[...])
        mo_ref[...] = m_new
        vo_ref[...] = v_new

    blk = pl.BlockSpec((None, tr, cols), lambda l, i: (l, i, 0))
    if per_layer:
        p_specs = [pl.BlockSpec((p.shape[0], tr, cols), lambda l, i, n=n: (0, jnp.where(l == n, i, 0), 0))
                   for n, p in enumerate(parts)]
    else:
        p_specs = [pl.BlockSpec((parts[0].shape[0], None, tr, cols), lambda l, i: (0, l, i, 0))]
    return pl.pallas_call(
        body, name=name, grid=(n_layer, rows // tr),
        in_specs=[blk] * 3 + p_specs,
        out_specs=[blk] * 4,
        out_shape=[jax.ShapeDtypeStruct(w.shape, F32)] * 4,
        compiler_params=_params("parallel", "parallel"),
    )(w, m, v, *parts)


def _dev_index(p):
    return 4 * p[0] + 2 * p[1] + p[2]


def _other_chips(mx, my):
    return [(1 - mx, my), (mx, 1 - my), (1 - mx, 1 - my)]


def _all_gather_small(name, x):
    rows, cols = x.shape

    def body(x_ref, o_ref, send_sems, recv_sems):
        mx, my, mc = _place()
        me = _dev_index((mx, my, mc))
        o_ref[me] = x_ref[...]

        def copy(dist, slot, peer):
            return pltpu.make_async_remote_copy(
                src_ref=x_ref, dst_ref=o_ref.at[slot], send_sem=send_sems.at[dist - 1], recv_sem=recv_sems.at[dist - 1],
                device_id=(peer // 4, (peer // 2) % 2, peer % 2), device_id_type=MESH)

        sends = [copy(dist, me, (me + dist) % N_DEV) for dist in range(1, N_DEV)]
        for cp in sends:
            cp.start()
        for dist in range(1, N_DEV):
            src = (me + N_DEV - dist) % N_DEV
            copy(dist, src, src).wait_recv()
        for cp in sends:
            cp.wait_send()

    return pl.pallas_call(
        body, name=name,
        out_shape=jax.ShapeDtypeStruct((N_DEV, rows, cols), x.dtype),
        in_specs=[pl.BlockSpec(memory_space=pltpu.VMEM)],
        out_specs=pl.BlockSpec(memory_space=pltpu.VMEM),
        scratch_shapes=[pltpu.SemaphoreType.DMA((N_DEV - 1,)), pltpu.SemaphoreType.DMA((N_DEV - 1,))],
        compiler_params=_params(),
    )(x)


def _split_copy_call(name, body, n_in, sems, through, extra_out=(), first=True):
    sem_shapes = [pltpu.SemaphoreType.DMA((k,)) for k in sems]
    if first:
        return pl.pallas_call(
            body, name=name,
            out_shape=(*sem_shapes, *[pltpu.HBM(t.shape, t.dtype) for t in through], *extra_out),
            in_specs=[HBM] * n_in,
            out_specs=(*[SEM] * len(sems), *[HBM] * len(through), *[VMEM_SPEC] * len(extra_out)),
            input_output_aliases={i: len(sems) + i for i in range(len(through))},
            compiler_params=pltpu.CompilerParams(has_side_effects=EFFECT),
        )
    return pl.pallas_call(
        body, name=name,
        out_shape=tuple(pltpu.HBM(t.shape, t.dtype) for t in through),
        in_specs=[HBM] * len(through) + [SEM] * len(sems) + [ANY],
        out_specs=tuple([HBM] * len(through)),
        input_output_aliases={i: i for i in range(len(through))},
        compiler_params=pltpu.CompilerParams(has_side_effects=EFFECT),
    )


def _gather_start(name, shards):
    n = len(shards)
    lands = [lax.empty((N_DEV, *s.shape), s.dtype) for s in shards]

    def body(*refs):
        x_refs, land_refs = refs[:n], refs[n:2 * n]
        send_sems, recv_sems = refs[2 * n], refs[2 * n + 1]
        token = refs[-1]
        mx, my, mc = _place()
        for a in range(n):
            for j, chip in enumerate(_other_chips(mx, my)):
                pltpu.make_async_remote_copy(
                    src_ref=x_refs[a], dst_ref=land_refs[a].at[_dev_index((mx, my, mc))],
                    send_sem=send_sems.at[3 * a + j], recv_sem=recv_sems.at[3 * a + j],
                    device_id=(*chip, mc), device_id_type=MESH).start()
        token[...] = jnp.zeros_like(token)

    operands = [pltpu.with_memory_space_constraint(t, pltpu.HBM) for t in (*shards, *lands)]
    res = _split_copy_call(name, body, 2 * n, (3 * n, 3 * n), operands,
                           extra_out=(jax.ShapeDtypeStruct((SUBLANE, LANE), F32),))(*operands)
    return res[:2], res[2:2 + n], res[2 + n:2 + 2 * n], res[-1]


def _gather_wait(name, sems, shards, lands, after):
    n = len(shards)

    def body(*refs):
        x_refs, land_refs = refs[:n], refs[n:2 * n]
        send_sems, recv_sems = refs[2 * n], refs[2 * n + 1]
        mx, my, mc = _place()
        for a in range(n):
            for j, chip in enumerate(_other_chips(mx, my)):
                copy = pltpu.make_async_remote_copy(
                    src_ref=x_refs[a], dst_ref=land_refs[a].at[_dev_index((*chip, mc))],
                    send_sem=send_sems.at[3 * a + j], recv_sem=recv_sems.at[3 * a + j],
                    device_id=(*chip, mc), device_id_type=MESH)
                copy.wait_send()
                copy.wait_recv()

    res = _split_copy_call(name, body, 2 * n, (3 * n, 3 * n), [*shards, *lands], first=False)(
        *shards, *lands, *sems, after)
    return res[:n], res[n:]


def _gather_finish(name, shards, lands):
    n = len(shards)

    def body(*refs):
        x_refs, land_refs, o_refs = refs[:n], refs[n:2 * n], refs[2 * n:3 * n]
        send_sems, recv_sems, local_sems = refs[3 * n:]
        mx, my, mc = _place()
        sibling = (mx, my, 1 - mc)
        blocks = [(mx, my)] + _other_chips(mx, my)

        def copy(a, k, core, src=None):
            dst = o_refs[a].at[_dev_index((*blocks[k], core))]
            return pltpu.make_async_remote_copy(
                src_ref=dst if src is None else src, dst_ref=dst,
                send_sem=send_sems.at[N_CHIP * a + k], recv_sem=recv_sems.at[N_CHIP * a + k],
                device_id=sibling, device_id_type=MESH)

        mine = [pltpu.make_async_copy(x_refs[a], o_refs[a].at[_dev_index((mx, my, mc))], local_sems.at[a])
                for a in range(n)]
        sends = [copy(a, k, mc, src=x_refs[a] if k == 0 else None) for a in range(n) for k in range(N_CHIP)]
        for cp in mine + sends:
            cp.start()
        for a in range(n):
            for k in range(N_CHIP):
                copy(a, k, 1 - mc).wait_recv()
        for cp in sends:
            cp.wait_send()
        for cp in mine:
            cp.wait()

    return pl.pallas_call(
        body, name=name,
        out_shape=[jax.ShapeDtypeStruct(t.shape, t.dtype) for t in lands],
        in_specs=[ANY] * (2 * n), out_specs=[ANY] * n,
        input_output_aliases={n + a: a for a in range(n)},
        scratch_shapes=[pltpu.SemaphoreType.DMA((N_CHIP * n,)), pltpu.SemaphoreType.DMA((N_CHIP * n,)),
                        pltpu.SemaphoreType.DMA((n,))],
        compiler_params=_params(),
    )(*shards, *lands)


def _exchange_sibling(name, grads):
    n = len(grads)

    def body(*refs):
        g_refs, r_refs = refs[:n], refs[n:2 * n]
        send_sems, recv_sems = refs[2 * n:]
        mx, my, mc = _place()

        def copy(a, q):
            return pltpu.make_async_remote_copy(
                src_ref=g_refs[a].at[q, 1 - mc], dst_ref=r_refs[a].at[q],
                send_sem=send_sems.at[N_CHIP * a + q], recv_sem=recv_sems.at[N_CHIP * a + q],
                device_id=(mx, my, 1 - mc), device_id_type=MESH)

        copies = [copy(a, q) for a in range(n) for q in range(N_CHIP)]
        for cp in copies:
            cp.start()
        for cp in copies:
            cp.wait_recv()
        for cp in copies:
            cp.wait_send()

    return pl.pallas_call(
        body, name=name,
        out_shape=[jax.ShapeDtypeStruct((N_CHIP, *g.shape[2:]), g.dtype) for g in grads],
        in_specs=[ANY] * n, out_specs=[ANY] * n,
        scratch_shapes=[pltpu.SemaphoreType.DMA((N_CHIP * n,)), pltpu.SemaphoreType.DMA((N_CHIP * n,))],
        compiler_params=_params(),
    )(*grads)


def _add_sibling(name, grad, recv, place):
    _, _, rows, cols = grad.shape
    tr = _tile(rows, max(2 * SUBLANE, (1 << 19) // cols), 2 * SUBLANE)

    def body(p_ref, g_ref, r_ref, o_ref, land_ref):
        total = (g_ref[...].astype(F32) + r_ref[...].astype(F32)).astype(o_ref.dtype)
        o_ref[...] = total

        @pl.when(pl.program_id(1) == p_ref[1])
        def _():
            land_ref[...] = total

    out = jax.ShapeDtypeStruct(recv.shape, recv.dtype)
    return pl.pallas_call(
        body, name=name,
        grid_spec=pltpu.PrefetchScalarGridSpec(
            num_scalar_prefetch=1, grid=(rows // tr, N_CHIP),
            in_specs=[pl.BlockSpec((None, None, tr, cols), lambda i, q, p: (q, p[0], i, 0)),
                      pl.BlockSpec((None, tr, cols), lambda i, q, p: (q, i, 0))],
            out_specs=[pl.BlockSpec((None, tr, cols), lambda i, q, p: (q, i, 0)),
                       pl.BlockSpec((None, tr, cols), lambda i, q, p: (p[1], i, 0))]),
        out_shape=[out, out],
        compiler_params=_params("parallel", "arbitrary"),
    )(place, grad, recv)


def _scatter_start(name, parts, lands):
    n = len(parts)

    def body(*refs):
        p_refs, land_refs = refs[:n], refs[n:2 * n]
        send_sems, recv_sems = refs[2 * n], refs[2 * n + 1]
        token = refs[-1]
        mx, my, mc = _place()
        for a in range(n):
            for j, chip in enumerate(_other_chips(mx, my)):
                pltpu.make_async_remote_copy(
                    src_ref=p_refs[a].at[2 * chip[0] + chip[1]], dst_ref=land_refs[a].at[2 * mx + my],
                    send_sem=send_sems.at[3 * a + j], recv_sem=recv_sems.at[3 * a + j],
                    device_id=(*chip, mc), device_id_type=MESH).start()
        token[...] = jnp.zeros_like(token)

    operands = [pltpu.with_memory_space_constraint(t, pltpu.HBM) for t in (*parts, *lands)]
    res = _split_copy_call(name, body, 2 * n, (3 * n, 3 * n), operands,
                           extra_out=(jax.ShapeDtypeStruct((SUBLANE, LANE), F32),))(*operands)
    return res[:2], res[2:2 + n], res[2 + n:2 + 2 * n], res[-1]


def _scatter_wait(name, sems, parts, lands, after):
    n = len(parts)

    def body(*refs):
        p_refs, land_refs = refs[:n], refs[n:2 * n]
        send_sems, recv_sems = refs[2 * n], refs[2 * n + 1]
        mx, my, mc = _place()
        for a in range(n):
            for j, chip in enumerate(_other_chips(mx, my)):
                copy = pltpu.make_async_remote_copy(
                    src_ref=p_refs[a].at[2 * chip[0] + chip[1]], dst_ref=land_refs[a].at[2 * chip[0] + chip[1]],
                    send_sem=send_sems.at[3 * a + j], recv_sem=recv_sems.at[3 * a + j],
                    device_id=(*chip, mc), device_id_type=MESH)
                copy.wait_send()
                copy.wait_recv()

    res = _split_copy_call(name, body, 2 * n, (3 * n, 3 * n), [*parts, *lands], first=False)(
        *parts, *lands, *sems, after)
    return res[n:]


def _reduce_scatter_start(tag, grads, place):
    wide = [g.reshape(N_CHIP, 2, *g.shape[1:]) for g in grads]
    from_sibling = _exchange_sibling(f"{tag}_reduce_sibling", wide)
    added = [_add_sibling(f"{tag}_add_sibling_{n}", g, r, place) for n, (g, r) in enumerate(zip(wide, from_sibling))]
    return _scatter_start(f"{tag}_reduce_chips_start", [p for p, _ in added], [l for _, l in added])


def _row(v):
    return v.reshape(1, -1)


def _out_proj(name, act, w_out, x, gate, tm, tn):
    t_len, d = x.shape
    k = act.shape[1]
    return _mm(name, (t_len // tm, d // tn),
               [(act, (tm, k), lambda i, j: (i, 0))], [(w_out, (k, tn), lambda i, j: (0, j))], NN,
               [((t_len, d), BF16, (tm, tn), lambda i, j: (i, j)), ((t_len, d), F32, (tm, tn), lambda i, j: (i, j))],
               epi=lambda accs, e: [accs[0], e[0] + e[1] * accs[0]],
               extras=[(x, (tm, tn), lambda i, j: (i, j)), (gate, (1, tn), lambda i, j: (0, j))])


def _proj_bwd(name, dy, w_out, dtype, tm, tn):
    t_len, d = dy.shape
    k = w_out.shape[0]
    return _mm(name, (t_len // tm, k // tn),
               [(dy, (tm, d), lambda i, j: (i, 0))], [(w_out, (tn, d), lambda i, j: (j, 0))], NT,
               [((t_len, k), dtype, (tm, tn), lambda i, j: (i, j))])[0]


def _weight_grad(name, act, dy, tm, tn):
    t_len, k = act.shape
    n = dy.shape[1]
    return _mm(name, (k // tm, n // tn),
               [(act, (t_len, tm), lambda i, j: (0, i))], [(dy, (t_len, tn), lambda i, j: (0, j))], TN,
               [((k, n), BF16, (tm, tn), lambda i, j: (i, j))])[0]


def _ffn_fwd(tag, x1, mod, g_norm, w_gate, w_up, w_down, tm):
    t_len, d = x1.shape
    fs = w_gate.shape[2]
    sh2, sc2, g2 = mod[3], mod[4], mod[5]
    h2 = _rms_mod(f"{tag}_ffn_norm", x1, g_norm, sc2, sh2, tm)
    hidden = ((N_DEV, t_len, fs), BF16, (None, tm, fs), lambda j, i: (j, i, 0))

    def swiglu(accs, _):
        a, b = accs
        return [a, b, a * jax.nn.sigmoid(a) * b]

    a, b, s = _mm(f"{tag}_ffn_up", (N_DEV, t_len // tm),
                  [(h2, (tm, d), lambda j, i: (i, 0))],
                  [(w_gate, (None, d, fs), lambda j, i: (j, 0, 0)), (w_up, (None, d, fs), lambda j, i: (j, 0, 0))],
                  NN, [hidden] * 3, epi=swiglu, summed=False)
    f, x2 = _mm(f"{tag}_ffn_down", (t_len // tm, 1, N_DEV),
                [(s, (None, tm, fs), lambda i, j, k: (k, i, 0))], [(w_down, (None, fs, d), lambda i, j, k: (k, 0, 0))],
                NN,
                [((t_len, d), BF16, (tm, d), lambda i, j, k: (i, 0)), ((t_len, d), F32, (tm, d), lambda i, j, k: (i, 0))],
                epi=lambda accs, e: [accs[0], e[0] + e[1] * accs[0]],
                extras=[(x1, (tm, d), lambda i, j, k: (i, 0)), (g2, (1, d), lambda i, j, k: (0, 0))], k_axis=2)
    return x2, (h2, a, b, s, f)


def _ffn_bwd(tag, dx2, x1, saved, mod, g_norm, w_gate, w_up, w_down, tm):
    t_len, d = x1.shape
    fs = w_gate.shape[2]
    h2, a, b, s, f = saved
    sc2, g2 = mod[4], mod[5]
    df, dg2 = _gate_bwd(f"{tag}_ffn_gate_bwd", dx2, f, g2, tm)
    hidden = ((N_DEV, t_len, fs), BF16, (None, tm, fs), lambda j, i: (j, i, 0))
    hid_in = lambda arr: (arr, (None, tm, fs), lambda j, i: (j, i, 0))

    def swiglu_bwd(accs, e):
        a_, b_ = e[0].astype(F32), e[1].astype(F32)
        sig = jax.nn.sigmoid(a_)
        return [accs[0] * b_ * sig * (1.0 + a_ * (1.0 - sig)), accs[0] * a_ * sig]

    da, db = _mm(f"{tag}_ffn_down_bwd", (N_DEV, t_len // tm),
                 [(df, (tm, d), lambda j, i: (i, 0))], [(w_down, (None, fs, d), lambda j, i: (j, 0, 0))], NT,
                 [hidden] * 2, epi=swiglu_bwd, extras=[hid_in(a), hid_in(b)])
    tn = _tile(d, 512)
    d_wd = _mm(f"{tag}_ffn_wdown_grad", (N_DEV, d // tn),
               [(s, (None, t_len, fs), lambda j, i: (j, 0, 0))], [(df, (t_len, tn), lambda j, i: (0, i))], TN,
               [((N_DEV, fs, d), BF16, (None, fs, tn), lambda j, i: (j, 0, i))])[0]
    w_grad = ((N_DEV, d, fs), BF16, (None, tn, fs), lambda j, i: (j, i, 0))
    d_wg, d_wu = _mm(f"{tag}_ffn_wup_grad", (N_DEV, d // tn),
                     [(h2, (t_len, tn), lambda j, i: (0, i))],
                     [(da, (None, t_len, fs), lambda j, i: (j, 0, 0)), (db, (None, t_len, fs), lambda j, i: (j, 0, 0))],
                     TN, [w_grad] * 2, summed=False)
    dh2 = _mm(f"{tag}_ffn_up_bwd", (t_len // tm, 1, N_DEV),
              [(da, (None, tm, fs), lambda i, j, k: (k, i, 0)), (db, (None, tm, fs), lambda i, j, k: (k, i, 0))],
              [(w_gate, (None, d, fs), lambda i, j, k: (k, 0, 0)), (w_up, (None, d, fs), lambda i, j, k: (k, 0, 0))],
              NT, [((t_len, d), F32, (tm, d), lambda i, j, k: (i, 0))], k_axis=2)[0]
    dx1, dsh2, dsc2, dgn = _rms_mod_bwd(f"{tag}_ffn_norm_bwd", x1, dh2, dx2, g_norm, sc2, tm // 2)
    return dx1, (d_wg, d_wu, d_wd), (dsh2, dsc2, dg2, dgn)


def kernel(x, c, ada_w, ada_b, norm_mix_g, norm_ffn_g, a_w_in, a_b_in, a_ln_g, a_ln_b, a_w_s, a_b_s, a_w_out, b_w_in, b_b_f, b_w_out, ffn_w_gate, ffn_w_up, ffn_w_down, final_g, loss_target, m_ada_w, m_ada_b, m_norm_mix_g, m_norm_ffn_g, m_a_w_in, m_a_b_in, m_a_ln_g, m_a_ln_b, m_a_w_s, m_a_b_s, m_a_w_out, m_b_w_in, m_b_b_f, m_b_w_out, m_ffn_w_gate, m_ffn_w_up, m_ffn_w_down, m_final_g, v_ada_w, v_ada_b, v_norm_mix_g, v_norm_ffn_g, v_a_w_in, v_a_b_in, v_a_ln_g, v_a_ln_b, v_a_w_s, v_a_b_s, v_a_w_out, v_b_w_in, v_b_b_f, v_b_w_out, v_ffn_w_gate, v_ffn_w_up, v_ffn_w_down, v_final_g):
    t_len, d = x.shape[1], x.shape[2]
    heads = d // CHUNK
    groups = a_w_s.shape[1]
    d_mod = 6 * d
    mod_cols = ada_w.shape[2]
    tm = _tile(t_len, 512)
    tn = _tile(d, 512)
    tq = _tile(t_len, 256)
    mx, my, mc = _place()
    me = _dev_index((mx, my, mc))
    x0, target = x[0], loss_target[0]

    groups_w = {"sgu": [a_w_in[0], a_w_out[0]], "ffn0": [ffn_w_gate[0], ffn_w_up[0], ffn_w_down[0]],
                "fox": [b_w_in[0], b_w_out[0]], "ffn1": [ffn_w_gate[1], ffn_w_up[1], ffn_w_down[1]]}
    started, token = {}, jnp.zeros((SUBLANE, LANE), F32)
    for key, group in groups_w.items():
        started[key] = _gather_start(f"gather_{key}_start", [(s + token[0, 0]).astype(BF16) for s in group])
        token = started[key][3]

    def gathered(key, after):
        sems, shards, lands, _ = started[key]
        shards, lands = _gather_wait(f"gather_{key}_wait", sems, shards, lands, after)
        return _gather_finish(f"gather_{key}_finish", shards, lands)

    c_all = _all_gather_small("gather_c", jnp.pad(c + token[0, 0], ((0, SUBLANE - 1), (0, 0))))[:, 0, :]
    c_act = _silu_rows("silu_c", jnp.pad(c_all, ((0, 2 * SUBLANE - N_DEV), (0, 0))))
    mod_part = _mm("mod_matmul", (2, mod_cols // _tile(mod_cols, 512)),
                   [(c_act, (2 * SUBLANE, d), lambda l, j: (0, 0))],
                   [(ada_w, (None, d, _tile(mod_cols, 512)), lambda l, j: (l, 0, j))], NN,
                   [((2, 2 * SUBLANE, mod_cols), F32, (None, 2 * SUBLANE, _tile(mod_cols, 512)), lambda l, j: (l, 0, j))])[0]
    mod_all = _all_gather_small("gather_mod", mod_part[:, :N_DEV, :].reshape(2 * N_DEV, mod_cols))
    mod_mine = lax.dynamic_index_in_dim(mod_all.reshape(N_DEV, 2, N_DEV, mod_cols), me, axis=2, keepdims=False)
    mod = mod_mine.transpose(1, 0, 2).reshape(2, d_mod) + ada_b
    mods = [[_row(mod[l, k * d:(k + 1) * d]) for k in range(6)] for l in range(2)]

    g_mix0, g_ffn0 = _row(norm_mix_g[0]), _row(norm_ffn_g[0])
    w_a_in, w_a_out = gathered("sgu", mod)
    w_a_in = w_a_in.transpose(1, 0, 2).reshape(d, 2 * d)
    w_a_out = w_a_out.reshape(d, d)
    na = a_w_in.shape[2]
    h_a = _rms_mod("l0_mix_norm", x0, g_mix0, mods[0][1], mods[0][0], tm)
    pre = _mm("l0_sgu_in", (t_len // tm, 2 * d // tn),
              [(h_a, (tm, d), lambda i, j: (i, 0))], [(w_a_in, (d, tn), lambda i, j: (0, j))], NN,
              [((t_len, 2 * d), F32, (tm, tn), lambda i, j: (i, j))],
              epi=lambda accs, e: [accs[0] + e[0]], extras=[(a_b_in, (1, tn), lambda i, j: (0, j))])[0]
    w_s, b_s_t = a_w_s[0], jnp.pad(a_b_s[0].T, ((0, 0), (0, LANE - groups)))
    ln_g, ln_b = a_ln_g, a_ln_b
    yy = _sgu_fwd("l0_sgu_mix", pre, w_s, b_s_t, ln_g, ln_b)
    y_a, x1 = _out_proj("l0_sgu_out", yy, w_a_out, x0, mods[0][2], tm, tn)
    w_ffn0 = gathered("ffn0", x1)
    x2, ffn0_saved = _ffn_fwd("l0", x1, mods[0], g_ffn0, *w_ffn0, tm)

    g_mix1, g_ffn1 = _row(norm_mix_g[1]), _row(norm_ffn_g[1])
    h_b = _rms_mod("l1_mix_norm", x2, g_mix1, mods[1][1], mods[1][0], tm)
    w_b_in, w_b_out = gathered("fox", h_b)
    w_b_out = w_b_out.reshape(d, d)
    w_b_full = w_b_in.transpose(1, 0, 2).reshape(d, 3 * d + heads)
    w_qkv = w_b_full[:, :3 * d]
    w_f = jnp.pad(w_b_full[:, 3 * d:], ((0, 0), (0, LANE - heads)))
    qkv = _mm("l1_qkv", (t_len // tm, 3 * d // tn),
              [(h_b, (tm, d), lambda i, j: (i, 0))], [(w_qkv, (d, tn), lambda i, j: (0, j))], NN,
              [((t_len, 3 * d), BF16, (tm, tn), lambda i, j: (i, j))])[0]
    f_logit = _mm("l1_forget_logit", (t_len // tm, 1),
                  [(h_b, (tm, d), lambda i, j: (i, 0))], [(w_f, (d, LANE), lambda i, j: (0, 0))], NN,
                  [((t_len, LANE), F32, (tm, LANE), lambda i, j: (i, 0))])[0]
    b_f = jnp.pad(b_b_f, ((0, 0), (0, LANE - heads)))
    f_cum, f_cum_t = _forget_cumsum("l1_forget_cumsum", f_logit, b_f)
    f_keys = f_cum_t[:heads].reshape(heads, 1, t_len)
    o, lse = _attn_fwd("l1_attn", qkv, f_cum, f_keys, heads, tq)
    y_b, x3 = _out_proj("l1_attn_out", o, w_b_out, x2, mods[1][2], tm, tn)
    w_ffn1 = gathered("ffn1", x3)
    x4, ffn1_saved = _ffn_fwd("l1", x3, mods[1], g_ffn1, *w_ffn1, tm)

    dx4, loss_cols, d_final_g = _final_loss("loss_head", x4, target, _row(final_g), tm // 2)
    loss = lax.psum(jnp.sum(loss_cols), ("x", "y", "c"))

    place = jnp.stack([mc, 2 * mx + my]).astype(jnp.int32)
    dx3, ffn1_grads, (dsh2_1, dsc2_1, dg2_1, dgf_1) = _ffn_bwd("l1", dx4, x3, ffn1_saved, mods[1], g_ffn1, *w_ffn1, tm)
    rs_ffn1 = _reduce_scatter_start("ffn1", ffn1_grads, place)
    dy_b, dg1_1 = _gate_bwd("l1_attn_gate_bwd", dx3, y_b, mods[1][2] + rs_ffn1[3][0, 0], tm)
    d_o = _proj_bwd("l1_attn_out_bwd", dy_b, w_b_out, BF16, tm, tn)
    d_w_b_out = _weight_grad("l1_attn_wout_grad", o, dy_b, tn, tn)
    delta = _attn_delta("l1_attn_delta", d_o, o, heads, tq)
    dq, dk, dv, dfq, dfk = _attn_bwd("l1_attn_bwd", qkv, d_o, f_cum, f_keys, lse, delta, heads, tq)
    d_cum = dfq - jnp.pad(dfk.reshape(heads, t_len).T, ((0, 0), (0, LANE - heads)))
    d_logit, d_b_f = _forget_bwd("l1_forget_bwd", d_cum, f_logit, b_f)
    d_logit = d_logit.astype(BF16)
    w_grad = ((d, d), BF16, (tn, tn), lambda i, j: (i, j))
    d_w_qkv = _mm("l1_wqkv_grad", (d // tn, d // tn),
                  [(h_b, (t_len, tn), lambda i, j: (0, i))], [(g, (t_len, tn), lambda i, j: (0, j)) for g in (dq, dk, dv)],
                  TN, [w_grad] * 3, summed=False)
    d_w_f = _weight_grad("l1_wf_grad", h_b, d_logit, tn, LANE)
    dh_b = _mm("l1_qkv_bwd", (t_len // tm, d // tn),
               [(g, (tm, d), lambda i, j: (i, 0)) for g in (dq, dk, dv)] + [(d_logit, (tm, LANE), lambda i, j: (i, 0))],
               [(w_qkv, (tn, d), lambda i, j, p=p: (j, p)) for p in range(3)] + [(w_f, (tn, LANE), lambda i, j: (j, 0))], NT,
               [((t_len, d), F32, (tm, tn), lambda i, j: (i, j))])[0]
    dx2, dsh1_1, dsc1_1, dgm_1 = _rms_mod_bwd("l1_mix_norm_bwd", x2, dh_b, dx3, g_mix1, mods[1][1], tm // 2)
    qs = b_w_in.shape[2]
    d_w_b_in = jnp.concatenate([*d_w_qkv, d_w_f[:, :heads]], axis=1).reshape(d, N_DEV, qs).transpose(1, 0, 2)
    rs_fox = _reduce_scatter_start("fox", [d_w_b_in, d_w_b_out.reshape(N_DEV, d // N_DEV, d)], place)

    mods[0][5] = mods[0][5] + rs_fox[3][0, 0]
    dx1, ffn0_grads, (dsh2_0, dsc2_0, dg2_0, dgf_0) = _ffn_bwd("l0", dx2, x1, ffn0_saved, mods[0], g_ffn0, *w_ffn0, tm)
    rs_ffn0 = _reduce_scatter_start("ffn0", ffn0_grads, place)
    dy_a, dg1_0 = _gate_bwd("l0_sgu_gate_bwd", dx1, y_a, mods[0][2] + rs_ffn0[3][0, 0], tm)
    dyy = _proj_bwd("l0_sgu_out_bwd", dy_a, w_a_out, F32, tm, tn)
    d_w_a_out = _weight_grad("l0_sgu_wout_grad", yy, dy_a, tn, tn)
    dpre, d_w_s, d_b_s_t, d_ln_g, d_ln_b, d_b_in = _sgu_bwd("l0_sgu_mix_bwd", pre, dyy, w_s, b_s_t, ln_g, ln_b)
    d_w_a_in = _mm("l0_sgu_win_grad", (N_DEV, d // tn),
                   [(h_a, (t_len, tn), lambda j, i: (0, i))], [(dpre, (t_len, na), lambda j, i: (0, j))], TN,
                   [((N_DEV, d, na), BF16, (None, tn, na), lambda j, i: (j, i, 0))])[0]
    dh_a = _proj_bwd("l0_sgu_in_bwd", dpre, w_a_in, F32, tm, tn)
    dx0, dsh1_0, dsc1_0, dgm_0 = _rms_mod_bwd("l0_mix_norm_bwd", x0, dh_a, dx1, g_mix0, mods[0][1], tm // 2)

    rs_sgu = _reduce_scatter_start("sgu", [d_w_a_in, d_w_a_out.reshape(N_DEV, d // N_DEV, d)], place)

    dmod = jnp.concatenate([dsh1_0, dsc1_0, dg1_0, dsh2_0, dsc2_0, dg2_0,
                            dsh1_1, dsc1_1, dg1_1, dsh2_1, dsc2_1, dg2_1], axis=1)
    small_grads = [dmod, jnp.concatenate([dgm_0, dgm_1], axis=1), jnp.concatenate([dgf_0, dgf_1], axis=1),
                   d_b_in, d_ln_g, d_ln_b, d_w_s, d_b_s_t[:, :groups].T, d_b_f[:, :heads], d_final_g]
    small_w = [ada_b, norm_mix_g, norm_ffn_g, a_b_in, a_ln_g, a_ln_b, a_w_s, a_b_s, b_b_f, final_g]
    small_m = [m_ada_b, m_norm_mix_g, m_norm_ffn_g, m_a_b_in, m_a_ln_g, m_a_ln_b, m_a_w_s, m_a_b_s, m_b_b_f, m_final_g]
    small_v = [v_ada_b, v_norm_mix_g, v_norm_ffn_g, v_a_b_in, v_a_ln_g, v_a_ln_b, v_a_w_s, v_a_b_s, v_b_b_f, v_final_g]
    n_small = sum(w.size for w in small_w)
    pack_rows = -(-n_small // (PACK_W * SUBLANE)) * SUBLANE

    def pack(arrs):
        flat = jnp.concatenate([a.reshape(-1) for a in arrs])
        return jnp.pad(flat, (0, pack_rows * PACK_W - n_small)).reshape(pack_rows, PACK_W)

    def unpack(packed):
        flat, out, pos = packed.reshape(-1), [], 0
        for w in small_w:
            out.append(flat[pos:pos + w.size].reshape(w.shape))
            pos += w.size
        return out

    all_small = _all_gather_small("gather_small_grads", pack(small_grads) + rs_sgu[3][0, 0])
    small_out = [unpack(t) for t in _adamw("adamw_small", pack(small_w)[None], all_small[:, None],
                                           pack(small_m)[None], pack(small_v)[None])]

    dmod_all = all_small.reshape(N_DEV, -1)[:, :2 * d_mod].reshape(N_DEV, 2, d_mod)
    dmod_cols = lax.dynamic_slice_in_dim(dmod_all, me * mod_cols, mod_cols, axis=2).transpose(1, 0, 2)
    c_act_t = jnp.pad(c_act[:N_DEV].T, ((0, 0), (0, LANE - N_DEV)))
    g_ada = _ada_grad("ada_w_grad", c_act_t, dmod_cols, _tile(d, 256))
    r_ada = _adamw("adamw_ada_w", ada_w, g_ada[None], m_ada_w, v_ada_w)

    def arrived(tag, started_rs, after):
        sems, parts, lands, _ = started_rs
        return _scatter_wait(f"{tag}_reduce_chips_wait", sems, parts, lands, after)

    g_gate1, g_up1, g_down1 = arrived("ffn1", rs_ffn1, dx0)
    g_b_in, g_b_out = arrived("fox", rs_fox, dx0)
    g_gate0, g_up0, g_down0 = arrived("ffn0", rs_ffn0, dx0)
    r_b_in = _adamw("adamw_b_w_in", b_w_in, [g_b_in], m_b_w_in, v_b_w_in)
    r_b_out = _adamw("adamw_b_w_out", b_w_out, [g_b_out], m_b_w_out, v_b_w_out)
    r_gate = _adamw("adamw_ffn_w_gate", ffn_w_gate, [g_gate0, g_gate1], m_ffn_w_gate, v_ffn_w_gate)
    r_up = _adamw("adamw_ffn_w_up", ffn_w_up, [g_up0, g_up1], m_ffn_w_up, v_ffn_w_up)
    r_down = _adamw("adamw_ffn_w_down", ffn_w_down, [g_down0, g_down1], m_ffn_w_down, v_ffn_w_down)
    done = [r_b_in, r_b_out, r_gate, r_up, r_down, r_ada, small_out]
    g_a_in, g_a_out = arrived("sgu", rs_sgu, jnp.stack([t[0].reshape(-1)[0] for t in done[:-1]] + [small_out[0][0].reshape(-1)[0]]))
    r_a_in = _adamw("adamw_a_w_in", a_w_in, [g_a_in], m_a_w_in, v_a_w_in)
    r_a_out = _adamw("adamw_a_w_out", a_w_out, [g_a_out], m_a_w_out, v_a_w_out)

    def leaf(k):
        s = small_out[k]
        return [r_ada[k], s[0], s[1], s[2], r_a_in[k], s[3], s[4], s[5], s[6], s[7], r_a_out[k],
                r_b_in[k], s[8], r_b_out[k], r_gate[k], r_up[k], r_down[k], s[9]]

    return (loss, dx0[None], *leaf(0), *leaf(1), *leaf(2), *leaf(3))
```

```python
import functools
import math

import jax
import jax.numpy as jnp
from jax import lax
from jax.experimental import pallas as pl
from jax.experimental.pallas import tpu as pltpu

F32 = jnp.float32
BF16 = jnp.bfloat16
MESH = pl.DeviceIdType.MESH
ANY = pl.BlockSpec(memory_space=pl.ANY)
HBM = pl.BlockSpec(memory_space=pltpu.HBM)
SEM = pl.BlockSpec(memory_space=pltpu.SEMAPHORE)
VMEM_SPEC = pl.BlockSpec(memory_space=pltpu.VMEM)
EFFECT = pltpu.SideEffectType.DATAFLOW_SIDE_EFFECTING

N_DEV = 8
N_CHIP = 4
LANE = 128
SUBLANE = 8
CHUNK = 128
VMEM_LIMIT_BYTES = 52 * 1024 * 1024
NORM_EPS = 1e-6
NEG = -1e30
PACK_W = 1024

ADAM_LR = 0.001
ADAM_B1 = 0.9
ADAM_B2 = 0.999
ADAM_EPS = 1e-08
ADAM_WD = 0.01
ADAM_STEP = 10

NN = ((1,), (0,))
NT = ((1,), (1,))
TN = ((0,), (0,))


def _params(*sem):
    return pltpu.CompilerParams(dimension_semantics=sem or None, vmem_limit_bytes=VMEM_LIMIT_BYTES)


def _tile(n, target, align=LANE):
    best = None
    for t in range(align, min(n, target) + 1, align):
        if n % t == 0:
            best = t
    return best or n


def _dot(a, b, dims):
    return lax.dot_general(a, b, (dims, ((), ())), preferred_element_type=F32)


def _place():
    return lax.axis_index("x"), lax.axis_index("y"), lax.axis_index("c")


def _mm(name, grid, lhs, rhs, dims, outs, epi=None, extras=(), summed=True, k_axis=None):
    n_a, n_b, n_e, n_o = len(lhs), len(rhs), len(extras), len(outs)
    n_acc = 1 if summed else n_b
    nk = grid[k_axis] if k_axis is not None else 1
    acc_shape = tuple(d for d in outs[0][2] if d is not None)

    def body(*refs):
        a_refs, b_refs = refs[:n_a], refs[n_a:n_a + n_b]
        e_refs = refs[n_a + n_b:n_a + n_b + n_e]
        o_refs = refs[n_a + n_b + n_e:n_a + n_b + n_e + n_o]
        acc_refs = refs[n_a + n_b + n_e + n_o:]

        def products():
            a_vals = [r[...].astype(BF16) for r in a_refs]
            ps = [_dot(a_vals[p % n_a], b_refs[p][...].astype(BF16), dims) for p in range(n_b)]
            return [functools.reduce(lambda u, w: u + w, ps)] if summed else ps

        def finish(accs):
            res = epi(accs, [e[...] for e in e_refs]) if epi is not None else accs
            for o_ref, r in zip(o_refs, res):
                o_ref[...] = r.astype(o_ref.dtype)

        if nk == 1:
            finish(products())
        else:
            k = pl.program_id(k_axis)

            @pl.when(k == 0)
            def _():
                for acc in acc_refs:
                    acc[...] = jnp.zeros_like(acc)

            for acc, p in zip(acc_refs, products()):
                acc[...] += p

            @pl.when(k == nk - 1)
            def _():
                finish([acc[...] for acc in acc_refs])

    sem = ["parallel"] * len(grid)
    if k_axis is not None:
        sem[k_axis] = "arbitrary"
    res = pl.pallas_call(
        body, name=name, grid=grid,
        in_specs=[pl.BlockSpec(bs, im) for _, bs, im in (*lhs, *rhs, *extras)],
        out_specs=[pl.BlockSpec(bs, im) for _, _, bs, im in outs],
        out_shape=[jax.ShapeDtypeStruct(s, d) for s, d, _, _ in outs],
        scratch_shapes=[pltpu.VMEM(acc_shape, F32)] * (n_acc if nk > 1 else 0),
        compiler_params=_params(*sem),
    )(*[a for a, _, _ in (*lhs, *rhs, *extras)])
    return res


def _rowwise(name, fn, tiled, whole, out_tiled, out_sums, tm):
    rows = tiled[0].shape[0]
    n_t, n_w, n_o, n_s = len(tiled), len(whole), len(out_tiled), len(out_sums)

    def body(*refs):
        t_refs, w_refs = refs[:n_t], refs[n_t:n_t + n_w]
        o_refs = refs[n_t + n_w:n_t + n_w + n_o]
        s_refs = refs[n_t + n_w + n_o:]
        outs, sums = fn([r[...] for r in t_refs], [r[...] for r in w_refs])
        for o_ref, val in zip(o_refs, outs):
            o_ref[...] = val.astype(o_ref.dtype)

        @pl.when(pl.program_id(0) == 0)
        def _():
            for s_ref in s_refs:
                s_ref[...] = jnp.zeros_like(s_ref)

        for s_ref, val in zip(s_refs, sums):
            s_ref[...] += val

    full = lambda a: pl.BlockSpec(a.shape, lambda i, nd=a.ndim: (0,) * nd)
    res = pl.pallas_call(
        body, name=name, grid=(rows // tm,),
        in_specs=[pl.BlockSpec((tm, a.shape[1]), lambda i: (i, 0)) for a in tiled] + [full(a) for a in whole],
        out_specs=[pl.BlockSpec((tm, n), lambda i: (i, 0)) for n, _ in out_tiled]
        + [pl.BlockSpec(s, lambda i, nd=len(s): (0,) * nd) for s in out_sums],
        out_shape=[jax.ShapeDtypeStruct((rows, n), d) for n, d in out_tiled]
        + [jax.ShapeDtypeStruct(s, F32) for s in out_sums],
        compiler_params=_params("arbitrary"),
    )(*tiled, *whole)
    return res


def _colsum(v):
    return jnp.sum(v, axis=0, keepdims=True)


def _rms_parts(x):
    inv = lax.rsqrt(jnp.mean(x * x, axis=-1, keepdims=True) + NORM_EPS)
    return inv, x * inv


def _rms_mod(name, x, g, sc, sh, tm):
    def fn(t, w):
        _, xhat = _rms_parts(t[0])
        return [xhat * w[0] * (1.0 + w[1]) + w[2]], []
    return _rowwise(name, fn, [x], [g, sc, sh], [(x.shape[1], BF16)], [], tm)[0]


def _rms_mod_bwd(name, x, dh, dres, g, sc, tm):
    d = x.shape[1]

    def fn(t, w):
        x_, dh_, dres_ = t
        g_, sc_ = w
        inv, xhat = _rms_parts(x_)
        dn = dh_ * (1.0 + sc_)
        dxhat = dn * g_
        dx = dres_ + inv * (dxhat - xhat * jnp.mean(dxhat * xhat, axis=-1, keepdims=True))
        return [dx], [_colsum(dh_), _colsum(dh_ * (xhat * g_)), _colsum(dn * xhat)]
    return _rowwise(name, fn, [x, dh, dres], [g, sc], [(d, F32)], [(1, d)] * 3, tm)


def _gate_bwd(name, dx, y, gate, tm):
    d = dx.shape[1]

    def fn(t, w):
        return [t[0] * w[0]], [_colsum(t[0] * t[1].astype(F32))]
    return _rowwise(name, fn, [dx, y], [gate], [(d, BF16)], [(1, d)], tm)


def _final_loss(name, x, target, g, tm):
    d = x.shape[1]

    def fn(t, w):
        inv, xhat = _rms_parts(t[0])
        err = xhat * w[0] - t[1]
        dout = err * (1.0 / d)
        dxhat = dout * w[0]
        dx = inv * (dxhat - xhat * jnp.mean(dxhat * xhat, axis=-1, keepdims=True))
        return [dx], [_colsum(err * err) * (0.5 / d), _colsum(dout * xhat)]
    return _rowwise(name, fn, [x, target], [g], [(d, F32)], [(1, d)] * 2, tm)


def _silu_rows(name, c):
    def fn(t, w):
        return [t[0] * jax.nn.sigmoid(t[0])], []
    return _rowwise(name, fn, [c], [], [(c.shape[1], F32)], [], c.shape[0])[0]


def _gelu(x):
    return 0.5 * x * (1.0 + lax.erf(x * (1.0 / math.sqrt(2.0))))


def _gelu_grad(x):
    cdf = 0.5 * (1.0 + lax.erf(x * (1.0 / math.sqrt(2.0))))
    return cdf + x * jnp.exp(-0.5 * x * x) * (1.0 / math.sqrt(2.0 * math.pi))


def _layer_norm_parts(v):
    mu = jnp.mean(v, axis=-1, keepdims=True)
    cen = v - mu
    rstd = lax.rsqrt(jnp.mean(cen * cen, axis=-1, keepdims=True) + NORM_EPS)
    return rstd, cen * rstd


def _causal(n):
    return lax.broadcasted_iota(jnp.int32, (n, n), 0) >= lax.broadcasted_iota(jnp.int32, (n, n), 1)


def _sgu_fwd(name, pre, w_s, b_s_t, ln_g, ln_b):
    t_len, d2 = pre.shape
    d = d2 // 2
    groups = w_s.shape[0]

    def body(pre_ref, w_ref, bs_ref, g_ref, b_ref, yy_ref):
        z = _gelu(pre_ref[...])
        u, v = z[:, :d], z[:, d:]
        _, vhat = _layer_norm_parts(v)
        vn = (vhat * g_ref[...] + b_ref[...]).astype(BF16)
        mask = _causal(CHUNK)
        bs = bs_ref[...]
        for g in range(groups):
            cols = slice(g * CHUNK, (g + 1) * CHUNK)
            w = jnp.where(mask, w_ref[g], 0.0).astype(BF16)
            sv = _dot(w, vn[:, cols], NN) + bs[:, g:g + 1]
            yy_ref[:, cols] = (u[:, cols] * sv).astype(BF16)

    full = lambda a: pl.BlockSpec(a.shape, lambda i, nd=a.ndim: (0,) * nd)
    return pl.pallas_call(
        body, name=name, grid=(t_len // CHUNK,),
        in_specs=[pl.BlockSpec((CHUNK, d2), lambda i: (i, 0)), full(w_s), full(b_s_t), full(ln_g), full(ln_b)],
        out_specs=pl.BlockSpec((CHUNK, d), lambda i: (i, 0)),
        out_shape=jax.ShapeDtypeStruct((t_len, d), BF16),
        compiler_params=_params("parallel"),
    )(pre, w_s, b_s_t, ln_g, ln_b)


def _sgu_bwd(name, pre, dyy, w_s, b_s_t, ln_g, ln_b):
    t_len, d2 = pre.shape
    d = d2 // 2
    groups = w_s.shape[0]

    def body(pre_ref, dyy_ref, w_ref, bs_ref, g_ref, b_ref, dpre_ref, dw_ref, dbs_ref, dg_ref, db_ref, dbin_ref, dvn_ref):
        @pl.when(pl.program_id(0) == 0)
        def _():
            for r in (dw_ref, dbs_ref, dg_ref, db_ref, dbin_ref):
                r[...] = jnp.zeros_like(r)

        pre_v = pre_ref[...]
        z = _gelu(pre_v)
        u, v = z[:, :d], z[:, d:]
        rstd, vhat = _layer_norm_parts(v)
        vn = (vhat * g_ref[...] + b_ref[...]).astype(BF16)
        mask = _causal(CHUNK)
        bs = bs_ref[...]
        lane = lax.broadcasted_iota(jnp.int32, (CHUNK, LANE), 1)
        dbs = jnp.zeros((CHUNK, LANE), F32)
        for g in range(groups):
            cols = slice(g * CHUNK, (g + 1) * CHUNK)
            w = jnp.where(mask, w_ref[g], 0.0).astype(BF16)
            sv = _dot(w, vn[:, cols], NN) + bs[:, g:g + 1]
            dyy_g = dyy_ref[:, cols]
            dpre_ref[:, cols] = (dyy_g * sv * _gelu_grad(pre_v[:, cols])).astype(BF16)
            dsv = dyy_g * u[:, cols]
            dbs = jnp.where(lane == g, jnp.sum(dsv, axis=1, keepdims=True), dbs)
            dsv_b = dsv.astype(BF16)
            dw_ref[g] += jnp.where(mask, _dot(dsv_b, vn[:, cols], NT), 0.0)
            dvn_ref[:, cols] = _dot(w, dsv_b, TN)
        dbs_ref[...] += dbs
        dvn = dvn_ref[...]
        dg_ref[...] += _colsum(dvn * vhat)
        db_ref[...] += _colsum(dvn)
        dvhat = dvn * g_ref[...]
        dv = rstd * (dvhat - jnp.mean(dvhat, axis=-1, keepdims=True)
                     - vhat * jnp.mean(dvhat * vhat, axis=-1, keepdims=True))
        dpre_ref[:, d:] = (dv * _gelu_grad(pre_v[:, d:])).astype(BF16)
        dbin_ref[...] += _colsum(dpre_ref[...].astype(F32))

    full = lambda a: pl.BlockSpec(a.shape, lambda i, nd=a.ndim: (0,) * nd)
    acc = lambda s: pl.BlockSpec(s, lambda i, nd=len(s): (0,) * nd)
    sums = [(groups, CHUNK, CHUNK), (CHUNK, LANE), (1, d), (1, d), (1, d2)]
    return pl.pallas_call(
        body, name=name, grid=(t_len // CHUNK,),
        in_specs=[pl.BlockSpec((CHUNK, d2), lambda i: (i, 0)), pl.BlockSpec((CHUNK, d), lambda i: (i, 0)),
                  full(w_s), full(b_s_t), full(ln_g), full(ln_b)],
        out_specs=[pl.BlockSpec((CHUNK, d2), lambda i: (i, 0))] + [acc(s) for s in sums],
        out_shape=[jax.ShapeDtypeStruct((t_len, d2), BF16)] + [jax.ShapeDtypeStruct(s, F32) for s in sums],
        scratch_shapes=[pltpu.VMEM((CHUNK, d), F32)],
        compiler_params=_params("arbitrary"),
    )(pre, dyy, w_s, b_s_t, ln_g, ln_b)


def _whole(rows, cols):
    return pl.BlockSpec((rows, cols), lambda i: (0, 0))


def _cum_matrix(reverse):
    r = lax.broadcasted_iota(jnp.int32, (CHUNK, CHUNK), 0)
    c = lax.broadcasted_iota(jnp.int32, (CHUNK, CHUNK), 1)
    return jnp.where((r <= c) if reverse else (r >= c), 1.0, 0.0).astype(F32)


def _forget_cumsum(name, logits, bias):
    t_len = logits.shape[0]

    def body(fl_ref, b_ref, f_ref, ft_ref):
        tri = _cum_matrix(False)

        def step(n, carry):
            off = pl.multiple_of(n * CHUNK, CHUNK)
            xv = fl_ref[pl.ds(off, CHUNK), :] + b_ref[...]
            log_f = jnp.minimum(xv, 0.0) - jnp.log1p(jnp.exp(-jnp.abs(xv)))
            cs = jnp.dot(tri, log_f, precision=lax.Precision.HIGHEST, preferred_element_type=F32) + carry
            f_ref[pl.ds(off, CHUNK), :] = cs
            ft_ref[:, pl.ds(off, CHUNK)] = cs.T
            return cs[CHUNK - 1:CHUNK, :]

        lax.fori_loop(0, t_len // CHUNK, step, jnp.zeros((1, LANE), F32))

    return pl.pallas_call(
        body, name=name, grid=(1,),
        in_specs=[_whole(t_len, LANE), _whole(1, LANE)],
        out_specs=[_whole(t_len, LANE), _whole(LANE, t_len)],
        out_shape=[jax.ShapeDtypeStruct((t_len, LANE), F32), jax.ShapeDtypeStruct((LANE, t_len), F32)],
        compiler_params=_params("arbitrary"),
    )(logits, bias)


def _forget_bwd(name, d_cum, logits, bias):
    t_len = logits.shape[0]
    n_chunks = t_len // CHUNK

    def body(dc_ref, fl_ref, b_ref, dl_ref, db_ref, run_ref):
        @pl.when(pl.program_id(0) == 0)
        def _():
            run_ref[...] = jnp.zeros_like(run_ref)
            db_ref[...] = jnp.zeros_like(db_ref)

        rc = jnp.dot(_cum_matrix(True), dc_ref[...], precision=lax.Precision.HIGHEST,
                     preferred_element_type=F32) + run_ref[0:1, :]
        dl = rc * jax.nn.sigmoid(-(fl_ref[...] + b_ref[...]))
        dl_ref[...] = dl
        db_ref[...] += _colsum(dl)
        run_ref[...] = jnp.broadcast_to(rc[0:1, :], run_ref.shape)

    back = pl.BlockSpec((CHUNK, LANE), lambda i: (n_chunks - 1 - i, 0))
    return pl.pallas_call(
        body, name=name, grid=(n_chunks,),
        in_specs=[back, back, _whole(1, LANE)],
        out_specs=[back, _whole(1, LANE)],
        out_shape=[jax.ShapeDtypeStruct((t_len, LANE), F32), jax.ShapeDtypeStruct((1, LANE), F32)],
        scratch_shapes=[pltpu.VMEM((SUBLANE, LANE), F32)],
        compiler_params=_params("arbitrary"),
    )(d_cum, logits, bias)


def _head_column(f_tile, head):
    lane = lax.broadcasted_iota(jnp.int32, f_tile.shape, 1)
    return jnp.sum(jnp.where(lane == head, f_tile, 0.0), axis=1, keepdims=True)


def _attn_fwd(name, qkv, f_cum, f_keys, heads, tq):
    t_len = qkv.shape[0]
    scale = 1.0 / math.sqrt(CHUNK)

    def body(q_ref, k_ref, v_ref, f_ref, fk_ref, o_ref, lse_ref):
        head, i = pl.program_id(0), pl.program_id(1)
        q = q_ref[...]
        fq = _head_column(f_ref[...], head)
        row = i * tq + lax.broadcasted_iota(jnp.int32, (tq, tq), 0)
        col = lax.broadcasted_iota(jnp.int32, (tq, tq), 1)

        def step(j, carry):
            m, l, acc = carry
            off = pl.multiple_of(j * tq, tq)
            s = _dot(q, k_ref[pl.ds(off, tq), :], NT) * scale + fq - fk_ref[:, pl.ds(off, tq)]
            s = jnp.where(col + off <= row, s, NEG)
            m_new = jnp.maximum(m, jnp.max(s, axis=1, keepdims=True))
            p = jnp.exp(s - m_new)
            alpha = jnp.exp(m - m_new)
            l = alpha * l + jnp.sum(p, axis=1, keepdims=True)
            acc = alpha * acc + _dot(p.astype(BF16), v_ref[pl.ds(off, tq), :], NN)
            return m_new, l, acc

        init = (jnp.full((tq, 1), NEG, F32), jnp.zeros((tq, 1), F32), jnp.zeros((tq, CHUNK), F32))
        m, l, acc = lax.fori_loop(0, i + 1, step, init)
        o_ref[...] = (acc / l).astype(BF16)
        lse_ref[...] = jnp.broadcast_to(m + jnp.log(l), (tq, LANE))

    return pl.pallas_call(
        body, name=name, grid=(heads, t_len // tq),
        in_specs=[pl.BlockSpec((tq, CHUNK), lambda h, i: (i, h)),
                  pl.BlockSpec((t_len, CHUNK), lambda h, i: (0, heads + h)),
                  pl.BlockSpec((t_len, CHUNK), lambda h, i: (0, 2 * heads + h)),
                  pl.BlockSpec((tq, LANE), lambda h, i: (i, 0)),
                  pl.BlockSpec((None, 1, t_len), lambda h, i: (h, 0, 0))],
        out_specs=[pl.BlockSpec((tq, CHUNK), lambda h, i: (i, h)),
                   pl.BlockSpec((None, tq, LANE), lambda h, i: (h, i, 0))],
        out_shape=[jax.ShapeDtypeStruct((t_len, heads * CHUNK), BF16),
                   jax.ShapeDtypeStruct((heads, t_len, LANE), F32)],
        compiler_params=_params("parallel", "parallel"),
    )(qkv, qkv, qkv, f_cum, f_keys)


def _attn_bwd(name, qkv, d_o, f_cum, f_keys, lse, delta, heads, tq):
    t_len = qkv.shape[0]
    d = heads * CHUNK
    n_q = t_len // tq
    scale = 1.0 / math.sqrt(CHUNK)

    def body(q_ref, k_ref, v_ref, do_ref, f_ref, fk_ref, lse_ref, dl_ref, dq_ref, dk_ref, dv_ref, dfq_ref, dfk_ref,
             dq_acc):
        head, j = pl.program_id(0), pl.program_id(1)

        @pl.when(j == 0)
        def _():
            dq_acc[...] = jnp.zeros_like(dq_acc)

        @pl.when((j == 0) & (head == 0))
        def _():
            dfq_ref[...] = jnp.zeros_like(dfq_ref)

        k, v, fk = k_ref[...], v_ref[...], fk_ref[...]
        col = j * tq + lax.broadcasted_iota(jnp.int32, (tq, tq), 1)
        row = lax.broadcasted_iota(jnp.int32, (tq, tq), 0)
        lane = lax.broadcasted_iota(jnp.int32, (tq, LANE), 1)

        def step(i, carry):
            dk, dv, dfk = carry
            off = pl.multiple_of(i * tq, tq)
            rows = pl.ds(off, tq)
            q, do = q_ref[rows, :], do_ref[rows, :]
            fq = _head_column(f_ref[rows, :], head)
            s = _dot(q, k, NT) * scale + fq - fk
            s = jnp.where(col <= row + off, s, NEG)
            p = jnp.exp(s - lse_ref[rows, :][:, 0:1])
            ds = p * (_dot(do, v, NT) - dl_ref[rows, :][:, 0:1])
            ds_b = ds.astype(BF16)
            dq_acc[rows, :] += _dot(ds_b, k, NN)
            dfq_ref[rows, :] += jnp.where(lane == head, jnp.sum(ds, axis=1, keepdims=True), 0.0)
            return (dk + _dot(ds_b, q, TN), dv + _dot(p.astype(BF16), do, TN), dfk + _colsum(ds))

        zero = jnp.zeros((tq, CHUNK), F32)
        dk, dv, dfk = lax.fori_loop(j, n_q, step, (zero, zero, jnp.zeros((1, tq), F32)))
        dk_ref[...] = (dk * scale).astype(BF16)
        dv_ref[...] = dv.astype(BF16)
        dfk_ref[...] = dfk

        @pl.when(j == n_q - 1)
        def _():
            dq_ref[...] = (dq_acc[...] * scale).astype(BF16)

    whole_head = lambda c0: pl.BlockSpec((t_len, CHUNK), lambda h, j: (0, c0 + h))
    per_head = pl.BlockSpec((None, t_len, LANE), lambda h, j: (h, 0, 0))
    key_block = lambda c0: pl.BlockSpec((tq, CHUNK), lambda h, j: (j, c0 + h))
    return pl.pallas_call(
        body, name=name, grid=(heads, n_q),
        in_specs=[whole_head(0), key_block(heads), key_block(2 * heads), whole_head(0),
                  pl.BlockSpec((t_len, LANE), lambda h, j: (0, 0)),
                  pl.BlockSpec((None, 1, tq), lambda h, j: (h, 0, j)),
                  per_head, per_head],
        out_specs=[whole_head(0), key_block(0), key_block(0),
                   pl.BlockSpec((t_len, LANE), lambda h, j: (0, 0)),
                   pl.BlockSpec((None, 1, tq), lambda h, j: (h, 0, j))],
        out_shape=[jax.ShapeDtypeStruct((t_len, d), BF16), jax.ShapeDtypeStruct((t_len, d), BF16),
                   jax.ShapeDtypeStruct((t_len, d), BF16), jax.ShapeDtypeStruct((t_len, LANE), F32),
                   jax.ShapeDtypeStruct((heads, 1, t_len), F32)],
        scratch_shapes=[pltpu.VMEM((t_len, CHUNK), F32)],
        compiler_params=_params("arbitrary", "arbitrary"),
    )(qkv, qkv, qkv, d_o, f_cum, f_keys, lse, delta)


def _ada_grad(name, c_act_t, dmod, tm):
    d = c_act_t.shape[0]
    n_layer, n_b, n_col = dmod.shape

    def body(c_ref, dm_ref, o_ref):
        c, dm = c_ref[...], dm_ref[...]
        acc = c[:, 0:1] * dm[0:1, :]
        for b in range(1, N_DEV):
            acc = acc + c[:, b:b + 1] * dm[b:b + 1, :]
        o_ref[...] = acc

    return pl.pallas_call(
        body, name=name, grid=(n_layer, d // tm),
        in_specs=[pl.BlockSpec((tm, LANE), lambda l, i: (i, 0)),
                  pl.BlockSpec((None, n_b, n_col), lambda l, i: (l, 0, 0))],
        out_specs=pl.BlockSpec((None, tm, n_col), lambda l, i: (l, i, 0)),
        out_shape=jax.ShapeDtypeStruct((n_layer, d, n_col), F32),
        compiler_params=_params("parallel", "parallel"),
    )(c_act_t, dmod)


def _adamw(name, w, parts, m, v):
    n_layer, rows, cols = w.shape
    per_layer = isinstance(parts, (list, tuple))
    parts = list(parts) if per_layer else [parts]
    tr = _tile(rows, max(2 * SUBLANE, (1 << 19) // cols), 2 * SUBLANE)

    def body(*refs):
        w_ref, m_ref, v_ref = refs[:3]
        p_refs = refs[3:3 + len(parts)]
        g_ref, d_ref, mo_ref, vo_ref = refs[3 + len(parts):]
        layer = pl.program_id(0)
        g = None
        for n, p_ref in enumerate(p_refs):
            g_n = p_ref[0].astype(F32)
            for p in range(1, p_ref.shape[0]):
                g_n = g_n + p_ref[p].astype(F32)
            g = g_n if g is None else jnp.where(layer == n, g_n, g)
        m_new = ADAM_B1 * m_ref[...] + (1.0 - ADAM_B1) * g
        v_new = ADAM_B2 * v_ref[...] + (1.0 - ADAM_B2) * jnp.square(g)
        m_hat = m_new / (1.0 - ADAM_B1 ** ADAM_STEP)
        v_hat = v_new / (1.0 - ADAM_B2 ** ADAM_STEP)
        g_ref[...] = g
        d_ref[...] = -ADAM_LR * (m_hat / (jnp.sqrt(v_hat) + ADAM_EPS) + ADAM_WD * w_ref[...])
        mo_ref[...] = m_new
        vo_ref[...] = v_new

    blk = pl.BlockSpec((None, tr, cols), lambda l, i: (l, i, 0))
    if per_layer:
        p_specs = [pl.BlockSpec((p.shape[0], tr, cols), lambda l, i, n=n: (0, jnp.where(l == n, i, 0), 0))
                   for n, p in enumerate(parts)]
    else:
        p_specs = [pl.BlockSpec((parts[0].shape[0], None, tr, cols), lambda l, i: (0, l, i, 0))]
    return pl.pallas_call(
        body, name=name, grid=(n_layer, rows // tr),
        in_specs=[blk] * 3 + p_specs,
        out_specs=[blk] * 4,
        out_shape=[jax.ShapeDtypeStruct(w.shape, F32)] * 4,
        compiler_params=_params("parallel", "parallel"),
    )(w, m, v, *parts)


def _dev_index(p):
    return 4 * p[0] + 2 * p[1] + p[2]


def _other_chips(mx, my):
    return [(1 - mx, my), (mx, 1 - my), (1 - mx, 1 - my)]


def _all_gather_small(name, x):
    rows, cols = x.shape

    def body(x_ref, o_ref, send_sems, recv_sems):
        mx, my, mc = _place()
        me = _dev_index((mx, my, mc))
        o_ref[me] = x_ref[...]

        def copy(dist, slot, peer):
            return pltpu.make_async_remote_copy(
                src_ref=x_ref, dst_ref=o_ref.at[slot], send_sem=send_sems.at[dist - 1], recv_sem=recv_sems.at[dist - 1],
                device_id=(peer // 4, (peer // 2) % 2, peer % 2), device_id_type=MESH)

        sends = [copy(dist, me, (me + dist) % N_DEV) for dist in range(1, N_DEV)]
        for cp in sends:
            cp.start()
        for dist in range(1, N_DEV):
            src = (me + N_DEV - dist) % N_DEV
            copy(dist, src, src).wait_recv()
        for cp in sends:
            cp.wait_send()

    return pl.pallas_call(
        body, name=name,
        out_shape=jax.ShapeDtypeStruct((N_DEV, rows, cols), x.dtype),
        in_specs=[pl.BlockSpec(memory_space=pltpu.VMEM)],
        out_specs=pl.BlockSpec(memory_space=pltpu.VMEM),
        scratch_shapes=[pltpu.SemaphoreType.DMA((N_DEV - 1,)), pltpu.SemaphoreType.DMA((N_DEV - 1,))],
        compiler_params=_params(),
    )(x)


def _split_copy_call(name, body, n_in, sems, through, extra_out=(), first=True):
    sem_shapes = [pltpu.SemaphoreType.DMA((k,)) for k in sems]
    if first:
        return pl.pallas_call(
            body, name=name,
            out_shape=(*sem_shapes, *[pltpu.HBM(t.shape, t.dtype) for t in through], *extra_out),
            in_specs=[HBM] * n_in,
            out_specs=(*[SEM] * len(sems), *[HBM] * len(through), *[VMEM_SPEC] * len(extra_out)),
            input_output_aliases={i: len(sems) + i for i in range(len(through))},
            compiler_params=pltpu.CompilerParams(has_side_effects=EFFECT),
        )
    return pl.pallas_call(
        body, name=name,
        out_shape=tuple(pltpu.HBM(t.shape, t.dtype) for t in through),
        in_specs=[HBM] * len(through) + [SEM] * len(sems) + [ANY],
        out_specs=tuple([HBM] * len(through)),
        input_output_aliases={i: i for i in range(len(through))},
        compiler_params=pltpu.CompilerParams(has_side_effects=EFFECT),
    )


def _own_slot(name, shard, me, token):
    rows, cols = shard.shape
    tr = _tile(rows, max(2 * SUBLANE, (1 << 19) // cols), 2 * SUBLANE)

    def body(me_ref, x_ref, t_ref, o_ref):
        o_ref[...] = (x_ref[...] + t_ref[0:1, 0:1]).astype(BF16)

    return pl.pallas_call(
        body, name=name,
        grid_spec=pltpu.PrefetchScalarGridSpec(
            num_scalar_prefetch=1, grid=(rows // tr,),
            in_specs=[pl.BlockSpec((tr, cols), lambda i, me: (i, 0)),
                      pl.BlockSpec((SUBLANE, LANE), lambda i, me: (0, 0))],
            out_specs=pl.BlockSpec((None, tr, cols), lambda i, me: (me[0], i, 0))),
        out_shape=jax.ShapeDtypeStruct((N_DEV, rows, cols), BF16),
        compiler_params=_params("parallel"),
    )(me, shard, token)


def _gather_start(name, lands):
    n = len(lands)

    def body(*refs):
        land_refs = refs[:n]
        send_sems, recv_sems = refs[n], refs[n + 1]
        token = refs[-1]
        mx, my, mc = _place()
        for a in range(n):
            own = land_refs[a].at[_dev_index((mx, my, mc))]
            for j, chip in enumerate(_other_chips(mx, my)):
                pltpu.make_async_remote_copy(
                    src_ref=own, dst_ref=own, send_sem=send_sems.at[3 * a + j], recv_sem=recv_sems.at[3 * a + j],
                    device_id=(*chip, mc), device_id_type=MESH).start()
        token[...] = jnp.zeros_like(token)

    operands = [pltpu.with_memory_space_constraint(t, pltpu.HBM) for t in lands]
    res = _split_copy_call(name, body, n, (3 * n, 3 * n), operands,
                           extra_out=(jax.ShapeDtypeStruct((SUBLANE, LANE), F32),))(*operands)
    return res[:2], res[2:2 + n], res[-1]


def _gather_wait(name, sems, lands, after):
    n = len(lands)

    def body(*refs):
        land_refs = refs[:n]
        send_sems, recv_sems = refs[n], refs[n + 1]
        mx, my, mc = _place()
        for a in range(n):
            for j, chip in enumerate(_other_chips(mx, my)):
                copy = pltpu.make_async_remote_copy(
                    src_ref=land_refs[a].at[_dev_index((mx, my, mc))], dst_ref=land_refs[a].at[_dev_index((*chip, mc))],
                    send_sem=send_sems.at[3 * a + j], recv_sem=recv_sems.at[3 * a + j],
                    device_id=(*chip, mc), device_id_type=MESH)
                copy.wait_send()
                copy.wait_recv()

    return _split_copy_call(name, body, n, (3 * n, 3 * n), lands, first=False)(*lands, *sems, after)


def _gather_finish(name, lands):
    n = len(lands)

    def body(*refs):
        o_refs = refs[n:2 * n]
        send_sems, recv_sems = refs[2 * n:]
        mx, my, mc = _place()
        blocks = [(mx, my)] + _other_chips(mx, my)

        def copy(a, k, core):
            slot = o_refs[a].at[_dev_index((*blocks[k], core))]
            return pltpu.make_async_remote_copy(
                src_ref=slot, dst_ref=slot, send_sem=send_sems.at[N_CHIP * a + k], recv_sem=recv_sems.at[N_CHIP * a + k],
                device_id=(mx, my, 1 - mc), device_id_type=MESH)

        sends = [copy(a, k, mc) for a in range(n) for k in range(N_CHIP)]
        for cp in sends:
            cp.start()
        for a in range(n):
            for k in range(N_CHIP):
                copy(a, k, 1 - mc).wait_recv()
        for cp in sends:
            cp.wait_send()

    return pl.pallas_call(
        body, name=name,
        out_shape=[jax.ShapeDtypeStruct(t.shape, t.dtype) for t in lands],
        in_specs=[ANY] * n, out_specs=[ANY] * n,
        input_output_aliases={a: a for a in range(n)},
        scratch_shapes=[pltpu.SemaphoreType.DMA((N_CHIP * n,)), pltpu.SemaphoreType.DMA((N_CHIP * n,))],
        compiler_params=_params(),
    )(*lands)


def _exchange_sibling(name, grads):
    n = len(grads)

    def body(*refs):
        g_refs, r_refs = refs[:n], refs[n:2 * n]
        send_sems, recv_sems = refs[2 * n:]
        mx, my, mc = _place()

        def copy(a, q):
            return pltpu.make_async_remote_copy(
                src_ref=g_refs[a].at[q, 1 - mc], dst_ref=r_refs[a].at[q],
                send_sem=send_sems.at[N_CHIP * a + q], recv_sem=recv_sems.at[N_CHIP * a + q],
                device_id=(mx, my, 1 - mc), device_id_type=MESH)

        copies = [copy(a, q) for a in range(n) for q in range(N_CHIP)]
        for cp in copies:
            cp.start()
        for cp in copies:
            cp.wait_recv()
        for cp in copies:
            cp.wait_send()

    return pl.pallas_call(
        body, name=name,
        out_shape=[jax.ShapeDtypeStruct((N_CHIP, *g.shape[2:]), g.dtype) for g in grads],
        in_specs=[ANY] * n, out_specs=[ANY] * n,
        scratch_shapes=[pltpu.SemaphoreType.DMA((N_CHIP * n,)), pltpu.SemaphoreType.DMA((N_CHIP * n,))],
        compiler_params=_params(),
    )(*grads)


def _add_sibling(name, grad, recv, place):
    _, _, rows, cols = grad.shape
    tr = _tile(rows, max(2 * SUBLANE, (1 << 19) // cols), 2 * SUBLANE)

    def body(p_ref, g_ref, r_ref, o_ref, land_ref):
        total = (g_ref[...].astype(F32) + r_ref[...].astype(F32)).astype(o_ref.dtype)
        o_ref[...] = total

        @pl.when(pl.program_id(1) == p_ref[1])
        def _():
            land_ref[...] = total

    out = jax.ShapeDtypeStruct(recv.shape, recv.dtype)
    return pl.pallas_call(
        body, name=name,
        grid_spec=pltpu.PrefetchScalarGridSpec(
            num_scalar_prefetch=1, grid=(rows // tr, N_CHIP),
            in_specs=[pl.BlockSpec((None, None, tr, cols), lambda i, q, p: (q, p[0], i, 0)),
                      pl.BlockSpec((None, tr, cols), lambda i, q, p: (q, i, 0))],
            out_specs=[pl.BlockSpec((None, tr, cols), lambda i, q, p: (q, i, 0)),
                       pl.BlockSpec((None, tr, cols), lambda i, q, p: (p[1], i, 0))]),
        out_shape=[out, out],
        compiler_params=_params("parallel", "arbitrary"),
    )(place, grad, recv)


def _scatter_start(name, parts, lands):
    n = len(parts)

    def body(*refs):
        p_refs, land_refs = refs[:n], refs[n:2 * n]
        send_sems, recv_sems = refs[2 * n], refs[2 * n + 1]
        token = refs[-1]
        mx, my, mc = _place()
        for a in range(n):
            for j, chip in enumerate(_other_chips(mx, my)):
                pltpu.make_async_remote_copy(
                    src_ref=p_refs[a].at[2 * chip[0] + chip[1]], dst_ref=land_refs[a].at[2 * mx + my],
                    send_sem=send_sems.at[3 * a + j], recv_sem=recv_sems.at[3 * a + j],
                    device_id=(*chip, mc), device_id_type=MESH).start()
        token[...] = jnp.zeros_like(token)

    operands = [pltpu.with_memory_space_constraint(t, pltpu.HBM) for t in (*parts, *lands)]
    res = _split_copy_call(name, body, 2 * n, (3 * n, 3 * n), operands,
                           extra_out=(jax.ShapeDtypeStruct((SUBLANE, LANE), F32),))(*operands)
    return res[:2], res[2:2 + n], res[2 + n:2 + 2 * n], res[-1]


def _scatter_wait(name, sems, parts, lands, after):
    n = len(parts)

    def body(*refs):
        p_refs, land_refs = refs[:n], refs[n:2 * n]
        send_sems, recv_sems = refs[2 * n], refs[2 * n + 1]
        mx, my, mc = _place()
        for a in range(n):
            for j, chip in enumerate(_other_chips(mx, my)):
                copy = pltpu.make_async_remote_copy(
                    src_ref=p_refs[a].at[2 * chip[0] + chip[1]], dst_ref=land_refs[a].at[2 * chip[0] + chip[1]],
                    send_sem=send_sems.at[3 * a + j], recv_sem=recv_sems.at[3 * a + j],
                    device_id=(*chip, mc), device_id_type=MESH)
                copy.wait_send()
                copy.wait_recv()

    res = _split_copy_call(name, body, 2 * n, (3 * n, 3 * n), [*parts, *lands], first=False)(
        *parts, *lands, *sems, after)
    return res[n:]


def _reduce_scatter_start(tag, grads, place):
    wide = [g.reshape(N_CHIP, 2, *g.shape[1:]) for g in grads]
    from_sibling = _exchange_sibling(f"{tag}_reduce_sibling", wide)
    added = [_add_sibling(f"{tag}_add_sibling_{n}", g, r, place) for n, (g, r) in enumerate(zip(wide, from_sibling))]
    return _scatter_start(f"{tag}_reduce_chips_start", [p for p, _ in added], [l for _, l in added])


def _row(v):
    return v.reshape(1, -1)


def _out_proj(name, act, w_out, x, gate, tm, tn):
    t_len, d = x.shape
    k = act.shape[1]
    return _mm(name, (t_len // tm, d // tn),
               [(act, (tm, k), lambda i, j: (i, 0))], [(w_out, (k, tn), lambda i, j: (0, j))], NN,
               [((t_len, d), BF16, (tm, tn), lambda i, j: (i, j)), ((t_len, d), F32, (tm, tn), lambda i, j: (i, j))],
               epi=lambda accs, e: [accs[0], e[0] + e[1] * accs[0]],
               extras=[(x, (tm, tn), lambda i, j: (i, j)), (gate, (1, tn), lambda i, j: (0, j))])


def _proj_bwd(name, dy, w_out, dtype, tm, tn):
    t_len, d = dy.shape
    k = w_out.shape[0]
    return _mm(name, (t_len // tm, k // tn),
               [(dy, (tm, d), lambda i, j: (i, 0))], [(w_out, (tn, d), lambda i, j: (j, 0))], NT,
               [((t_len, k), dtype, (tm, tn), lambda i, j: (i, j))])[0]


def _weight_grad(name, act, dy, tm, tn):
    t_len, k = act.shape
    n = dy.shape[1]
    return _mm(name, (k // tm, n // tn),
               [(act, (t_len, tm), lambda i, j: (0, i))], [(dy, (t_len, tn), lambda i, j: (0, j))], TN,
               [((k, n), BF16, (tm, tn), lambda i, j: (i, j))])[0]


def _ffn_fwd(tag, x1, mod, g_norm, w_gate, w_up, w_down, tm):
    t_len, d = x1.shape
    fs = w_gate.shape[2]
    sh2, sc2, g2 = mod[3], mod[4], mod[5]
    h2 = _rms_mod(f"{tag}_ffn_norm", x1, g_norm, sc2, sh2, tm)
    hidden = ((N_DEV, t_len, fs), BF16, (None, tm, fs), lambda j, i: (j, i, 0))

    def swiglu(accs, _):
        a, b = accs
        return [a, b, a * jax.nn.sigmoid(a) * b]

    a, b, s = _mm(f"{tag}_ffn_up", (N_DEV, t_len // tm),
                  [(h2, (tm, d), lambda j, i: (i, 0))],
                  [(w_gate, (None, d, fs), lambda j, i: (j, 0, 0)), (w_up, (None, d, fs), lambda j, i: (j, 0, 0))],
                  NN, [hidden] * 3, epi=swiglu, summed=False)
    f, x2 = _mm(f"{tag}_ffn_down", (t_len // tm, 1, N_DEV),
                [(s, (None, tm, fs), lambda i, j, k: (k, i, 0))], [(w_down, (None, fs, d), lambda i, j, k: (k, 0, 0))],
                NN,
                [((t_len, d), BF16, (tm, d), lambda i, j, k: (i, 0)), ((t_len, d), F32, (tm, d), lambda i, j, k: (i, 0))],
                epi=lambda accs, e: [accs[0], e[0] + e[1] * accs[0]],
                extras=[(x1, (tm, d), lambda i, j, k: (i, 0)), (g2, (1, d), lambda i, j, k: (0, 0))], k_axis=2)
    return x2, (h2, a, b, s, f)


def _ffn_bwd(tag, dx2, x1, saved, mod, g_norm, w_gate, w_up, w_down, tm):
    t_len, d = x1.shape
    fs = w_gate.shape[2]
    h2, a, b, s, f = saved
    sc2, g2 = mod[4], mod[5]
    df, dg2 = _gate_bwd(f"{tag}_ffn_gate_bwd", dx2, f, g2, tm)
    hidden = ((N_DEV, t_len, fs), BF16, (None, tm, fs), lambda j, i: (j, i, 0))
    hid_in = lambda arr: (arr, (None, tm, fs), lambda j, i: (j, i, 0))

    def swiglu_bwd(accs, e):
        a_, b_ = e[0].astype(F32), e[1].astype(F32)
        sig = jax.nn.sigmoid(a_)
        return [accs[0] * b_ * sig * (1.0 + a_ * (1.0 - sig)), accs[0] * a_ * sig]

    da, db = _mm(f"{tag}_ffn_down_bwd", (N_DEV, t_len // tm),
                 [(df, (tm, d), lambda j, i: (i, 0))], [(w_down, (None, fs, d), lambda j, i: (j, 0, 0))], NT,
                 [hidden] * 2, epi=swiglu_bwd, extras=[hid_in(a), hid_in(b)])
    tn = _tile(d, 512)
    d_wd = _mm(f"{tag}_ffn_wdown_grad", (N_DEV, d // tn),
               [(s, (None, t_len, fs), lambda j, i: (j, 0, 0))], [(df, (t_len, tn), lambda j, i: (0, i))], TN,
               [((N_DEV, fs, d), BF16, (None, fs, tn), lambda j, i: (j, 0, i))])[0]
    w_grad = ((N_DEV, d, fs), BF16, (None, tn, fs), lambda j, i: (j, i, 0))
    d_wg, d_wu = _mm(f"{tag}_ffn_wup_grad", (N_DEV, d // tn),
                     [(h2, (t_len, tn), lambda j, i: (0, i))],
                     [(da, (None, t_len, fs), lambda j, i: (j, 0, 0)), (db, (None, t_len, fs), lambda j, i: (j, 0, 0))],
                     TN, [w_grad] * 2, summed=False)
    dh2 = _mm(f"{tag}_ffn_up_bwd", (t_len // tm, 1, N_DEV),
              [(da, (None, tm, fs), lambda i, j, k: (k, i, 0)), (db, (None, tm, fs), lambda i, j, k: (k, i, 0))],
              [(w_gate, (None, d, fs), lambda i, j, k: (k, 0, 0)), (w_up, (None, d, fs), lambda i, j, k: (k, 0, 0))],
              NT, [((t_len, d), F32, (tm, d), lambda i, j, k: (i, 0))], k_axis=2)[0]
    dx1, dsh2, dsc2, dgn = _rms_mod_bwd(f"{tag}_ffn_norm_bwd", x1, dh2, dx2, g_norm, sc2, tm // 2)
    return dx1, (d_wg, d_wu, d_wd), (dsh2, dsc2, dg2, dgn)


def kernel(x, c, ada_w, ada_b, norm_mix_g, norm_ffn_g, a_w_in, a_b_in, a_ln_g, a_ln_b, a_w_s, a_b_s, a_w_out, b_w_in, b_b_f, b_w_out, ffn_w_gate, ffn_w_up, ffn_w_down, final_g, loss_target, m_ada_w, m_ada_b, m_norm_mix_g, m_norm_ffn_g, m_a_w_in, m_a_b_in, m_a_ln_g, m_a_ln_b, m_a_w_s, m_a_b_s, m_a_w_out, m_b_w_in, m_b_b_f, m_b_w_out, m_ffn_w_gate, m_ffn_w_up, m_ffn_w_down, m_final_g, v_ada_w, v_ada_b, v_norm_mix_g, v_norm_ffn_g, v_a_w_in, v_a_b_in, v_a_ln_g, v_a_ln_b, v_a_w_s, v_a_b_s, v_a_w_out, v_b_w_in, v_b_b_f, v_b_w_out, v_ffn_w_gate, v_ffn_w_up, v_ffn_w_down, v_final_g):
    t_len, d = x.shape[1], x.shape[2]
    heads = d // CHUNK
    groups = a_w_s.shape[1]
    d_mod = 6 * d
    mod_cols = ada_w.shape[2]
    tm = _tile(t_len, 512)
    tn = _tile(d, 512)
    tq = _tile(t_len, 256)
    mx, my, mc = _place()
    me = _dev_index((mx, my, mc))
    x0, target = x[0], loss_target[0]

    groups_w = {"sgu": [a_w_in[0], a_w_out[0]], "ffn0": [ffn_w_gate[0], ffn_w_up[0], ffn_w_down[0]],
                "fox": [b_w_in[0], b_w_out[0]], "ffn1": [ffn_w_gate[1], ffn_w_up[1], ffn_w_down[1]]}
    c_all = _all_gather_small("gather_c", jnp.pad(c, ((0, SUBLANE - 1), (0, 0))))[:, 0, :]
    c_act = _silu_rows("silu_c", jnp.pad(c_all, ((0, 2 * SUBLANE - N_DEV), (0, 0))))
    mod_part = _mm("mod_matmul", (2, mod_cols // _tile(mod_cols, 512)),
                   [(c_act, (2 * SUBLANE, d), lambda l, j: (0, 0))],
                   [(ada_w, (None, d, _tile(mod_cols, 512)), lambda l, j: (l, 0, j))], NN,
                   [((2, 2 * SUBLANE, mod_cols), F32, (None, 2 * SUBLANE, _tile(mod_cols, 512)), lambda l, j: (l, 0, j))])[0]
    mod_all = _all_gather_small("gather_mod", mod_part[:, :N_DEV, :].reshape(2 * N_DEV, mod_cols))
    mod_mine = lax.dynamic_index_in_dim(mod_all.reshape(N_DEV, 2, N_DEV, mod_cols), me, axis=2, keepdims=False)
    mod = mod_mine.transpose(1, 0, 2).reshape(2, d_mod) + ada_b
    me_op = me.astype(jnp.int32).reshape(1)
    started, token = {}, jnp.zeros((SUBLANE, LANE), F32) + mod_all[0, 0, 0] * 0.0
    for key, group in groups_w.items():
        lands = [_own_slot(f"own_slot_{key}_{n}", s, me_op, token) for n, s in enumerate(group)]
        started[key] = _gather_start(f"gather_{key}_start", lands)
        token = started[key][2]
    mod = mod + token[0, 0]
    mods = [[_row(mod[l, k * d:(k + 1) * d]) for k in range(6)] for l in range(2)]

    def gathered(key, after):
        sems, lands, _ = started[key]
        return _gather_finish(f"gather_{key}_finish", _gather_wait(f"gather_{key}_wait", sems, lands, after))

    g_mix0, g_ffn0 = _row(norm_mix_g[0]), _row(norm_ffn_g[0])
    w_a_in, w_a_out = gathered("sgu", mod)
    w_a_in = w_a_in.transpose(1, 0, 2).reshape(d, 2 * d)
    w_a_out = w_a_out.reshape(d, d)
    na = a_w_in.shape[2]
    h_a = _rms_mod("l0_mix_norm", x0, g_mix0, mods[0][1], mods[0][0], tm)
    pre = _mm("l0_sgu_in", (t_len // tm, 2 * d // tn),
              [(h_a, (tm, d), lambda i, j: (i, 0))], [(w_a_in, (d, tn), lambda i, j: (0, j))], NN,
              [((t_len, 2 * d), F32, (tm, tn), lambda i, j: (i, j))],
              epi=lambda accs, e: [accs[0] + e[0]], extras=[(a_b_in, (1, tn), lambda i, j: (0, j))])[0]
    w_s, b_s_t = a_w_s[0], jnp.pad(a_b_s[0].T, ((0, 0), (0, LANE - groups)))
    ln_g, ln_b = a_ln_g, a_ln_b
    yy = _sgu_fwd("l0_sgu_mix", pre, w_s, b_s_t, ln_g, ln_b)
    y_a, x1 = _out_proj("l0_sgu_out", yy, w_a_out, x0, mods[0][2], tm, tn)
    w_ffn0 = gathered("ffn0", x1)
    x2, ffn0_saved = _ffn_fwd("l0", x1, mods[0], g_ffn0, *w_ffn0, tm)

    g_mix1, g_ffn1 = _row(norm_mix_g[1]), _row(norm_ffn_g[1])
    h_b = _rms_mod("l1_mix_norm", x2, g_mix1, mods[1][1], mods[1][0], tm)
    w_b_in, w_b_out = gathered("fox", h_b)
    w_b_out = w_b_out.reshape(d, d)
    w_b_full = w_b_in.transpose(1, 0, 2).reshape(d, 3 * d + heads)
    w_qkv = w_b_full[:, :3 * d]
    w_f = jnp.pad(w_b_full[:, 3 * d:], ((0, 0), (0, LANE - heads)))
    qkv = _mm("l1_qkv", (t_len // tm, 3 * d // tn),
              [(h_b, (tm, d), lambda i, j: (i, 0))], [(w_qkv, (d, tn), lambda i, j: (0, j))], NN,
              [((t_len, 3 * d), BF16, (tm, tn), lambda i, j: (i, j))])[0]
    f_logit = _mm("l1_forget_logit", (t_len // tm, 1),
                  [(h_b, (tm, d), lambda i, j: (i, 0))], [(w_f, (d, LANE), lambda i, j: (0, 0))], NN,
                  [((t_len, LANE), F32, (tm, LANE), lambda i, j: (i, 0))])[0]
    b_f = jnp.pad(b_b_f, ((0, 0), (0, LANE - heads)))
    f_cum, f_cum_t = _forget_cumsum("l1_forget_cumsum", f_logit, b_f)
    f_keys = f_cum_t[:heads].reshape(heads, 1, t_len)
    o, lse = _attn_fwd("l1_attn", qkv, f_cum, f_keys, heads, tq)
    y_b, x3 = _out_proj("l1_attn_out", o, w_b_out, x2, mods[1][2], tm, tn)
    w_ffn1 = gathered("ffn1", x3)
    x4, ffn1_saved = _ffn_fwd("l1", x3, mods[1], g_ffn1, *w_ffn1, tm)

    dx4, loss_cols, d_final_g = _final_loss("loss_head", x4, target, _row(final_g), tm // 2)
    loss = lax.psum(jnp.sum(loss_cols), ("x", "y", "c"))

    place = jnp.stack([mc, 2 * mx + my]).astype(jnp.int32)
    dx3, ffn1_grads, (dsh2_1, dsc2_1, dg2_1, dgf_1) = _ffn_bwd("l1", dx4, x3, ffn1_saved, mods[1], g_ffn1, *w_ffn1, tm)
    rs_ffn1 = _reduce_scatter_start("ffn1", ffn1_grads, place)
    dy_b, dg1_1 = _gate_bwd("l1_attn_gate_bwd", dx3, y_b, mods[1][2] + rs_ffn1[3][0, 0], tm)
    heads_tn = tn // CHUNK

    def with_delta(accs, e):
        prod = accs[0] * e[0].astype(F32)
        sums = [jnp.sum(prod[:, h * CHUNK:(h + 1) * CHUNK], axis=1, keepdims=True) for h in range(heads_tn)]
        return [accs[0], jnp.stack([jnp.broadcast_to(v, (tm, LANE)) for v in sums])]

    d_o, delta = _mm("l1_attn_out_bwd", (t_len // tm, d // tn),
                     [(dy_b, (tm, d), lambda i, j: (i, 0))], [(w_b_out, (tn, d), lambda i, j: (j, 0))], NT,
                     [((t_len, d), BF16, (tm, tn), lambda i, j: (i, j)),
                      ((heads, t_len, LANE), F32, (heads_tn, tm, LANE), lambda i, j: (j, i, 0))],
                     epi=with_delta, extras=[(o, (tm, tn), lambda i, j: (i, j))])
    d_w_b_out = _weight_grad("l1_attn_wout_grad", o, dy_b, tn, tn)
    dq, dk, dv, dfq, dfk = _attn_bwd("l1_attn_bwd", qkv, d_o, f_cum, f_keys, lse, delta, heads, tq)
    d_cum = dfq - jnp.pad(dfk.reshape(heads, t_len).T, ((0, 0), (0, LANE - heads)))
    d_logit, d_b_f = _forget_bwd("l1_forget_bwd", d_cum, f_logit, b_f)
    d_logit = d_logit.astype(BF16)
    w_grad = ((d, d), BF16, (tn, tn), lambda i, j: (i, j))
    d_w_qkv = _mm("l1_wqkv_grad", (d // tn, d // tn),
                  [(h_b, (t_len, tn), lambda i, j: (0, i))], [(g, (t_len, tn), lambda i, j: (0, j)) for g in (dq, dk, dv)],
                  TN, [w_grad] * 3, summed=False)
    d_w_f = _weight_grad("l1_wf_grad", h_b, d_logit, tn, LANE)
    dh_b = _mm("l1_qkv_bwd", (t_len // tm, d // tn),
               [(g, (tm, d), lambda i, j: (i, 0)) for g in (dq, dk, dv)] + [(d_logit, (tm, LANE), lambda i, j: (i, 0))],
               [(w_qkv, (tn, d), lambda i, j, p=p: (j, p)) for p in range(3)] + [(w_f, (tn, LANE), lambda i, j: (j, 0))], NT,
               [((t_len, d), F32, (tm, tn), lambda i, j: (i, j))])[0]
    dx2, dsh1_1, dsc1_1, dgm_1 = _rms_mod_bwd("l1_mix_norm_bwd", x2, dh_b, dx3, g_mix1, mods[1][1], tm // 2)
    qs = b_w_in.shape[2]
    d_w_b_in = jnp.concatenate([*d_w_qkv, d_w_f[:, :heads]], axis=1).reshape(d, N_DEV, qs).transpose(1, 0, 2)
    rs_fox = _reduce_scatter_start("fox", [d_w_b_in, d_w_b_out.reshape(N_DEV, d // N_DEV, d)], place)

    mods[0][5] = mods[0][5] + rs_fox[3][0, 0]
    dx1, ffn0_grads, (dsh2_0, dsc2_0, dg2_0, dgf_0) = _ffn_bwd("l0", dx2, x1, ffn0_saved, mods[0], g_ffn0, *w_ffn0, tm)
    rs_ffn0 = _reduce_scatter_start("ffn0", ffn0_grads, place)
    dy_a, dg1_0 = _gate_bwd("l0_sgu_gate_bwd", dx1, y_a, mods[0][2] + rs_ffn0[3][0, 0], tm)
    dyy = _proj_bwd("l0_sgu_out_bwd", dy_a, w_a_out, F32, tm, tn)
    d_w_a_out = _weight_grad("l0_sgu_wout_grad", yy, dy_a, tn, tn)
    dpre, d_w_s, d_b_s_t, d_ln_g, d_ln_b, d_b_in = _sgu_bwd("l0_sgu_mix_bwd", pre, dyy, w_s, b_s_t, ln_g, ln_b)
    d_w_a_in = _mm("l0_sgu_win_grad", (N_DEV, d // tn),
                   [(h_a, (t_len, tn), lambda j, i: (0, i))], [(dpre, (t_len, na), lambda j, i: (0, j))], TN,
                   [((N_DEV, d, na), BF16, (None, tn, na), lambda j, i: (j, i, 0))])[0]
    dh_a = _proj_bwd("l0_sgu_in_bwd", dpre, w_a_in, F32, tm, tn)
    dx0, dsh1_0, dsc1_0, dgm_0 = _rms_mod_bwd("l0_mix_norm_bwd", x0, dh_a, dx1, g_mix0, mods[0][1], tm // 2)

    dmod = jnp.concatenate([dsh1_0, dsc1_0, dg1_0, dsh2_0, dsc2_0, dg2_0,
                            dsh1_1, dsc1_1, dg1_1, dsh2_1, dsc2_1, dg2_1], axis=1)
    small_grads = [dmod, jnp.concatenate([dgm_0, dgm_1], axis=1), jnp.concatenate([dgf_0, dgf_1], axis=1),
                   d_b_in, d_ln_g, d_ln_b, d_w_s, d_b_s_t[:, :groups].T, d_b_f[:, :heads], d_final_g]
    small_w = [ada_b, norm_mix_g, norm_ffn_g, a_b_in, a_ln_g, a_ln_b, a_w_s, a_b_s, b_b_f, final_g]
    small_m = [m_ada_b, m_norm_mix_g, m_norm_ffn_g, m_a_b_in, m_a_ln_g, m_a_ln_b, m_a_w_s, m_a_b_s, m_b_b_f, m_final_g]
    small_v = [v_ada_b, v_norm_mix_g, v_norm_ffn_g, v_a_b_in, v_a_ln_g, v_a_ln_b, v_a_w_s, v_a_b_s, v_b_b_f, v_final_g]
    n_small = sum(w.size for w in small_w)
    pack_rows = -(-n_small // (PACK_W * SUBLANE)) * SUBLANE

    def pack(arrs):
        flat = jnp.concatenate([a.reshape(-1) for a in arrs])
        return jnp.pad(flat, (0, pack_rows * PACK_W - n_small)).reshape(pack_rows, PACK_W)

    def unpack(packed):
        flat, out, pos = packed.reshape(-1), [], 0
        for w in small_w:
            out.append(flat[pos:pos + w.size].reshape(w.shape))
            pos += w.size
        return out

    all_small = _all_gather_small("gather_small_grads", pack(small_grads))

    behind_small = d_w_a_out + (all_small[0, 0, 0] * 0.0).astype(BF16)
    rs_sgu = _reduce_scatter_start("sgu", [d_w_a_in, behind_small.reshape(N_DEV, d // N_DEV, d)], place)
    small_out = [unpack(t) for t in _adamw("adamw_small", pack(small_w)[None], all_small[:, None],
                                           pack(small_m)[None], pack(small_v)[None])]

    dmod_all = all_small.reshape(N_DEV, -1)[:, :2 * d_mod].reshape(N_DEV, 2, d_mod)
    dmod_cols = lax.dynamic_slice_in_dim(dmod_all, me * mod_cols, mod_cols, axis=2).transpose(1, 0, 2)
    c_act_t = jnp.pad(c_act[:N_DEV].T, ((0, 0), (0, LANE - N_DEV)))
    g_ada = _ada_grad("ada_w_grad", c_act_t, dmod_cols, _tile(d, 256))
    r_ada = _adamw("adamw_ada_w", ada_w, g_ada[None], m_ada_w, v_ada_w)

    def arrived(tag, started_rs, after):
        sems, parts, lands, _ = started_rs
        return _scatter_wait(f"{tag}_reduce_chips_wait", sems, parts, lands, after)

    g_gate1, g_up1, g_down1 = arrived("ffn1", rs_ffn1, rs_sgu[3])
    g_b_in, g_b_out = arrived("fox", rs_fox, rs_sgu[3])
    g_gate0, g_up0, g_down0 = arrived("ffn0", rs_ffn0, rs_sgu[3])
    r_b_in = _adamw("adamw_b_w_in", b_w_in, [g_b_in], m_b_w_in, v_b_w_in)
    r_b_out = _adamw("adamw_b_w_out", b_w_out, [g_b_out], m_b_w_out, v_b_w_out)
    r_gate = _adamw("adamw_ffn_w_gate", ffn_w_gate, [g_gate0, g_gate1], m_ffn_w_gate, v_ffn_w_gate)
    r_up = _adamw("adamw_ffn_w_up", ffn_w_up, [g_up0, g_up1], m_ffn_w_up, v_ffn_w_up)
    r_down = _adamw("adamw_ffn_w_down", ffn_w_down, [g_down0, g_down1], m_ffn_w_down, v_ffn_w_down)
    done = [r_b_in, r_b_out, r_gate, r_up, r_down, r_ada, small_out]
    g_a_in, g_a_out = arrived("sgu", rs_sgu, jnp.stack([t[0].reshape(-1)[0] for t in done[:-1]] + [small_out[0][0].reshape(-1)[0]]))
    r_a_in = _adamw("adamw_a_w_in", a_w_in, [g_a_in], m_a_w_in, v_a_w_in)
    r_a_out = _adamw("adamw_a_w_out", a_w_out, [g_a_out], m_a_w_out, v_a_w_out)

    def leaf(k):
        s = small_out[k]
        return [r_ada[k], s[0], s[1], s[2], r_a_in[k], s[3], s[4], s[5], s[6], s[7], r_a_out[k],
                r_b_in[k], s[8], r_b_out[k], r_gate[k], r_up[k], r_down[k], s[9]]

    return (loss, dx0[None], *leaf(0), *leaf(1), *leaf(2), *leaf(3))
```

```python
import functools
import math

import jax
import jax.numpy as jnp
from jax import lax
from jax.experimental import pallas as pl
from jax.experimental.pallas import tpu as pltpu

F32 = jnp.float32
BF16 = jnp.bfloat16
MESH = pl.DeviceIdType.MESH
ANY = pl.BlockSpec(memory_space=pl.ANY)
HBM = pl.BlockSpec(memory_space=pltpu.HBM)
SEM = pl.BlockSpec(memory_space=pltpu.SEMAPHORE)
VMEM_SPEC = pl.BlockSpec(memory_space=pltpu.VMEM)
EFFECT = pltpu.SideEffectType.DATAFLOW_SIDE_EFFECTING

N_DEV = 8
N_CHIP = 4
LANE = 128
SUBLANE = 8
CHUNK = 128
VMEM_LIMIT_BYTES = 52 * 1024 * 1024
NORM_EPS = 1e-6
NEG = -1e30
PACK_W = 1024

ADAM_LR = 0.001
ADAM_B1 = 0.9
ADAM_B2 = 0.999
ADAM_EPS = 1e-08
ADAM_WD = 0.01
ADAM_STEP = 10

NN = ((1,), (0,))
NT = ((1,), (1,))
TN = ((0,), (0,))


def _params(*sem):
    return pltpu.CompilerParams(dimension_semantics=sem or None, vmem_limit_bytes=VMEM_LIMIT_BYTES)


def _tile(n, target, align=LANE):
    best = None
    for t in range(align, min(n, target) + 1, align):
        if n % t == 0:
            best = t
    return best or n


def _dot(a, b, dims):
    return lax.dot_general(a, b, (dims, ((), ())), preferred_element_type=F32)


def _place():
    return lax.axis_index("x"), lax.axis_index("y"), lax.axis_index("c")


def _mm(name, grid, lhs, rhs, dims, outs, epi=None, extras=(), summed=True, k_axis=None):
    n_a, n_b, n_e, n_o = len(lhs), len(rhs), len(extras), len(outs)
    n_acc = 1 if summed else n_b
    nk = grid[k_axis] if k_axis is not None else 1
    acc_shape = tuple(d for d in outs[0][2] if d is not None)

    def body(*refs):
        a_refs, b_refs = refs[:n_a], refs[n_a:n_a + n_b]
        e_refs = refs[n_a + n_b:n_a + n_b + n_e]
        o_refs = refs[n_a + n_b + n_e:n_a + n_b + n_e + n_o]
        acc_refs = refs[n_a + n_b + n_e + n_o:]

        def products():
            a_vals = [r[...].astype(BF16) for r in a_refs]
            ps = [_dot(a_vals[p % n_a], b_refs[p][...].astype(BF16), dims) for p in range(n_b)]
            return [functools.reduce(lambda u, w: u + w, ps)] if summed else ps

        def finish(accs):
            res = epi(accs, [e[...] for e in e_refs]) if epi is not None else accs
            for o_ref, r in zip(o_refs, res):
                o_ref[...] = r.astype(o_ref.dtype)

        if nk == 1:
            finish(products())
        else:
            k = pl.program_id(k_axis)

            @pl.when(k == 0)
            def _():
                for acc in acc_refs:
                    acc[...] = jnp.zeros_like(acc)

            for acc, p in zip(acc_refs, products()):
                acc[...] += p

            @pl.when(k == nk - 1)
            def _():
                finish([acc[...] for acc in acc_refs])

    sem = ["parallel"] * len(grid)
    if k_axis is not None:
        sem[k_axis] = "arbitrary"
    res = pl.pallas_call(
        body, name=name, grid=grid,
        in_specs=[pl.BlockSpec(bs, im) for _, bs, im in (*lhs, *rhs, *extras)],
        out_specs=[pl.BlockSpec(bs, im) for _, _, bs, im in outs],
        out_shape=[jax.ShapeDtypeStruct(s, d) for s, d, _, _ in outs],
        scratch_shapes=[pltpu.VMEM(acc_shape, F32)] * (n_acc if nk > 1 else 0),
        compiler_params=_params(*sem),
    )(*[a for a, _, _ in (*lhs, *rhs, *extras)])
    return res


def _rowwise(name, fn, tiled, whole, out_tiled, out_sums, tm):
    rows = tiled[0].shape[0]
    n_t, n_w, n_o, n_s = len(tiled), len(whole), len(out_tiled), len(out_sums)

    def body(*refs):
        t_refs, w_refs = refs[:n_t], refs[n_t:n_t + n_w]
        o_refs = refs[n_t + n_w:n_t + n_w + n_o]
        s_refs = refs[n_t + n_w + n_o:]
        outs, sums = fn([r[...] for r in t_refs], [r[...] for r in w_refs])
        for o_ref, val in zip(o_refs, outs):
            o_ref[...] = val.astype(o_ref.dtype)

        @pl.when(pl.program_id(0) == 0)
        def _():
            for s_ref in s_refs:
                s_ref[...] = jnp.zeros_like(s_ref)

        for s_ref, val in zip(s_refs, sums):
            s_ref[...] += val

    full = lambda a: pl.BlockSpec(a.shape, lambda i, nd=a.ndim: (0,) * nd)
    res = pl.pallas_call(
        body, name=name, grid=(rows // tm,),
        in_specs=[pl.BlockSpec((tm, a.shape[1]), lambda i: (i, 0)) for a in tiled] + [full(a) for a in whole],
        out_specs=[pl.BlockSpec((tm, n), lambda i: (i, 0)) for n, _ in out_tiled]
        + [pl.BlockSpec(s, lambda i, nd=len(s): (0,) * nd) for s in out_sums],
        out_shape=[jax.ShapeDtypeStruct((rows, n), d) for n, d in out_tiled]
        + [jax.ShapeDtypeStruct(s, F32) for s in out_sums],
        compiler_params=_params("arbitrary"),
    )(*tiled, *whole)
    return res


def _colsum(v):
    return jnp.sum(v, axis=0, keepdims=True)


def _rms_parts(x):
    inv = lax.rsqrt(jnp.mean(x * x, axis=-1, keepdims=True) + NORM_EPS)
    return inv, x * inv


def _rms_mod(name, x, g, sc, sh, tm):
    def fn(t, w):
        _, xhat = _rms_parts(t[0])
        return [xhat * w[0] * (1.0 + w[1]) + w[2]], []
    return _rowwise(name, fn, [x], [g, sc, sh], [(x.shape[1], BF16)], [], tm)[0]


def _rms_mod_bwd(name, x, dh, dres, g, sc, tm):
    d = x.shape[1]

    def fn(t, w):
        x_, dh_, dres_ = t
        g_, sc_ = w
        inv, xhat = _rms_parts(x_)
        dn = dh_ * (1.0 + sc_)
        dxhat = dn * g_
        dx = dres_ + inv * (dxhat - xhat * jnp.mean(dxhat * xhat, axis=-1, keepdims=True))
        return [dx], [_colsum(dh_), _colsum(dh_ * (xhat * g_)), _colsum(dn * xhat)]
    return _rowwise(name, fn, [x, dh, dres], [g, sc], [(d, F32)], [(1, d)] * 3, tm)


def _gate_bwd(name, dx, y, gate, tm):
    d = dx.shape[1]

    def fn(t, w):
        return [t[0] * w[0]], [_colsum(t[0] * t[1].astype(F32))]
    return _rowwise(name, fn, [dx, y], [gate], [(d, BF16)], [(1, d)], tm)


def _final_loss(name, x, target, g, tm):
    d = x.shape[1]

    def fn(t, w):
        inv, xhat = _rms_parts(t[0])
        err = xhat * w[0] - t[1]
        dout = err * (1.0 / d)
        dxhat = dout * w[0]
        dx = inv * (dxhat - xhat * jnp.mean(dxhat * xhat, axis=-1, keepdims=True))
        return [dx], [_colsum(err * err) * (0.5 / d), _colsum(dout * xhat)]
    return _rowwise(name, fn, [x, target], [g], [(d, F32)], [(1, d)] * 2, tm)


def _silu_rows(name, c):
    def fn(t, w):
        return [t[0] * jax.nn.sigmoid(t[0])], []
    return _rowwise(name, fn, [c], [], [(c.shape[1], F32)], [], c.shape[0])[0]


def _gelu(x):
    return 0.5 * x * (1.0 + lax.erf(x * (1.0 / math.sqrt(2.0))))


def _gelu_grad(x):
    cdf = 0.5 * (1.0 + lax.erf(x * (1.0 / math.sqrt(2.0))))
    return cdf + x * jnp.exp(-0.5 * x * x) * (1.0 / math.sqrt(2.0 * math.pi))


def _layer_norm_parts(v):
    mu = jnp.mean(v, axis=-1, keepdims=True)
    cen = v - mu
    rstd = lax.rsqrt(jnp.mean(cen * cen, axis=-1, keepdims=True) + NORM_EPS)
    return rstd, cen * rstd


def _causal(n):
    return lax.broadcasted_iota(jnp.int32, (n, n), 0) >= lax.broadcasted_iota(jnp.int32, (n, n), 1)


def _sgu_fwd(name, pre, w_s, b_s_t, ln_g, ln_b):
    t_len, d2 = pre.shape
    d = d2 // 2
    groups = w_s.shape[0]

    def body(pre_ref, w_ref, bs_ref, g_ref, b_ref, yy_ref):
        z = _gelu(pre_ref[...])
        u, v = z[:, :d], z[:, d:]
        _, vhat = _layer_norm_parts(v)
        vn = (vhat * g_ref[...] + b_ref[...]).astype(BF16)
        mask = _causal(CHUNK)
        bs = bs_ref[...]
        for g in range(groups):
            cols = slice(g * CHUNK, (g + 1) * CHUNK)
            w = jnp.where(mask, w_ref[g], 0.0).astype(BF16)
            sv = _dot(w, vn[:, cols], NN) + bs[:, g:g + 1]
            yy_ref[:, cols] = (u[:, cols] * sv).astype(BF16)

    full = lambda a: pl.BlockSpec(a.shape, lambda i, nd=a.ndim: (0,) * nd)
    return pl.pallas_call(
        body, name=name, grid=(t_len // CHUNK,),
        in_specs=[pl.BlockSpec((CHUNK, d2), lambda i: (i, 0)), full(w_s), full(b_s_t), full(ln_g), full(ln_b)],
        out_specs=pl.BlockSpec((CHUNK, d), lambda i: (i, 0)),
        out_shape=jax.ShapeDtypeStruct((t_len, d), BF16),
        compiler_params=_params("parallel"),
    )(pre, w_s, b_s_t, ln_g, ln_b)


def _sgu_bwd(name, pre, dyy, w_s, b_s_t, ln_g, ln_b):
    t_len, d2 = pre.shape
    d = d2 // 2
    groups = w_s.shape[0]

    def body(pre_ref, dyy_ref, w_ref, bs_ref, g_ref, b_ref, dpre_ref, dw_ref, dbs_ref, dg_ref, db_ref, dbin_ref, dvn_ref):
        @pl.when(pl.program_id(0) == 0)
        def _():
            for r in (dw_ref, dbs_ref, dg_ref, db_ref, dbin_ref):
                r[...] = jnp.zeros_like(r)

        pre_v = pre_ref[...]
        z = _gelu(pre_v)
        u, v = z[:, :d], z[:, d:]
        rstd, vhat = _layer_norm_parts(v)
        vn = (vhat * g_ref[...] + b_ref[...]).astype(BF16)
        mask = _causal(CHUNK)
        bs = bs_ref[...]
        lane = lax.broadcasted_iota(jnp.int32, (CHUNK, LANE), 1)
        dbs = jnp.zeros((CHUNK, LANE), F32)
        for g in range(groups):
            cols = slice(g * CHUNK, (g + 1) * CHUNK)
            w = jnp.where(mask, w_ref[g], 0.0).astype(BF16)
            sv = _dot(w, vn[:, cols], NN) + bs[:, g:g + 1]
            dyy_g = dyy_ref[:, cols]
            dpre_ref[:, cols] = (dyy_g * sv * _gelu_grad(pre_v[:, cols])).astype(BF16)
            dsv = dyy_g * u[:, cols]
            dbs = jnp.where(lane == g, jnp.sum(dsv, axis=1, keepdims=True), dbs)
            dsv_b = dsv.astype(BF16)
            dw_ref[g] += jnp.where(mask, _dot(dsv_b, vn[:, cols], NT), 0.0)
            dvn_ref[:, cols] = _dot(w, dsv_b, TN)
        dbs_ref[...] += dbs
        dvn = dvn_ref[...]
        dg_ref[...] += _colsum(dvn * vhat)
        db_ref[...] += _colsum(dvn)
        dvhat = dvn * g_ref[...]
        dv = rstd * (dvhat - jnp.mean(dvhat, axis=-1, keepdims=True)
                     - vhat * jnp.mean(dvhat * vhat, axis=-1, keepdims=True))
        dpre_ref[:, d:] = (dv * _gelu_grad(pre_v[:, d:])).astype(BF16)
        dbin_ref[...] += _colsum(dpre_ref[...].astype(F32))

    full = lambda a: pl.BlockSpec(a.shape, lambda i, nd=a.ndim: (0,) * nd)
    acc = lambda s: pl.BlockSpec(s, lambda i, nd=len(s): (0,) * nd)
    sums = [(groups, CHUNK, CHUNK), (CHUNK, LANE), (1, d), (1, d), (1, d2)]
    return pl.pallas_call(
        body, name=name, grid=(t_len // CHUNK,),
        in_specs=[pl.BlockSpec((CHUNK, d2), lambda i: (i, 0)), pl.BlockSpec((CHUNK, d), lambda i: (i, 0)),
                  full(w_s), full(b_s_t), full(ln_g), full(ln_b)],
        out_specs=[pl.BlockSpec((CHUNK, d2), lambda i: (i, 0))] + [acc(s) for s in sums],
        out_shape=[jax.ShapeDtypeStruct((t_len, d2), BF16)] + [jax.ShapeDtypeStruct(s, F32) for s in sums],
        scratch_shapes=[pltpu.VMEM((CHUNK, d), F32)],
        compiler_params=_params("arbitrary"),
    )(pre, dyy, w_s, b_s_t, ln_g, ln_b)


def _whole(rows, cols):
    return pl.BlockSpec((rows, cols), lambda i: (0, 0))


def _cum_matrix(reverse):
    r = lax.broadcasted_iota(jnp.int32, (CHUNK, CHUNK), 0)
    c = lax.broadcasted_iota(jnp.int32, (CHUNK, CHUNK), 1)
    return jnp.where((r <= c) if reverse else (r >= c), 1.0, 0.0).astype(F32)


def _forget_cumsum(name, logits, bias):
    t_len = logits.shape[0]

    def body(fl_ref, b_ref, f_ref, ft_ref):
        tri = _cum_matrix(False)

        def step(n, carry):
            off = pl.multiple_of(n * CHUNK, CHUNK)
            xv = fl_ref[pl.ds(off, CHUNK), :] + b_ref[...]
            log_f = jnp.minimum(xv, 0.0) - jnp.log1p(jnp.exp(-jnp.abs(xv)))
            cs = jnp.dot(tri, log_f, precision=lax.Precision.HIGHEST, preferred_element_type=F32) + carry
            f_ref[pl.ds(off, CHUNK), :] = cs
            ft_ref[:, pl.ds(off, CHUNK)] = cs.T
            return cs[CHUNK - 1:CHUNK, :]

        lax.fori_loop(0, t_len // CHUNK, step, jnp.zeros((1, LANE), F32))

    return pl.pallas_call(
        body, name=name, grid=(1,),
        in_specs=[_whole(t_len, LANE), _whole(1, LANE)],
        out_specs=[_whole(t_len, LANE), _whole(LANE, t_len)],
        out_shape=[jax.ShapeDtypeStruct((t_len, LANE), F32), jax.ShapeDtypeStruct((LANE, t_len), F32)],
        compiler_params=_params("arbitrary"),
    )(logits, bias)


def _forget_bwd(name, d_cum, logits, bias):
    t_len = logits.shape[0]
    n_chunks = t_len // CHUNK

    def body(dc_ref, fl_ref, b_ref, dl_ref, db_ref, run_ref):
        @pl.when(pl.program_id(0) == 0)
        def _():
            run_ref[...] = jnp.zeros_like(run_ref)
            db_ref[...] = jnp.zeros_like(db_ref)

        rc = jnp.dot(_cum_matrix(True), dc_ref[...], precision=lax.Precision.HIGHEST,
                     preferred_element_type=F32) + run_ref[0:1, :]
        dl = rc * jax.nn.sigmoid(-(fl_ref[...] + b_ref[...]))
        dl_ref[...] = dl
        db_ref[...] += _colsum(dl)
        run_ref[...] = jnp.broadcast_to(rc[0:1, :], run_ref.shape)

    back = pl.BlockSpec((CHUNK, LANE), lambda i: (n_chunks - 1 - i, 0))
    return pl.pallas_call(
        body, name=name, grid=(n_chunks,),
        in_specs=[back, back, _whole(1, LANE)],
        out_specs=[back, _whole(1, LANE)],
        out_shape=[jax.ShapeDtypeStruct((t_len, LANE), F32), jax.ShapeDtypeStruct((1, LANE), F32)],
        scratch_shapes=[pltpu.VMEM((SUBLANE, LANE), F32)],
        compiler_params=_params("arbitrary"),
    )(d_cum, logits, bias)


def _head_column(f_tile, head):
    lane = lax.broadcasted_iota(jnp.int32, f_tile.shape, 1)
    return jnp.sum(jnp.where(lane == head, f_tile, 0.0), axis=1, keepdims=True)


def _attn_fwd(name, qkv, f_cum, f_keys, heads, tq):
    t_len = qkv.shape[0]
    scale = 1.0 / math.sqrt(CHUNK)

    def body(q_ref, k_ref, v_ref, f_ref, fk_ref, o_ref, lse_ref):
        head, i = pl.program_id(0), pl.program_id(1)
        q = q_ref[...]
        fq = _head_column(f_ref[...], head)
        causal = _causal(tq)

        def keys(j):
            return pl.ds(pl.multiple_of(j * tq, tq), tq)

        def update(s_raw, j, m, l, diagonal):
            s = s_raw * scale + fq - fk_ref[:, keys(j)]
            if diagonal:
                s = jnp.where(causal, s, NEG)
            m_new = jnp.maximum(m, jnp.max(s, axis=1, keepdims=True))
            p = jnp.exp(s - m_new)
            alpha = jnp.exp(m - m_new)
            return p.astype(BF16), alpha, m_new, alpha * l + jnp.sum(p, axis=1, keepdims=True)

        def step(j, carry):
            m, l, acc, s_raw, p_prev, alpha_prev = carry
            pv = _dot(p_prev, v_ref[keys(jnp.maximum(j - 1, 0)), :], NN)
            s_next = _dot(q, k_ref[keys(j + 1), :], NT)
            p, alpha, m, l = update(s_raw, j, m, l, False)
            return m, l, alpha_prev * acc + pv, s_next, p, alpha

        init = (jnp.full((tq, 1), NEG, F32), jnp.zeros((tq, 1), F32), jnp.zeros((tq, CHUNK), F32),
                _dot(q, k_ref[keys(0), :], NT), jnp.zeros((tq, tq), BF16), jnp.ones((tq, 1), F32))
        m, l, acc, s_raw, p_prev, alpha_prev = lax.fori_loop(0, i, step, init)
        pv = _dot(p_prev, v_ref[keys(jnp.maximum(i - 1, 0)), :], NN)
        p, alpha, m, l = update(s_raw, i, m, l, True)
        acc = alpha * (alpha_prev * acc + pv) + _dot(p, v_ref[keys(i), :], NN)
        o_ref[...] = (acc / l).astype(BF16)
        lse_ref[...] = jnp.broadcast_to(m + jnp.log(l), (tq, LANE))

    return pl.pallas_call(
        body, name=name, grid=(heads, t_len // tq),
        in_specs=[pl.BlockSpec((tq, CHUNK), lambda h, i: (i, h)),
                  pl.BlockSpec((t_len, CHUNK), lambda h, i: (0, heads + h)),
                  pl.BlockSpec((t_len, CHUNK), lambda h, i: (0, 2 * heads + h)),
                  pl.BlockSpec((tq, LANE), lambda h, i: (i, 0)),
                  pl.BlockSpec((None, 1, t_len), lambda h, i: (h, 0, 0))],
        out_specs=[pl.BlockSpec((tq, CHUNK), lambda h, i: (i, h)),
                   pl.BlockSpec((None, tq, LANE), lambda h, i: (h, i, 0))],
        out_shape=[jax.ShapeDtypeStruct((t_len, heads * CHUNK), BF16),
                   jax.ShapeDtypeStruct((heads, t_len, LANE), F32)],
        compiler_params=_params("parallel", "parallel"),
    )(qkv, qkv, qkv, f_cum, f_keys)


def _attn_bwd(name, qkv, d_o, f_cum, f_keys, lse, delta, heads, tq):
    t_len = qkv.shape[0]
    d = heads * CHUNK
    n_q = t_len // tq
    scale = 1.0 / math.sqrt(CHUNK)

    def body(q_ref, k_ref, v_ref, do_ref, f_ref, fk_ref, lse_ref, dl_ref, dq_ref, dk_ref, dv_ref, dfq_ref, dfk_ref,
             dq_acc):
        head, j = pl.program_id(0), pl.program_id(1)

        @pl.when(j == 0)
        def _():
            dq_acc[...] = jnp.zeros_like(dq_acc)

        @pl.when((j == 0) & (head == 0))
        def _():
            dfq_ref[...] = jnp.zeros_like(dfq_ref)

        k, v, fk = k_ref[...], v_ref[...], fk_ref[...]
        lane = lax.broadcasted_iota(jnp.int32, (tq, LANE), 1)
        causal = _causal(tq)

        def rows(i):
            return pl.ds(pl.multiple_of(i * tq, tq), tq)

        def products(i):
            r = rows(i)
            return _dot(q_ref[r, :], k, NT), _dot(do_ref[r, :], v, NT)

        def elementwise(i, s_raw, dp, diagonal):
            r = rows(i)
            s = s_raw * scale + _head_column(f_ref[r, :], head) - fk
            if diagonal:
                s = jnp.where(causal, s, NEG)
            p = jnp.exp(s - lse_ref[r, :][:, 0:1])
            ds = p * (dp - dl_ref[r, :][:, 0:1])
            dfq_ref[r, :] += jnp.where(lane == head, jnp.sum(ds, axis=1, keepdims=True), 0.0)
            return ds.astype(BF16), p.astype(BF16), _colsum(ds)

        def flush(i, ds_b, p_b, dk, dv):
            r = rows(i)
            dq_acc[r, :] += _dot(ds_b, k, NN)
            return dk + _dot(ds_b, q_ref[r, :], TN), dv + _dot(p_b, do_ref[r, :], TN)

        def step(i, carry):
            dk, dv, dfk, ds_prev, p_prev = carry
            s_raw, dp = products(i)
            dk, dv = flush(i - 1, ds_prev, p_prev, dk, dv)
            ds_b, p_b, col = elementwise(i, s_raw, dp, False)
            return dk, dv, dfk + col, ds_b, p_b

        s_raw, dp = products(j)
        ds_b, p_b, dfk = elementwise(j, s_raw, dp, True)
        zero = jnp.zeros((tq, CHUNK), F32)
        dk, dv, dfk, ds_b, p_b = lax.fori_loop(j + 1, n_q, step, (zero, zero, dfk, ds_b, p_b))
        dk, dv = flush(n_q - 1, ds_b, p_b, dk, dv)
        dk_ref[...] = (dk * scale).astype(BF16)
        dv_ref[...] = dv.astype(BF16)
        dfk_ref[...] = dfk

        @pl.when(j == n_q - 1)
        def _():
            dq_ref[...] = (dq_acc[...] * scale).astype(BF16)

    whole_head = lambda c0: pl.BlockSpec((t_len, CHUNK), lambda h, j: (0, c0 + h))
    per_head = pl.BlockSpec((None, t_len, LANE), lambda h, j: (h, 0, 0))
    key_block = lambda c0: pl.BlockSpec((tq, CHUNK), lambda h, j: (j, c0 + h))
    return pl.pallas_call(
        body, name=name, grid=(heads, n_q),
        in_specs=[whole_head(0), key_block(heads), key_block(2 * heads), whole_head(0),
                  pl.BlockSpec((t_len, LANE), lambda h, j: (0, 0)),
                  pl.BlockSpec((None, 1, tq), lambda h, j: (h, 0, j)),
                  per_head, per_head],
        out_specs=[whole_head(0), key_block(0), key_block(0),
                   pl.BlockSpec((t_len, LANE), lambda h, j: (0, 0)),
                   pl.BlockSpec((None, 1, tq), lambda h, j: (h, 0, j))],
        out_shape=[jax.ShapeDtypeStruct((t_len, d), BF16), jax.ShapeDtypeStruct((t_len, d), BF16),
                   jax.ShapeDtypeStruct((t_len, d), BF16), jax.ShapeDtypeStruct((t_len, LANE), F32),
                   jax.ShapeDtypeStruct((heads, 1, t_len), F32)],
        scratch_shapes=[pltpu.VMEM((t_len, CHUNK), F32)],
        compiler_params=_params("arbitrary", "arbitrary"),
    )(qkv, qkv, qkv, d_o, f_cum, f_keys, lse, delta)


def _ada_grad(name, c_act_t, dmod, tm):
    d = c_act_t.shape[0]
    n_layer, n_b, n_col = dmod.shape

    def body(c_ref, dm_ref, o_ref):
        c, dm = c_ref[...], dm_ref[...]
        acc = c[:, 0:1] * dm[0:1, :]
        for b in range(1, N_DEV):
            acc = acc + c[:, b:b + 1] * dm[b:b + 1, :]
        o_ref[...] = acc

    return pl.pallas_call(
        body, name=name, grid=(n_layer, d // tm),
        in_specs=[pl.BlockSpec((tm, LANE), lambda l, i: (i, 0)),
                  pl.BlockSpec((None, n_b, n_col), lambda l, i: (l, 0, 0))],
        out_specs=pl.BlockSpec((None, tm, n_col), lambda l, i: (l, i, 0)),
        out_shape=jax.ShapeDtypeStruct((n_layer, d, n_col), F32),
        compiler_params=_params("parallel", "parallel"),
    )(c_act_t, dmod)


def _adamw(name, w, parts, m, v):
    n_layer, rows, cols = w.shape
    per_layer = isinstance(parts, (list, tuple))
    parts = list(parts) if per_layer else [parts]
    tr = _tile(rows, max(2 * SUBLANE, (1 << 19) // cols), 2 * SUBLANE)

    def body(*refs):
        w_ref, m_ref, v_ref = refs[:3]
        p_refs = refs[3:3 + len(parts)]
        g_ref, d_ref, mo_ref, vo_ref = refs[3 + len(parts):]
        layer = pl.program_id(0)
        g = None
        for n, p_ref in enumerate(p_refs):
            g_n = p_ref[0].astype(F32)
            for p in range(1, p_ref.shape[0]):
                g_n = g_n + p_ref[p].astype(F32)
            g = g_n if g is None else jnp.where(layer == n, g_n, g)
        m_new = ADAM_B1 * m_ref[...] + (1.0 - ADAM_B1) * g
        v_new = ADAM_B2 * v_ref[...] + (1.0 - ADAM_B2) * jnp.square(g)
        m_hat = m_new / (1.0 - ADAM_B1 ** ADAM_STEP)
        v_hat = v_new / (1.0 - ADAM_B2 ** ADAM_STEP)
        g_ref[...] = g
        d_ref[...] = -ADAM_LR * (m_hat / (jnp.sqrt(v_hat) + ADAM_EPS) + ADAM_WD * w_ref[...])
        mo_ref[...] = m_new
        vo_ref[...] = v_new

    blk = pl.BlockSpec((None, tr, cols), lambda l, i: (l, i, 0))
    if per_layer:
        p_specs = [pl.BlockSpec((p.shape[0], tr, cols), lambda l, i, n=n: (0, jnp.where(l == n, i, 0), 0))
                   for n, p in enumerate(parts)]
    else:
        p_specs = [pl.BlockSpec((parts[0].shape[0], None, tr, cols), lambda l, i: (0, l, i, 0))]
    return pl.pallas_call(
        body, name=name, grid=(n_layer, rows // tr),
        in_specs=[blk] * 3 + p_specs,
        out_specs=[blk] * 4,
        out_shape=[jax.ShapeDtypeStruct(w.shape, F32)] * 4,
        compiler_params=_params("parallel", "parallel"),
    )(w, m, v, *parts)


def _dev_index(p):
    return 4 * p[0] + 2 * p[1] + p[2]


def _other_chips(mx, my):
    return [(1 - mx, my), (mx, 1 - my), (1 - mx, 1 - my)]


def _all_gather_small(name, x):
    rows, cols = x.shape

    def body(x_ref, o_ref, send_sems, recv_sems):
        mx, my, mc = _place()
        me = _dev_index((mx, my, mc))
        o_ref[me] = x_ref[...]

        def copy(dist, slot, peer):
            return pltpu.make_async_remote_copy(
                src_ref=x_ref, dst_ref=o_ref.at[slot], send_sem=send_sems.at[dist - 1], recv_sem=recv_sems.at[dist - 1],
                device_id=(peer // 4, (peer // 2) % 2, peer % 2), device_id_type=MESH)

        sends = [copy(dist, me, (me + dist) % N_DEV) for dist in range(1, N_DEV)]
        for cp in sends:
            cp.start()
        for dist in range(1, N_DEV):
            src = (me + N_DEV - dist) % N_DEV
            copy(dist, src, src).wait_recv()
        for cp in sends:
            cp.wait_send()

    return pl.pallas_call(
        body, name=name,
        out_shape=jax.ShapeDtypeStruct((N_DEV, rows, cols), x.dtype),
        in_specs=[pl.BlockSpec(memory_space=pltpu.VMEM)],
        out_specs=pl.BlockSpec(memory_space=pltpu.VMEM),
        scratch_shapes=[pltpu.SemaphoreType.DMA((N_DEV - 1,)), pltpu.SemaphoreType.DMA((N_DEV - 1,))],
        compiler_params=_params(),
    )(x)


def _split_copy_call(name, body, n_in, sems, through, extra_out=(), first=True):
    sem_shapes = [pltpu.SemaphoreType.DMA((k,)) for k in sems]
    if first:
        return pl.pallas_call(
            body, name=name,
            out_shape=(*sem_shapes, *[pltpu.HBM(t.shape, t.dtype) for t in through], *extra_out),
            in_specs=[HBM] * n_in,
            out_specs=(*[SEM] * len(sems), *[HBM] * len(through), *[VMEM_SPEC] * len(extra_out)),
            input_output_aliases={i: len(sems) + i for i in range(len(through))},
            compiler_params=pltpu.CompilerParams(has_side_effects=EFFECT),
        )
    return pl.pallas_call(
        body, name=name,
        out_shape=tuple(pltpu.HBM(t.shape, t.dtype) for t in through),
        in_specs=[HBM] * len(through) + [SEM] * len(sems) + [ANY],
        out_specs=tuple([HBM] * len(through)),
        input_output_aliases={i: i for i in range(len(through))},
        compiler_params=pltpu.CompilerParams(has_side_effects=EFFECT),
    )


def _own_slot(name, shard, me, token):
    rows, cols = shard.shape
    tr = _tile(rows, max(2 * SUBLANE, (1 << 19) // cols), 2 * SUBLANE)

    def body(me_ref, x_ref, t_ref, o_ref):
        o_ref[...] = (x_ref[...] + t_ref[0:1, 0:1]).astype(BF16)

    return pl.pallas_call(
        body, name=name,
        grid_spec=pltpu.PrefetchScalarGridSpec(
            num_scalar_prefetch=1, grid=(rows // tr,),
            in_specs=[pl.BlockSpec((tr, cols), lambda i, me: (i, 0)),
                      pl.BlockSpec((SUBLANE, LANE), lambda i, me: (0, 0))],
            out_specs=pl.BlockSpec((None, tr, cols), lambda i, me: (me[0], i, 0))),
        out_shape=jax.ShapeDtypeStruct((N_DEV, rows, cols), BF16),
        compiler_params=_params("parallel"),
    )(me, shard, token)


def _gather_start(name, lands):
    n = len(lands)

    def body(*refs):
        land_refs = refs[:n]
        send_sems, recv_sems = refs[n], refs[n + 1]
        token = refs[-1]
        mx, my, mc = _place()
        for a in range(n):
            own = land_refs[a].at[_dev_index((mx, my, mc))]
            for j, chip in enumerate(_other_chips(mx, my)):
                pltpu.make_async_remote_copy(
                    src_ref=own, dst_ref=own, send_sem=send_sems.at[3 * a + j], recv_sem=recv_sems.at[3 * a + j],
                    device_id=(*chip, mc), device_id_type=MESH).start()
        token[...] = jnp.zeros_like(token)

    operands = [pltpu.with_memory_space_constraint(t, pltpu.HBM) for t in lands]
    res = _split_copy_call(name, body, n, (3 * n, 3 * n), operands,
                           extra_out=(jax.ShapeDtypeStruct((SUBLANE, LANE), F32),))(*operands)
    return res[:2], res[2:2 + n], res[-1]


def _gather_wait(name, sems, lands, after):
    n = len(lands)

    def body(*refs):
        land_refs = refs[:n]
        send_sems, recv_sems = refs[n], refs[n + 1]
        mx, my, mc = _place()
        for a in range(n):
            for j, chip in enumerate(_other_chips(mx, my)):
                copy = pltpu.make_async_remote_copy(
                    src_ref=land_refs[a].at[_dev_index((mx, my, mc))], dst_ref=land_refs[a].at[_dev_index((*chip, mc))],
                    send_sem=send_sems.at[3 * a + j], recv_sem=recv_sems.at[3 * a + j],
                    device_id=(*chip, mc), device_id_type=MESH)
                copy.wait_send()
                copy.wait_recv()

    return _split_copy_call(name, body, n, (3 * n, 3 * n), lands, first=False)(*lands, *sems, after)


def _gather_finish(name, lands):
    n = len(lands)

    def body(*refs):
        o_refs = refs[n:2 * n]
        send_sems, recv_sems = refs[2 * n:]
        mx, my, mc = _place()
        blocks = [(mx, my)] + _other_chips(mx, my)

        def copy(a, k, core):
            slot = o_refs[a].at[_dev_index((*blocks[k], core))]
            return pltpu.make_async_remote_copy(
                src_ref=slot, dst_ref=slot, send_sem=send_sems.at[N_CHIP * a + k], recv_sem=recv_sems.at[N_CHIP * a + k],
                device_id=(mx, my, 1 - mc), device_id_type=MESH)

        sends = [copy(a, k, mc) for a in range(n) for k in range(N_CHIP)]
        for cp in sends:
            cp.start()
        for a in range(n):
            for k in range(N_CHIP):
                copy(a, k, 1 - mc).wait_recv()
        for cp in sends:
            cp.wait_send()

    return pl.pallas_call(
        body, name=name,
        out_shape=[jax.ShapeDtypeStruct(t.shape, t.dtype) for t in lands],
        in_specs=[ANY] * n, out_specs=[ANY] * n,
        input_output_aliases={a: a for a in range(n)},
        scratch_shapes=[pltpu.SemaphoreType.DMA((N_CHIP * n,)), pltpu.SemaphoreType.DMA((N_CHIP * n,))],
        compiler_params=_params(),
    )(*lands)


def _exchange_sibling(name, grads):
    n = len(grads)

    def body(*refs):
        g_refs, r_refs = refs[:n], refs[n:2 * n]
        send_sems, recv_sems = refs[2 * n:]
        mx, my, mc = _place()

        def copy(a, q):
            return pltpu.make_async_remote_copy(
                src_ref=g_refs[a].at[q, 1 - mc], dst_ref=r_refs[a].at[q],
                send_sem=send_sems.at[N_CHIP * a + q], recv_sem=recv_sems.at[N_CHIP * a + q],
                device_id=(mx, my, 1 - mc), device_id_type=MESH)

        copies = [copy(a, q) for a in range(n) for q in range(N_CHIP)]
        for cp in copies:
            cp.start()
        for cp in copies:
            cp.wait_recv()
        for cp in copies:
            cp.wait_send()

    return pl.pallas_call(
        body, name=name,
        out_shape=[jax.ShapeDtypeStruct((N_CHIP, *g.shape[2:]), g.dtype) for g in grads],
        in_specs=[ANY] * n, out_specs=[ANY] * n,
        scratch_shapes=[pltpu.SemaphoreType.DMA((N_CHIP * n,)), pltpu.SemaphoreType.DMA((N_CHIP * n,))],
        compiler_params=_params(),
    )(*grads)


def _add_sibling(name, grad, recv, place):
    _, _, rows, cols = grad.shape
    tr = _tile(rows, max(2 * SUBLANE, (1 << 19) // cols), 2 * SUBLANE)

    def body(p_ref, g_ref, r_ref, o_ref, land_ref):
        total = (g_ref[...].astype(F32) + r_ref[...].astype(F32)).astype(o_ref.dtype)
        o_ref[...] = total

        @pl.when(pl.program_id(1) == p_ref[1])
        def _():
            land_ref[...] = total

    out = jax.ShapeDtypeStruct(recv.shape, recv.dtype)
    return pl.pallas_call(
        body, name=name,
        grid_spec=pltpu.PrefetchScalarGridSpec(
            num_scalar_prefetch=1, grid=(rows // tr, N_CHIP),
            in_specs=[pl.BlockSpec((None, None, tr, cols), lambda i, q, p: (q, p[0], i, 0)),
                      pl.BlockSpec((None, tr, cols), lambda i, q, p: (q, i, 0))],
            out_specs=[pl.BlockSpec((None, tr, cols), lambda i, q, p: (q, i, 0)),
                       pl.BlockSpec((None, tr, cols), lambda i, q, p: (p[1], i, 0))]),
        out_shape=[out, out],
        compiler_params=_params("parallel", "arbitrary"),
    )(place, grad, recv)


def _scatter_start(name, parts, lands):
    n = len(parts)

    def body(*refs):
        p_refs, land_refs = refs[:n], refs[n:2 * n]
        send_sems, recv_sems = refs[2 * n], refs[2 * n + 1]
        token = refs[-1]
        mx, my, mc = _place()
        for a in range(n):
            for j, chip in enumerate(_other_chips(mx, my)):
                pltpu.make_async_remote_copy(
                    src_ref=p_refs[a].at[2 * chip[0] + chip[1]], dst_ref=land_refs[a].at[2 * mx + my],
                    send_sem=send_sems.at[3 * a + j], recv_sem=recv_sems.at[3 * a + j],
                    device_id=(*chip, mc), device_id_type=MESH).start()
        token[...] = jnp.zeros_like(token)

    operands = [pltpu.with_memory_space_constraint(t, pltpu.HBM) for t in (*parts, *lands)]
    res = _split_copy_call(name, body, 2 * n, (3 * n, 3 * n), operands,
                           extra_out=(jax.ShapeDtypeStruct((SUBLANE, LANE), F32),))(*operands)
    return res[:2], res[2:2 + n], res[2 + n:2 + 2 * n], res[-1]


def _scatter_wait(name, sems, parts, lands, after):
    n = len(parts)

    def body(*refs):
        p_refs, land_refs = refs[:n], refs[n:2 * n]
        send_sems, recv_sems = refs[2 * n], refs[2 * n + 1]
        mx, my, mc = _place()
        for a in range(n):
            for j, chip in enumerate(_other_chips(mx, my)):
                copy = pltpu.make_async_remote_copy(
                    src_ref=p_refs[a].at[2 * chip[0] + chip[1]], dst_ref=land_refs[a].at[2 * chip[0] + chip[1]],
                    send_sem=send_sems.at[3 * a + j], recv_sem=recv_sems.at[3 * a + j],
                    device_id=(*chip, mc), device_id_type=MESH)
                copy.wait_send()
                copy.wait_recv()

    res = _split_copy_call(name, body, 2 * n, (3 * n, 3 * n), [*parts, *lands], first=False)(
        *parts, *lands, *sems, after)
    return res[n:]


def _reduce_scatter_start(tag, grads, place):
    wide = [g.reshape(N_CHIP, 2, *g.shape[1:]) for g in grads]
    from_sibling = _exchange_sibling(f"{tag}_reduce_sibling", wide)
    added = [_add_sibling(f"{tag}_add_sibling_{n}", g, r, place) for n, (g, r) in enumerate(zip(wide, from_sibling))]
    return _scatter_start(f"{tag}_reduce_chips_start", [p for p, _ in added], [l for _, l in added])


def _row(v):
    return v.reshape(1, -1)


def _out_proj(name, act, w_out, x, gate, tm, tn):
    t_len, d = x.shape
    k = act.shape[1]
    return _mm(name, (t_len // tm, d // tn),
               [(act, (tm, k), lambda i, j: (i, 0))], [(w_out, (k, tn), lambda i, j: (0, j))], NN,
               [((t_len, d), BF16, (tm, tn), lambda i, j: (i, j)), ((t_len, d), F32, (tm, tn), lambda i, j: (i, j))],
               epi=lambda accs, e: [accs[0], e[0] + e[1] * accs[0]],
               extras=[(x, (tm, tn), lambda i, j: (i, j)), (gate, (1, tn), lambda i, j: (0, j))])


def _proj_bwd(name, dy, w_out, dtype, tm, tn):
    t_len, d = dy.shape
    k = w_out.shape[0]
    return _mm(name, (t_len // tm, k // tn),
               [(dy, (tm, d), lambda i, j: (i, 0))], [(w_out, (tn, d), lambda i, j: (j, 0))], NT,
               [((t_len, k), dtype, (tm, tn), lambda i, j: (i, j))])[0]


def _weight_grad(name, act, dy, tm, tn):
    t_len, k = act.shape
    n = dy.shape[1]
    return _mm(name, (k // tm, n // tn),
               [(act, (t_len, tm), lambda i, j: (0, i))], [(dy, (t_len, tn), lambda i, j: (0, j))], TN,
               [((k, n), BF16, (tm, tn), lambda i, j: (i, j))])[0]


def _ffn_fwd(tag, x1, mod, g_norm, w_gate, w_up, w_down, tm):
    t_len, d = x1.shape
    fs = w_gate.shape[2]
    sh2, sc2, g2 = mod[3], mod[4], mod[5]
    h2 = _rms_mod(f"{tag}_ffn_norm", x1, g_norm, sc2, sh2, tm)
    hidden = ((N_DEV, t_len, fs), BF16, (None, tm, fs), lambda j, i: (j, i, 0))

    def swiglu(accs, _):
        a, b = accs
        return [a, b, a * jax.nn.sigmoid(a) * b]

    a, b, s = _mm(f"{tag}_ffn_up", (N_DEV, t_len // tm),
                  [(h2, (tm, d), lambda j, i: (i, 0))],
                  [(w_gate, (None, d, fs), lambda j, i: (j, 0, 0)), (w_up, (None, d, fs), lambda j, i: (j, 0, 0))],
                  NN, [hidden] * 3, epi=swiglu, summed=False)
    f, x2 = _mm(f"{tag}_ffn_down", (t_len // tm, 1, N_DEV),
                [(s, (None, tm, fs), lambda i, j, k: (k, i, 0))], [(w_down, (None, fs, d), lambda i, j, k: (k, 0, 0))],
                NN,
                [((t_len, d), BF16, (tm, d), lambda i, j, k: (i, 0)), ((t_len, d), F32, (tm, d), lambda i, j, k: (i, 0))],
                epi=lambda accs, e: [accs[0], e[0] + e[1] * accs[0]],
                extras=[(x1, (tm, d), lambda i, j, k: (i, 0)), (g2, (1, d), lambda i, j, k: (0, 0))], k_axis=2)
    return x2, (h2, a, b, s, f)


def _ffn_bwd(tag, dx2, x1, saved, mod, g_norm, w_gate, w_up, w_down, tm):
    t_len, d = x1.shape
    fs = w_gate.shape[2]
    h2, a, b, s, f = saved
    sc2, g2 = mod[4], mod[5]
    df, dg2 = _gate_bwd(f"{tag}_ffn_gate_bwd", dx2, f, g2, tm)
    hidden = ((N_DEV, t_len, fs), BF16, (None, tm, fs), lambda j, i: (j, i, 0))
    hid_in = lambda arr: (arr, (None, tm, fs), lambda j, i: (j, i, 0))

    def swiglu_bwd(accs, e):
        a_, b_ = e[0].astype(F32), e[1].astype(F32)
        sig = jax.nn.sigmoid(a_)
        return [accs[0] * b_ * sig * (1.0 + a_ * (1.0 - sig)), accs[0] * a_ * sig]

    da, db = _mm(f"{tag}_ffn_down_bwd", (N_DEV, t_len // tm),
                 [(df, (tm, d), lambda j, i: (i, 0))], [(w_down, (None, fs, d), lambda j, i: (j, 0, 0))], NT,
                 [hidden] * 2, epi=swiglu_bwd, extras=[hid_in(a), hid_in(b)])
    tn = _tile(d, 512)
    d_wd = _mm(f"{tag}_ffn_wdown_grad", (N_DEV, d // tn),
               [(s, (None, t_len, fs), lambda j, i: (j, 0, 0))], [(df, (t_len, tn), lambda j, i: (0, i))], TN,
               [((N_DEV, fs, d), BF16, (None, fs, tn), lambda j, i: (j, 0, i))])[0]
    w_grad = ((N_DEV, d, fs), BF16, (None, tn, fs), lambda j, i: (j, i, 0))
    d_wg, d_wu = _mm(f"{tag}_ffn_wup_grad", (N_DEV, d // tn),
                     [(h2, (t_len, tn), lambda j, i: (0, i))],
                     [(da, (None, t_len, fs), lambda j, i: (j, 0, 0)), (db, (None, t_len, fs), lambda j, i: (j, 0, 0))],
                     TN, [w_grad] * 2, summed=False)
    dh2 = _mm(f"{tag}_ffn_up_bwd", (t_len // tm, 1, N_DEV),
              [(da, (None, tm, fs), lambda i, j, k: (k, i, 0)), (db, (None, tm, fs), lambda i, j, k: (k, i, 0))],
              [(w_gate, (None, d, fs), lambda i, j, k: (k, 0, 0)), (w_up, (None, d, fs), lambda i, j, k: (k, 0, 0))],
              NT, [((t_len, d), F32, (tm, d), lambda i, j, k: (i, 0))], k_axis=2)[0]
    dx1, dsh2, dsc2, dgn = _rms_mod_bwd(f"{tag}_ffn_norm_bwd", x1, dh2, dx2, g_norm, sc2, tm // 2)
    return dx1, (d_wg, d_wu, d_wd), (dsh2, dsc2, dg2, dgn)


def kernel(x, c, ada_w, ada_b, norm_mix_g, norm_ffn_g, a_w_in, a_b_in, a_ln_g, a_ln_b, a_w_s, a_b_s, a_w_out, b_w_in, b_b_f, b_w_out, ffn_w_gate, ffn_w_up, ffn_w_down, final_g, loss_target, m_ada_w, m_ada_b, m_norm_mix_g, m_norm_ffn_g, m_a_w_in, m_a_b_in, m_a_ln_g, m_a_ln_b, m_a_w_s, m_a_b_s, m_a_w_out, m_b_w_in, m_b_b_f, m_b_w_out, m_ffn_w_gate, m_ffn_w_up, m_ffn_w_down, m_final_g, v_ada_w, v_ada_b, v_norm_mix_g, v_norm_ffn_g, v_a_w_in, v_a_b_in, v_a_ln_g, v_a_ln_b, v_a_w_s, v_a_b_s, v_a_w_out, v_b_w_in, v_b_b_f, v_b_w_out, v_ffn_w_gate, v_ffn_w_up, v_ffn_w_down, v_final_g):
    t_len, d = x.shape[1], x.shape[2]
    heads = d // CHUNK
    groups = a_w_s.shape[1]
    d_mod = 6 * d
    mod_cols = ada_w.shape[2]
    tm = _tile(t_len, 512)
    tn = _tile(d, 512)
    tq = _tile(t_len, 256)
    mx, my, mc = _place()
    me = _dev_index((mx, my, mc))
    x0, target = x[0], loss_target[0]

    groups_w = {"sgu": [a_w_in[0], a_w_out[0]], "ffn0": [ffn_w_gate[0], ffn_w_up[0], ffn_w_down[0]],
                "fox": [b_w_in[0], b_w_out[0]], "ffn1": [ffn_w_gate[1], ffn_w_up[1], ffn_w_down[1]]}
    c_all = _all_gather_small("gather_c", jnp.pad(c, ((0, SUBLANE - 1), (0, 0))))[:, 0, :]
    c_act = _silu_rows("silu_c", jnp.pad(c_all, ((0, 2 * SUBLANE - N_DEV), (0, 0))))
    mod_part = _mm("mod_matmul", (2, mod_cols // _tile(mod_cols, 512)),
                   [(c_act, (2 * SUBLANE, d), lambda l, j: (0, 0))],
                   [(ada_w, (None, d, _tile(mod_cols, 512)), lambda l, j: (l, 0, j))], NN,
                   [((2, 2 * SUBLANE, mod_cols), F32, (None, 2 * SUBLANE, _tile(mod_cols, 512)), lambda l, j: (l, 0, j))])[0]
    mod_all = _all_gather_small("gather_mod", mod_part[:, :N_DEV, :].reshape(2 * N_DEV, mod_cols))
    mod_mine = lax.dynamic_index_in_dim(mod_all.reshape(N_DEV, 2, N_DEV, mod_cols), me, axis=2, keepdims=False)
    mod = mod_mine.transpose(1, 0, 2).reshape(2, d_mod) + ada_b
    me_op = me.astype(jnp.int32).reshape(1)
    started, token = {}, jnp.zeros((SUBLANE, LANE), F32) + mod_all[0, 0, 0] * 0.0
    for key, group in groups_w.items():
        lands = [_own_slot(f"own_slot_{key}_{n}", s, me_op, token) for n, s in enumerate(group)]
        started[key] = _gather_start(f"gather_{key}_start", lands)
        token = started[key][2]
    mod = mod + token[0, 0]
    mods = [[_row(mod[l, k * d:(k + 1) * d]) for k in range(6)] for l in range(2)]

    def gathered(key, after):
        sems, lands, _ = started[key]
        return _gather_finish(f"gather_{key}_finish", _gather_wait(f"gather_{key}_wait", sems, lands, after))

    g_mix0, g_ffn0 = _row(norm_mix_g[0]), _row(norm_ffn_g[0])
    w_a_in, w_a_out = gathered("sgu", mod)
    w_a_in = w_a_in.transpose(1, 0, 2).reshape(d, 2 * d)
    w_a_out = w_a_out.reshape(d, d)
    na = a_w_in.shape[2]
    h_a = _rms_mod("l0_mix_norm", x0, g_mix0, mods[0][1], mods[0][0], tm)
    pre = _mm("l0_sgu_in", (t_len // tm, 2 * d // tn),
              [(h_a, (tm, d), lambda i, j: (i, 0))], [(w_a_in, (d, tn), lambda i, j: (0, j))], NN,
              [((t_len, 2 * d), F32, (tm, tn), lambda i, j: (i, j))],
              epi=lambda accs, e: [accs[0] + e[0]], extras=[(a_b_in, (1, tn), lambda i, j: (0, j))])[0]
    w_s, b_s_t = a_w_s[0], jnp.pad(a_b_s[0].T, ((0, 0), (0, LANE - groups)))
    ln_g, ln_b = a_ln_g, a_ln_b
    yy = _sgu_fwd("l0_sgu_mix", pre, w_s, b_s_t, ln_g, ln_b)
    y_a, x1 = _out_proj("l0_sgu_out", yy, w_a_out, x0, mods[0][2], tm, tn)
    w_ffn0 = gathered("ffn0", x1)
    x2, ffn0_saved = _ffn_fwd("l0", x1, mods[0], g_ffn0, *w_ffn0, tm)

    g_mix1, g_ffn1 = _row(norm_mix_g[1]), _row(norm_ffn_g[1])
    h_b = _rms_mod("l1_mix_norm", x2, g_mix1, mods[1][1], mods[1][0], tm)
    w_b_in, w_b_out = gathered("fox", h_b)
    w_b_out = w_b_out.reshape(d, d)
    w_b_full = w_b_in.transpose(1, 0, 2).reshape(d, 3 * d + heads)
    w_qkv = w_b_full[:, :3 * d]
    w_f = jnp.pad(w_b_full[:, 3 * d:], ((0, 0), (0, LANE - heads)))
    qkv = _mm("l1_qkv", (t_len // tm, 3 * d // tn),
              [(h_b, (tm, d), lambda i, j: (i, 0))], [(w_qkv, (d, tn), lambda i, j: (0, j))], NN,
              [((t_len, 3 * d), BF16, (tm, tn), lambda i, j: (i, j))])[0]
    f_logit = _mm("l1_forget_logit", (t_len // tm, 1),
                  [(h_b, (tm, d), lambda i, j: (i, 0))], [(w_f, (d, LANE), lambda i, j: (0, 0))], NN,
                  [((t_len, LANE), F32, (tm, LANE), lambda i, j: (i, 0))])[0]
    b_f = jnp.pad(b_b_f, ((0, 0), (0, LANE - heads)))
    f_cum, f_cum_t = _forget_cumsum("l1_forget_cumsum", f_logit, b_f)
    f_keys = f_cum_t[:heads].reshape(heads, 1, t_len)
    o, lse = _attn_fwd("l1_attn", qkv, f_cum, f_keys, heads, tq)
    y_b, x3 = _out_proj("l1_attn_out", o, w_b_out, x2, mods[1][2], tm, tn)
    w_ffn1 = gathered("ffn1", x3)
    x4, ffn1_saved = _ffn_fwd("l1", x3, mods[1], g_ffn1, *w_ffn1, tm)

    dx4, loss_cols, d_final_g = _final_loss("loss_head", x4, target, _row(final_g), tm // 2)
    loss = lax.psum(jnp.sum(loss_cols), ("x", "y", "c"))

    place = jnp.stack([mc, 2 * mx + my]).astype(jnp.int32)
    dx3, ffn1_grads, (dsh2_1, dsc2_1, dg2_1, dgf_1) = _ffn_bwd("l1", dx4, x3, ffn1_saved, mods[1], g_ffn1, *w_ffn1, tm)
    rs_ffn1 = _reduce_scatter_start("ffn1", ffn1_grads, place)
    dy_b, dg1_1 = _gate_bwd("l1_attn_gate_bwd", dx3, y_b, mods[1][2] + rs_ffn1[3][0, 0], tm)
    heads_tn = tn // CHUNK

    def with_delta(accs, e):
        prod = accs[0] * e[0].astype(F32)
        sums = [jnp.sum(prod[:, h * CHUNK:(h + 1) * CHUNK], axis=1, keepdims=True) for h in range(heads_tn)]
        return [accs[0], jnp.stack([jnp.broadcast_to(v, (tm, LANE)) for v in sums])]

    d_o, delta = _mm("l1_attn_out_bwd", (t_len // tm, d // tn),
                     [(dy_b, (tm, d), lambda i, j: (i, 0))], [(w_b_out, (tn, d), lambda i, j: (j, 0))], NT,
                     [((t_len, d), BF16, (tm, tn), lambda i, j: (i, j)),
                      ((heads, t_len, LANE), F32, (heads_tn, tm, LANE), lambda i, j: (j, i, 0))],
                     epi=with_delta, extras=[(o, (tm, tn), lambda i, j: (i, j))])
    d_w_b_out = _weight_grad("l1_attn_wout_grad", o, dy_b, tn, tn)
    dq, dk, dv, dfq, dfk = _attn_bwd("l1_attn_bwd", qkv, d_o, f_cum, f_keys, lse, delta, heads, tq)
    d_cum = dfq - jnp.pad(dfk.reshape(heads, t_len).T, ((0, 0), (0, LANE - heads)))
    d_logit, d_b_f = _forget_bwd("l1_forget_bwd", d_cum, f_logit, b_f)
    d_logit = d_logit.astype(BF16)
    w_grad = ((d, d), BF16, (tn, tn), lambda i, j: (i, j))
    d_w_qkv = _mm("l1_wqkv_grad", (d // tn, d // tn),
                  [(h_b, (t_len, tn), lambda i, j: (0, i))], [(g, (t_len, tn), lambda i, j: (0, j)) for g in (dq, dk, dv)],
                  TN, [w_grad] * 3, summed=False)
    d_w_f = _weight_grad("l1_wf_grad", h_b, d_logit, tn, LANE)
    dh_b = _mm("l1_qkv_bwd", (t_len // tm, d // tn),
               [(g, (tm, d), lambda i, j: (i, 0)) for g in (dq, dk, dv)] + [(d_logit, (tm, LANE), lambda i, j: (i, 0))],
               [(w_qkv, (tn, d), lambda i, j, p=p: (j, p)) for p in range(3)] + [(w_f, (tn, LANE), lambda i, j: (j, 0))], NT,
               [((t_len, d), F32, (tm, tn), lambda i, j: (i, j))])[0]
    dx2, dsh1_1, dsc1_1, dgm_1 = _rms_mod_bwd("l1_mix_norm_bwd", x2, dh_b, dx3, g_mix1, mods[1][1], tm // 2)
    qs = b_w_in.shape[2]
    d_w_b_in = jnp.concatenate([*d_w_qkv, d_w_f[:, :heads]], axis=1).reshape(d, N_DEV, qs).transpose(1, 0, 2)
    rs_fox = _reduce_scatter_start("fox", [d_w_b_in, d_w_b_out.reshape(N_DEV, d // N_DEV, d)], place)

    mods[0][5] = mods[0][5] + rs_fox[3][0, 0]
    dx1, ffn0_grads, (dsh2_0, dsc2_0, dg2_0, dgf_0) = _ffn_bwd("l0", dx2, x1, ffn0_saved, mods[0], g_ffn0, *w_ffn0, tm)
    rs_ffn0 = _reduce_scatter_start("ffn0", ffn0_grads, place)
    dy_a, dg1_0 = _gate_bwd("l0_sgu_gate_bwd", dx1, y_a, mods[0][2] + rs_ffn0[3][0, 0], tm)
    dyy = _proj_bwd("l0_sgu_out_bwd", dy_a, w_a_out, F32, tm, tn)
    d_w_a_out = _weight_grad("l0_sgu_wout_grad", yy, dy_a, tn, tn)
    dpre, d_w_s, d_b_s_t, d_ln_g, d_ln_b, d_b_in = _sgu_bwd("l0_sgu_mix_bwd", pre, dyy, w_s, b_s_t, ln_g, ln_b)
    d_w_a_in = _mm("l0_sgu_win_grad", (N_DEV, d // tn),
                   [(h_a, (t_len, tn), lambda j, i: (0, i))], [(dpre, (t_len, na), lambda j, i: (0, j))], TN,
                   [((N_DEV, d, na), BF16, (None, tn, na), lambda j, i: (j, i, 0))])[0]
    dh_a = _proj_bwd("l0_sgu_in_bwd", dpre, w_a_in, F32, tm, tn)
    dx0, dsh1_0, dsc1_0, dgm_0 = _rms_mod_bwd("l0_mix_norm_bwd", x0, dh_a, dx1, g_mix0, mods[0][1], tm // 2)

    dmod = jnp.concatenate([dsh1_0, dsc1_0, dg1_0, dsh2_0, dsc2_0, dg2_0,
                            dsh1_1, dsc1_1, dg1_1, dsh2_1, dsc2_1, dg2_1], axis=1)
    small_grads = [dmod, jnp.concatenate([dgm_0, dgm_1], axis=1), jnp.concatenate([dgf_0, dgf_1], axis=1),
                   d_b_in, d_ln_g, d_ln_b, d_w_s, d_b_s_t[:, :groups].T, d_b_f[:, :heads], d_final_g]
    small_w = [ada_b, norm_mix_g, norm_ffn_g, a_b_in, a_ln_g, a_ln_b, a_w_s, a_b_s, b_b_f, final_g]
    small_m = [m_ada_b, m_norm_mix_g, m_norm_ffn_g, m_a_b_in, m_a_ln_g, m_a_ln_b, m_a_w_s, m_a_b_s, m_b_b_f, m_final_g]
    small_v = [v_ada_b, v_norm_mix_g, v_norm_ffn_g, v_a_b_in, v_a_ln_g, v_a_ln_b, v_a_w_s, v_a_b_s, v_b_b_f, v_final_g]
    n_small = sum(w.size for w in small_w)
    pack_rows = -(-n_small // (PACK_W * SUBLANE)) * SUBLANE

    def pack(arrs):
        flat = jnp.concatenate([a.reshape(-1) for a in arrs])
        return jnp.pad(flat, (0, pack_rows * PACK_W - n_small)).reshape(pack_rows, PACK_W)

    def unpack(packed):
        flat, out, pos = packed.reshape(-1), [], 0
        for w in small_w:
            out.append(flat[pos:pos + w.size].reshape(w.shape))
            pos += w.size
        return out

    all_small = _all_gather_small("gather_small_grads", pack(small_grads))

    behind_small = d_w_a_out + (all_small[0, 0, 0] * 0.0).astype(BF16)
    rs_sgu = _reduce_scatter_start("sgu", [d_w_a_in, behind_small.reshape(N_DEV, d // N_DEV, d)], place)
    small_out = [unpack(t) for t in _adamw("adamw_small", pack(small_w)[None], all_small[:, None],
                                           pack(small_m)[None], pack(small_v)[None])]

    dmod_all = all_small.reshape(N_DEV, -1)[:, :2 * d_mod].reshape(N_DEV, 2, d_mod)
    dmod_cols = lax.dynamic_slice_in_dim(dmod_all, me * mod_cols, mod_cols, axis=2).transpose(1, 0, 2)
    c_act_t = jnp.pad(c_act[:N_DEV].T, ((0, 0), (0, LANE - N_DEV)))
    g_ada = _ada_grad("ada_w_grad", c_act_t, dmod_cols, _tile(d, 256))
    r_ada = _adamw("adamw_ada_w", ada_w, g_ada[None], m_ada_w, v_ada_w)

    def arrived(tag, started_rs, after):
        sems, parts, lands, _ = started_rs
        return _scatter_wait(f"{tag}_reduce_chips_wait", sems, parts, lands, after)

    g_gate1, g_up1, g_down1 = arrived("ffn1", rs_ffn1, rs_sgu[3])
    g_b_in, g_b_out = arrived("fox", rs_fox, rs_sgu[3])
    g_gate0, g_up0, g_down0 = arrived("ffn0", rs_ffn0, rs_sgu[3])
    r_b_in = _adamw("adamw_b_w_in", b_w_in, [g_b_in], m_b_w_in, v_b_w_in)
    r_b_out = _adamw("adamw_b_w_out", b_w_out, [g_b_out], m_b_w_out, v_b_w_out)
    r_gate = _adamw("adamw_ffn_w_gate", ffn_w_gate, [g_gate0, g_gate1], m_ffn_w_gate, v_ffn_w_gate)
    r_up = _adamw("adamw_ffn_w_up", ffn_w_up, [g_up0, g_up1], m_ffn_w_up, v_ffn_w_up)
    r_down = _adamw("adamw_ffn_w_down", ffn_w_down, [g_down0, g_down1], m_ffn_w_down, v_ffn_w_down)
    done = [r_b_in, r_b_out, r_gate, r_up, r_down, r_ada, small_out]
    g_a_in, g_a_out = arrived("sgu", rs_sgu, jnp.stack([t[0].reshape(-1)[0] for t in done[:-1]] + [small_out[0][0].reshape(-1)[0]]))
    r_a_in = _adamw("adamw_a_w_in", a_w_in, [g_a_in], m_a_w_in, v_a_w_in)
    r_a_out = _adamw("adamw_a_w_out", a_w_out, [g_a_out], m_a_w_out, v_a_w_out)

    def leaf(k):
        s = small_out[k]
        return [r_ada[k], s[0], s[1], s[2], r_a_in[k], s[3], s[4], s[5], s[6], s[7], r_a_out[k],
                r_b_in[k], s[8], r_b_out[k], r_gate[k], r_up[k], r_down[k], s[9]]

    return (loss, dx0[None], *leaf(0), *leaf(1), *leaf(2), *leaf(3))
```

```python
import functools
import math

import jax
import jax.numpy as jnp
from jax import lax
from jax.experimental import pallas as pl
from jax.experimental.pallas import tpu as pltpu

F32 = jnp.float32
BF16 = jnp.bfloat16
MESH = pl.DeviceIdType.MESH
ANY = pl.BlockSpec(memory_space=pl.ANY)
HBM = pl.BlockSpec(memory_space=pltpu.HBM)
SEM = pl.BlockSpec(memory_space=pltpu.SEMAPHORE)
VMEM_SPEC = pl.BlockSpec(memory_space=pltpu.VMEM)
EFFECT = pltpu.SideEffectType.DATAFLOW_SIDE_EFFECTING

N_DEV = 8
N_CHIP = 4
LANE = 128
SUBLANE = 8
CHUNK = 128
VMEM_LIMIT_BYTES = 52 * 1024 * 1024
NORM_EPS = 1e-6
NEG = -1e30
PACK_W = 1024

ADAM_LR = 0.001
ADAM_B1 = 0.9
ADAM_B2 = 0.999
ADAM_EPS = 1e-08
ADAM_WD = 0.01
ADAM_STEP = 10

NN = ((1,), (0,))
NT = ((1,), (1,))
TN = ((0,), (0,))


def _params(*sem):
    return pltpu.CompilerParams(dimension_semantics=sem or None, vmem_limit_bytes=VMEM_LIMIT_BYTES)


def _tile(n, target, align=LANE):
    best = None
    for t in range(align, min(n, target) + 1, align):
        if n % t == 0:
            best = t
    return best or n


def _dot(a, b, dims):
    return lax.dot_general(a, b, (dims, ((), ())), preferred_element_type=F32)


def _place():
    return lax.axis_index("x"), lax.axis_index("y"), lax.axis_index("c")


def _mm(name, grid, lhs, rhs, dims, outs, epi=None, extras=(), summed=True, k_axis=None):
    n_a, n_b, n_e, n_o = len(lhs), len(rhs), len(extras), len(outs)
    n_p = max(n_a, n_b)
    n_acc = 1 if summed else n_p
    nk = grid[k_axis] if k_axis is not None else 1
    acc_shape = tuple(d for d in outs[0][2] if d is not None)

    def body(*refs):
        a_refs, b_refs = refs[:n_a], refs[n_a:n_a + n_b]
        e_refs = refs[n_a + n_b:n_a + n_b + n_e]
        o_refs = refs[n_a + n_b + n_e:n_a + n_b + n_e + n_o]
        acc_refs = refs[n_a + n_b + n_e + n_o:]

        def products():
            a_vals = [r[...].astype(BF16) for r in a_refs]
            b_vals = [r[...].astype(BF16) for r in b_refs]
            ps = [_dot(a_vals[p % n_a], b_vals[p % n_b], dims) for p in range(n_p)]
            return [functools.reduce(lambda u, w: u + w, ps)] if summed else ps

        def finish(accs):
            res = epi(accs, [e[...] for e in e_refs]) if epi is not None else accs
            for o_ref, r in zip(o_refs, res):
                o_ref[...] = r.astype(o_ref.dtype)

        if nk == 1:
            finish(products())
        else:
            k = pl.program_id(k_axis)

            @pl.when(k == 0)
            def _():
                for acc in acc_refs:
                    acc[...] = jnp.zeros_like(acc)

            for acc, p in zip(acc_refs, products()):
                acc[...] += p

            @pl.when(k == nk - 1)
            def _():
                finish([acc[...] for acc in acc_refs])

    sem = ["parallel"] * len(grid)
    if k_axis is not None:
        sem[k_axis] = "arbitrary"
    res = pl.pallas_call(
        body, name=name, grid=grid,
        in_specs=[pl.BlockSpec(bs, im) for _, bs, im in (*lhs, *rhs, *extras)],
        out_specs=[pl.BlockSpec(bs, im) for _, _, bs, im in outs],
        out_shape=[jax.ShapeDtypeStruct(s, d) for s, d, _, _ in outs],
        scratch_shapes=[pltpu.VMEM(acc_shape, F32)] * (n_acc if nk > 1 else 0),
        compiler_params=_params(*sem),
    )(*[a for a, _, _ in (*lhs, *rhs, *extras)])
    return res


def _rowwise(name, fn, tiled, whole, out_tiled, out_sums, tm):
    rows = tiled[0].shape[0]
    n_t, n_w, n_o, n_s = len(tiled), len(whole), len(out_tiled), len(out_sums)

    def body(*refs):
        t_refs, w_refs = refs[:n_t], refs[n_t:n_t + n_w]
        o_refs = refs[n_t + n_w:n_t + n_w + n_o]
        s_refs = refs[n_t + n_w + n_o:]
        outs, sums = fn([r[...] for r in t_refs], [r[...] for r in w_refs])
        for o_ref, val in zip(o_refs, outs):
            o_ref[...] = val.astype(o_ref.dtype)

        @pl.when(pl.program_id(0) == 0)
        def _():
            for s_ref in s_refs:
                s_ref[...] = jnp.zeros_like(s_ref)

        for s_ref, val in zip(s_refs, sums):
            s_ref[...] += val

    full = lambda a: pl.BlockSpec(a.shape, lambda i, nd=a.ndim: (0,) * nd)
    res = pl.pallas_call(
        body, name=name, grid=(rows // tm,),
        in_specs=[pl.BlockSpec((tm, a.shape[1]), lambda i: (i, 0)) for a in tiled] + [full(a) for a in whole],
        out_specs=[pl.BlockSpec((tm, n), lambda i: (i, 0)) for n, _ in out_tiled]
        + [pl.BlockSpec(s, lambda i, nd=len(s): (0,) * nd) for s in out_sums],
        out_shape=[jax.ShapeDtypeStruct((rows, n), d) for n, d in out_tiled]
        + [jax.ShapeDtypeStruct(s, F32) for s in out_sums],
        compiler_params=_params("arbitrary"),
    )(*tiled, *whole)
    return res


def _colsum(v):
    return jnp.sum(v, axis=0, keepdims=True)


def _rms_parts(x):
    inv = lax.rsqrt(jnp.mean(x * x, axis=-1, keepdims=True) + NORM_EPS)
    return inv, x * inv


def _rms_mod(name, x, g, sc, sh, tm):
    def fn(t, w):
        _, xhat = _rms_parts(t[0])
        return [xhat * w[0] * (1.0 + w[1]) + w[2]], []
    return _rowwise(name, fn, [x], [g, sc, sh], [(x.shape[1], BF16)], [], tm)[0]


def _rms_mod_bwd(name, x, dh, dres, g, sc, tm):
    d = x.shape[1]

    def fn(t, w):
        x_, dh_, dres_ = t
        g_, sc_ = w
        inv, xhat = _rms_parts(x_)
        dn = dh_ * (1.0 + sc_)
        dxhat = dn * g_
        dx = dres_ + inv * (dxhat - xhat * jnp.mean(dxhat * xhat, axis=-1, keepdims=True))
        return [dx], [_colsum(dh_), _colsum(dh_ * (xhat * g_)), _colsum(dn * xhat)]
    return _rowwise(name, fn, [x, dh, dres], [g, sc], [(d, F32)], [(1, d)] * 3, tm)


def _gate_bwd(name, dx, y, gate, tm):
    d = dx.shape[1]

    def fn(t, w):
        return [t[0] * w[0]], [_colsum(t[0] * t[1].astype(F32))]
    return _rowwise(name, fn, [dx, y], [gate], [(d, BF16)], [(1, d)], tm)


def _final_loss(name, x, target, g, tm):
    d = x.shape[1]

    def fn(t, w):
        inv, xhat = _rms_parts(t[0])
        err = xhat * w[0] - t[1]
        dout = err * (1.0 / d)
        dxhat = dout * w[0]
        dx = inv * (dxhat - xhat * jnp.mean(dxhat * xhat, axis=-1, keepdims=True))
        return [dx], [_colsum(err * err) * (0.5 / d), _colsum(dout * xhat)]
    return _rowwise(name, fn, [x, target], [g], [(d, F32)], [(1, d)] * 2, tm)


def _silu_rows(name, c):
    def fn(t, w):
        return [t[0] * jax.nn.sigmoid(t[0])], []
    return _rowwise(name, fn, [c], [], [(c.shape[1], F32)], [], c.shape[0])[0]


def _gelu(x):
    return 0.5 * x * (1.0 + lax.erf(x * (1.0 / math.sqrt(2.0))))


def _gelu_grad(x):
    cdf = 0.5 * (1.0 + lax.erf(x * (1.0 / math.sqrt(2.0))))
    return cdf + x * jnp.exp(-0.5 * x * x) * (1.0 / math.sqrt(2.0 * math.pi))


def _layer_norm_parts(v):
    mu = jnp.mean(v, axis=-1, keepdims=True)
    cen = v - mu
    rstd = lax.rsqrt(jnp.mean(cen * cen, axis=-1, keepdims=True) + NORM_EPS)
    return rstd, cen * rstd


def _causal(n):
    return lax.broadcasted_iota(jnp.int32, (n, n), 0) >= lax.broadcasted_iota(jnp.int32, (n, n), 1)


def _sgu_fwd(name, pre, w_s, b_s_t, ln_g, ln_b):
    t_len, d2 = pre.shape
    d = d2 // 2
    groups = w_s.shape[0]

    def body(pre_ref, w_ref, bs_ref, g_ref, b_ref, yy_ref):
        z = _gelu(pre_ref[...])
        u, v = z[:, :d], z[:, d:]
        _, vhat = _layer_norm_parts(v)
        vn = (vhat * g_ref[...] + b_ref[...]).astype(BF16)
        mask = _causal(CHUNK)
        bs = bs_ref[...]
        for g in range(groups):
            cols = slice(g * CHUNK, (g + 1) * CHUNK)
            w = jnp.where(mask, w_ref[g], 0.0).astype(BF16)
            sv = _dot(w, vn[:, cols], NN) + bs[:, g:g + 1]
            yy_ref[:, cols] = (u[:, cols] * sv).astype(BF16)

    full = lambda a: pl.BlockSpec(a.shape, lambda i, nd=a.ndim: (0,) * nd)
    return pl.pallas_call(
        body, name=name, grid=(t_len // CHUNK,),
        in_specs=[pl.BlockSpec((CHUNK, d2), lambda i: (i, 0)), full(w_s), full(b_s_t), full(ln_g), full(ln_b)],
        out_specs=pl.BlockSpec((CHUNK, d), lambda i: (i, 0)),
        out_shape=jax.ShapeDtypeStruct((t_len, d), BF16),
        compiler_params=_params("parallel"),
    )(pre, w_s, b_s_t, ln_g, ln_b)


def _sgu_bwd(name, pre, dyy, w_s, b_s_t, ln_g, ln_b):
    t_len, d2 = pre.shape
    d = d2 // 2
    groups = w_s.shape[0]

    def body(pre_ref, dyy_ref, w_ref, bs_ref, g_ref, b_ref, dpre_ref, dw_ref, dbs_ref, dg_ref, db_ref, dbin_ref, dvn_ref):
        @pl.when(pl.program_id(0) == 0)
        def _():
            for r in (dw_ref, dbs_ref, dg_ref, db_ref, dbin_ref):
                r[...] = jnp.zeros_like(r)

        pre_v = pre_ref[...]
        z = _gelu(pre_v)
        u, v = z[:, :d], z[:, d:]
        rstd, vhat = _layer_norm_parts(v)
        vn = (vhat * g_ref[...] + b_ref[...]).astype(BF16)
        mask = _causal(CHUNK)
        bs = bs_ref[...]
        lane = lax.broadcasted_iota(jnp.int32, (CHUNK, LANE), 1)
        dbs = jnp.zeros((CHUNK, LANE), F32)
        for g in range(groups):
            cols = slice(g * CHUNK, (g + 1) * CHUNK)
            w = jnp.where(mask, w_ref[g], 0.0).astype(BF16)
            sv = _dot(w, vn[:, cols], NN) + bs[:, g:g + 1]
            dyy_g = dyy_ref[:, cols]
            dpre_ref[:, cols] = (dyy_g * sv * _gelu_grad(pre_v[:, cols])).astype(BF16)
            dsv = dyy_g * u[:, cols]
            dbs = jnp.where(lane == g, jnp.sum(dsv, axis=1, keepdims=True), dbs)
            dsv_b = dsv.astype(BF16)
            dw_ref[g] += jnp.where(mask, _dot(dsv_b, vn[:, cols], NT), 0.0)
            dvn_ref[:, cols] = _dot(w, dsv_b, TN)
        dbs_ref[...] += dbs
        dvn = dvn_ref[...]
        dg_ref[...] += _colsum(dvn * vhat)
        db_ref[...] += _colsum(dvn)
        dvhat = dvn * g_ref[...]
        dv = rstd * (dvhat - jnp.mean(dvhat, axis=-1, keepdims=True)
                     - vhat * jnp.mean(dvhat * vhat, axis=-1, keepdims=True))
        dpre_ref[:, d:] = (dv * _gelu_grad(pre_v[:, d:])).astype(BF16)
        dbin_ref[...] += _colsum(dpre_ref[...].astype(F32))

    full = lambda a: pl.BlockSpec(a.shape, lambda i, nd=a.ndim: (0,) * nd)
    acc = lambda s: pl.BlockSpec(s, lambda i, nd=len(s): (0,) * nd)
    sums = [(groups, CHUNK, CHUNK), (CHUNK, LANE), (1, d), (1, d), (1, d2)]
    return pl.pallas_call(
        body, name=name, grid=(t_len // CHUNK,),
        in_specs=[pl.BlockSpec((CHUNK, d2), lambda i: (i, 0)), pl.BlockSpec((CHUNK, d), lambda i: (i, 0)),
                  full(w_s), full(b_s_t), full(ln_g), full(ln_b)],
        out_specs=[pl.BlockSpec((CHUNK, d2), lambda i: (i, 0))] + [acc(s) for s in sums],
        out_shape=[jax.ShapeDtypeStruct((t_len, d2), BF16)] + [jax.ShapeDtypeStruct(s, F32) for s in sums],
        scratch_shapes=[pltpu.VMEM((CHUNK, d), F32)],
        compiler_params=_params("arbitrary"),
    )(pre, dyy, w_s, b_s_t, ln_g, ln_b)


def _whole(rows, cols):
    return pl.BlockSpec((rows, cols), lambda i: (0, 0))


def _cum_matrix(reverse):
    r = lax.broadcasted_iota(jnp.int32, (CHUNK, CHUNK), 0)
    c = lax.broadcasted_iota(jnp.int32, (CHUNK, CHUNK), 1)
    return jnp.where((r <= c) if reverse else (r >= c), 1.0, 0.0).astype(F32)


def _forget_cumsum(name, logits, bias):
    t_len = logits.shape[0]

    def body(fl_ref, b_ref, f_ref, ft_ref):
        tri = _cum_matrix(False)

        def step(n, carry):
            off = pl.multiple_of(n * CHUNK, CHUNK)
            xv = fl_ref[pl.ds(off, CHUNK), :] + b_ref[...]
            log_f = jnp.minimum(xv, 0.0) - jnp.log1p(jnp.exp(-jnp.abs(xv)))
            cs = jnp.dot(tri, log_f, precision=lax.Precision.HIGHEST, preferred_element_type=F32) + carry
            f_ref[pl.ds(off, CHUNK), :] = cs
            ft_ref[:, pl.ds(off, CHUNK)] = cs.T
            return cs[CHUNK - 1:CHUNK, :]

        lax.fori_loop(0, t_len // CHUNK, step, jnp.zeros((1, LANE), F32))

    return pl.pallas_call(
        body, name=name, grid=(1,),
        in_specs=[_whole(t_len, LANE), _whole(1, LANE)],
        out_specs=[_whole(t_len, LANE), _whole(LANE, t_len)],
        out_shape=[jax.ShapeDtypeStruct((t_len, LANE), F32), jax.ShapeDtypeStruct((LANE, t_len), F32)],
        compiler_params=_params("arbitrary"),
    )(logits, bias)


def _forget_bwd(name, d_cum, logits, bias):
    t_len = logits.shape[0]
    n_chunks = t_len // CHUNK

    def body(dc_ref, fl_ref, b_ref, dl_ref, db_ref, run_ref):
        @pl.when(pl.program_id(0) == 0)
        def _():
            run_ref[...] = jnp.zeros_like(run_ref)
            db_ref[...] = jnp.zeros_like(db_ref)

        rc = jnp.dot(_cum_matrix(True), dc_ref[...], precision=lax.Precision.HIGHEST,
                     preferred_element_type=F32) + run_ref[0:1, :]
        dl = rc * jax.nn.sigmoid(-(fl_ref[...] + b_ref[...]))
        dl_ref[...] = dl
        db_ref[...] += _colsum(dl)
        run_ref[...] = jnp.broadcast_to(rc[0:1, :], run_ref.shape)

    back = pl.BlockSpec((CHUNK, LANE), lambda i: (n_chunks - 1 - i, 0))
    return pl.pallas_call(
        body, name=name, grid=(n_chunks,),
        in_specs=[back, back, _whole(1, LANE)],
        out_specs=[back, _whole(1, LANE)],
        out_shape=[jax.ShapeDtypeStruct((t_len, LANE), F32), jax.ShapeDtypeStruct((1, LANE), F32)],
        scratch_shapes=[pltpu.VMEM((SUBLANE, LANE), F32)],
        compiler_params=_params("arbitrary"),
    )(d_cum, logits, bias)


def _head_column(f_tile, head):
    lane = lax.broadcasted_iota(jnp.int32, f_tile.shape, 1)
    return jnp.sum(jnp.where(lane == head, f_tile, 0.0), axis=1, keepdims=True)


def _attn_fwd(name, qkv, f_cum, f_keys, heads, tq):
    t_len = qkv.shape[0]
    scale = 1.0 / math.sqrt(CHUNK)

    def body(q_ref, k_ref, v_ref, f_ref, fk_ref, o_ref, lse_ref):
        head, i = pl.program_id(0), pl.program_id(1)
        q = q_ref[...]
        fq = _head_column(f_ref[...], head)
        causal = _causal(tq)

        def keys(j):
            return pl.ds(pl.multiple_of(j * tq, tq), tq)

        def update(s_raw, j, m, l, diagonal):
            s = s_raw * scale + fq - fk_ref[:, keys(j)]
            if diagonal:
                s = jnp.where(causal, s, NEG)
            m_new = jnp.maximum(m, jnp.max(s, axis=1, keepdims=True))
            p = jnp.exp(s - m_new)
            alpha = jnp.exp(m - m_new)
            return p.astype(BF16), alpha, m_new, alpha * l + jnp.sum(p, axis=1, keepdims=True)

        def step(j, carry):
            m, l, acc, s_raw, p_prev, alpha_prev = carry
            pv = _dot(p_prev, v_ref[keys(jnp.maximum(j - 1, 0)), :], NN)
            s_next = _dot(q, k_ref[keys(j + 1), :], NT)
            p, alpha, m, l = update(s_raw, j, m, l, False)
            return m, l, alpha_prev * acc + pv, s_next, p, alpha

        init = (jnp.full((tq, 1), NEG, F32), jnp.zeros((tq, 1), F32), jnp.zeros((tq, CHUNK), F32),
                _dot(q, k_ref[keys(0), :], NT), jnp.zeros((tq, tq), BF16), jnp.ones((tq, 1), F32))
        m, l, acc, s_raw, p_prev, alpha_prev = lax.fori_loop(0, i, step, init)
        pv = _dot(p_prev, v_ref[keys(jnp.maximum(i - 1, 0)), :], NN)
        p, alpha, m, l = update(s_raw, i, m, l, True)
        acc = alpha * (alpha_prev * acc + pv) + _dot(p, v_ref[keys(i), :], NN)
        o_ref[...] = (acc / l).astype(BF16)
        lse_ref[...] = jnp.broadcast_to(m + jnp.log(l), (tq, LANE))

    return pl.pallas_call(
        body, name=name, grid=(heads, t_len // tq),
        in_specs=[pl.BlockSpec((tq, CHUNK), lambda h, i: (i, h)),
                  pl.BlockSpec((t_len, CHUNK), lambda h, i: (0, heads + h)),
                  pl.BlockSpec((t_len, CHUNK), lambda h, i: (0, 2 * heads + h)),
                  pl.BlockSpec((tq, LANE), lambda h, i: (i, 0)),
                  pl.BlockSpec((None, 1, t_len), lambda h, i: (h, 0, 0))],
        out_specs=[pl.BlockSpec((tq, CHUNK), lambda h, i: (i, h)),
                   pl.BlockSpec((None, tq, LANE), lambda h, i: (h, i, 0))],
        out_shape=[jax.ShapeDtypeStruct((t_len, heads * CHUNK), BF16),
                   jax.ShapeDtypeStruct((heads, t_len, LANE), F32)],
        compiler_params=_params("parallel", "parallel"),
    )(qkv, qkv, qkv, f_cum, f_keys)


def _attn_bwd(name, qkv, d_o, f_cum, f_keys, lse, delta, heads, tq):
    t_len = qkv.shape[0]
    d = heads * CHUNK
    n_q = t_len // tq
    scale = 1.0 / math.sqrt(CHUNK)

    def body(q_ref, k_ref, v_ref, do_ref, f_ref, fk_ref, lse_ref, dl_ref, dq_ref, dk_ref, dv_ref, dfq_ref, dfk_ref,
             dq_acc):
        head, j = pl.program_id(0), pl.program_id(1)

        @pl.when(j == 0)
        def _():
            dq_acc[...] = jnp.zeros_like(dq_acc)

        @pl.when((j == 0) & (head == 0))
        def _():
            dfq_ref[...] = jnp.zeros_like(dfq_ref)

        k, v, fk = k_ref[...], v_ref[...], fk_ref[...]
        lane = lax.broadcasted_iota(jnp.int32, (tq, LANE), 1)
        causal = _causal(tq)

        def rows(i):
            return pl.ds(pl.multiple_of(i * tq, tq), tq)

        def products(i):
            r = rows(i)
            return _dot(q_ref[r, :], k, NT), _dot(do_ref[r, :], v, NT)

        def elementwise(i, s_raw, dp, diagonal):
            r = rows(i)
            s = s_raw * scale + _head_column(f_ref[r, :], head) - fk
            if diagonal:
                s = jnp.where(causal, s, NEG)
            p = jnp.exp(s - lse_ref[r, :][:, 0:1])
            ds = p * (dp - dl_ref[r, :][:, 0:1])
            dfq_ref[r, :] += jnp.where(lane == head, jnp.sum(ds, axis=1, keepdims=True), 0.0)
            return ds.astype(BF16), p.astype(BF16), _colsum(ds)

        def flush(i, ds_b, p_b, dk, dv):
            r = rows(i)
            dq_acc[r, :] += _dot(ds_b, k, NN)
            return dk + _dot(ds_b, q_ref[r, :], TN), dv + _dot(p_b, do_ref[r, :], TN)

        def step(i, carry):
            dk, dv, dfk, ds_prev, p_prev = carry
            s_raw, dp = products(i)
            dk, dv = flush(i - 1, ds_prev, p_prev, dk, dv)
            ds_b, p_b, col = elementwise(i, s_raw, dp, False)
            return dk, dv, dfk + col, ds_b, p_b

        s_raw, dp = products(j)
        ds_b, p_b, dfk = elementwise(j, s_raw, dp, True)
        zero = jnp.zeros((tq, CHUNK), F32)
        dk, dv, dfk, ds_b, p_b = lax.fori_loop(j + 1, n_q, step, (zero, zero, dfk, ds_b, p_b))
        dk, dv = flush(n_q - 1, ds_b, p_b, dk, dv)
        dk_ref[...] = (dk * scale).astype(BF16)
        dv_ref[...] = dv.astype(BF16)
        dfk_ref[...] = dfk

        @pl.when(j == n_q - 1)
        def _():
            dq_ref[...] = (dq_acc[...] * scale).astype(BF16)

    whole_head = lambda c0: pl.BlockSpec((t_len, CHUNK), lambda h, j: (0, c0 + h))
    per_head = pl.BlockSpec((None, t_len, LANE), lambda h, j: (h, 0, 0))
    key_block = lambda c0: pl.BlockSpec((tq, CHUNK), lambda h, j: (j, c0 + h))
    return pl.pallas_call(
        body, name=name, grid=(heads, n_q),
        in_specs=[whole_head(0), key_block(heads), key_block(2 * heads), whole_head(0),
                  pl.BlockSpec((t_len, LANE), lambda h, j: (0, 0)),
                  pl.BlockSpec((None, 1, tq), lambda h, j: (h, 0, j)),
                  per_head, per_head],
        out_specs=[whole_head(0), key_block(0), key_block(0),
                   pl.BlockSpec((t_len, LANE), lambda h, j: (0, 0)),
                   pl.BlockSpec((None, 1, tq), lambda h, j: (h, 0, j))],
        out_shape=[jax.ShapeDtypeStruct((t_len, d), BF16), jax.ShapeDtypeStruct((t_len, d), BF16),
                   jax.ShapeDtypeStruct((t_len, d), BF16), jax.ShapeDtypeStruct((t_len, LANE), F32),
                   jax.ShapeDtypeStruct((heads, 1, t_len), F32)],
        scratch_shapes=[pltpu.VMEM((t_len, CHUNK), F32)],
        compiler_params=_params("arbitrary", "arbitrary"),
    )(qkv, qkv, qkv, d_o, f_cum, f_keys, lse, delta)


def _ada_grad(name, c_act_t, dmod, tm):
    d = c_act_t.shape[0]
    n_layer, n_b, n_col = dmod.shape

    def body(c_ref, dm_ref, o_ref):
        c, dm = c_ref[...], dm_ref[...]
        acc = c[:, 0:1] * dm[0:1, :]
        for b in range(1, N_DEV):
            acc = acc + c[:, b:b + 1] * dm[b:b + 1, :]
        o_ref[...] = acc

    return pl.pallas_call(
        body, name=name, grid=(n_layer, d // tm),
        in_specs=[pl.BlockSpec((tm, LANE), lambda l, i: (i, 0)),
                  pl.BlockSpec((None, n_b, n_col), lambda l, i: (l, 0, 0))],
        out_specs=pl.BlockSpec((None, tm, n_col), lambda l, i: (l, i, 0)),
        out_shape=jax.ShapeDtypeStruct((n_layer, d, n_col), F32),
        compiler_params=_params("parallel", "parallel"),
    )(c_act_t, dmod)


def _adamw(name, w, parts, m, v):
    n_layer, rows, cols = w.shape
    per_layer = isinstance(parts, (list, tuple))
    parts = list(parts) if per_layer else [parts]
    by_rows = rows % (2 * SUBLANE) == 0
    tr = _tile(rows, max(2 * SUBLANE, (1 << 19) // cols), 2 * SUBLANE) if by_rows else rows
    tc = cols if by_rows else _tile(cols, max(LANE, (1 << 19) // rows))
    n_tiles = rows // tr if by_rows else cols // tc
    at = (lambda i: (i, 0)) if by_rows else (lambda i: (0, i))

    def body(*refs):
        w_ref, m_ref, v_ref = refs[:3]
        p_refs = refs[3:3 + len(parts)]
        g_ref, d_ref, mo_ref, vo_ref = refs[3 + len(parts):]
        layer = pl.program_id(0)
        g = None
        for n, p_ref in enumerate(p_refs):
            g_n = p_ref[0].astype(F32)
            for p in range(1, p_ref.shape[0]):
                g_n = g_n + p_ref[p].astype(F32)
            g = g_n if g is None else jnp.where(layer == n, g_n, g)
        m_new = ADAM_B1 * m_ref[...] + (1.0 - ADAM_B1) * g
        v_new = ADAM_B2 * v_ref[...] + (1.0 - ADAM_B2) * jnp.square(g)
        m_hat = m_new / (1.0 - ADAM_B1 ** ADAM_STEP)
        v_hat = v_new / (1.0 - ADAM_B2 ** ADAM_STEP)
        g_ref[...] = g
        d_ref[...] = -ADAM_LR * (m_hat / (jnp.sqrt(v_hat) + ADAM_EPS) + ADAM_WD * w_ref[...])
        mo_ref[...] = m_new
        vo_ref[...] = v_new

    blk = pl.BlockSpec((None, tr, tc), lambda l, i: (l, *at(i)))
    if per_layer:
        p_specs = [pl.BlockSpec((p.shape[0], tr, tc), lambda l, i, n=n: (0, *at(jnp.where(l == n, i, 0))))
                   for n, p in enumerate(parts)]
    else:
        p_specs = [pl.BlockSpec((parts[0].shape[0], None, tr, tc), lambda l, i: (0, l, *at(i)))]
    return pl.pallas_call(
        body, name=name, grid=(n_layer, n_tiles),
        in_specs=[blk] * 3 + p_specs,
        out_specs=[blk] * 4,
        out_shape=[jax.ShapeDtypeStruct(w.shape, F32)] * 4,
        compiler_params=_params("parallel", "parallel"),
    )(w, m, v, *parts)


def _dev_index(p):
    return 4 * p[0] + 2 * p[1] + p[2]


def _other_chips(mx, my):
    return [(1 - mx, my), (mx, 1 - my), (1 - mx, 1 - my)]


def _all_gather_small(name, x):
    rows, cols = x.shape

    def body(x_ref, o_ref, send_sems, recv_sems):
        mx, my, mc = _place()
        me = _dev_index((mx, my, mc))
        o_ref[me] = x_ref[...]

        def copy(dist, slot, peer):
            return pltpu.make_async_remote_copy(
                src_ref=x_ref, dst_ref=o_ref.at[slot], send_sem=send_sems.at[dist - 1], recv_sem=recv_sems.at[dist - 1],
                device_id=(peer // 4, (peer // 2) % 2, peer % 2), device_id_type=MESH)

        sends = [copy(dist, me, (me + dist) % N_DEV) for dist in range(1, N_DEV)]
        for cp in sends:
            cp.start()
        for dist in range(1, N_DEV):
            src = (me + N_DEV - dist) % N_DEV
            copy(dist, src, src).wait_recv()
        for cp in sends:
            cp.wait_send()

    return pl.pallas_call(
        body, name=name,
        out_shape=jax.ShapeDtypeStruct((N_DEV, rows, cols), x.dtype),
        in_specs=[pl.BlockSpec(memory_space=pltpu.VMEM)],
        out_specs=pl.BlockSpec(memory_space=pltpu.VMEM),
        scratch_shapes=[pltpu.SemaphoreType.DMA((N_DEV - 1,)), pltpu.SemaphoreType.DMA((N_DEV - 1,))],
        compiler_params=_params(),
    )(x)


def _split_copy_call(name, body, n_in, sems, through, extra_out=(), first=True):
    sem_shapes = [pltpu.SemaphoreType.DMA((k,)) for k in sems]
    if first:
        return pl.pallas_call(
            body, name=name,
            out_shape=(*sem_shapes, *[pltpu.HBM(t.shape, t.dtype) for t in through], *extra_out),
            in_specs=[HBM] * n_in,
            out_specs=(*[SEM] * len(sems), *[HBM] * len(through), *[VMEM_SPEC] * len(extra_out)),
            input_output_aliases={i: len(sems) + i for i in range(len(through))},
            compiler_params=pltpu.CompilerParams(has_side_effects=EFFECT),
        )
    return pl.pallas_call(
        body, name=name,
        out_shape=tuple(pltpu.HBM(t.shape, t.dtype) for t in through),
        in_specs=[HBM] * len(through) + [SEM] * len(sems) + [ANY],
        out_specs=tuple([HBM] * len(through)),
        input_output_aliases={i: i for i in range(len(through))},
        compiler_params=pltpu.CompilerParams(has_side_effects=EFFECT),
    )


def _own_slot(name, shard, me, token):
    rows, cols = shard.shape
    tr = _tile(rows, max(2 * SUBLANE, (1 << 19) // cols), 2 * SUBLANE)

    def body(me_ref, x_ref, t_ref, o_ref):
        o_ref[...] = (x_ref[...] + t_ref[0:1, 0:1]).astype(BF16)

    return pl.pallas_call(
        body, name=name,
        grid_spec=pltpu.PrefetchScalarGridSpec(
            num_scalar_prefetch=1, grid=(rows // tr,),
            in_specs=[pl.BlockSpec((tr, cols), lambda i, me: (i, 0)),
                      pl.BlockSpec((SUBLANE, LANE), lambda i, me: (0, 0))],
            out_specs=pl.BlockSpec((None, tr, cols), lambda i, me: (me[0], i, 0))),
        out_shape=jax.ShapeDtypeStruct((N_DEV, rows, cols), BF16),
        compiler_params=_params("parallel"),
    )(me, shard, token)


def _gather_start(name, lands):
    n = len(lands)

    def body(*refs):
        land_refs = refs[:n]
        send_sems, recv_sems = refs[n], refs[n + 1]
        token = refs[-1]
        mx, my, mc = _place()
        for a in range(n):
            own = land_refs[a].at[_dev_index((mx, my, mc))]
            for j, chip in enumerate(_other_chips(mx, my)):
                pltpu.make_async_remote_copy(
                    src_ref=own, dst_ref=own, send_sem=send_sems.at[3 * a + j], recv_sem=recv_sems.at[3 * a + j],
                    device_id=(*chip, mc), device_id_type=MESH).start()
        token[...] = jnp.zeros_like(token)

    operands = [pltpu.with_memory_space_constraint(t, pltpu.HBM) for t in lands]
    res = _split_copy_call(name, body, n, (3 * n, 3 * n), operands,
                           extra_out=(jax.ShapeDtypeStruct((SUBLANE, LANE), F32),))(*operands)
    return res[:2], res[2:2 + n], res[-1]


def _gather_wait(name, sems, lands, after):
    n = len(lands)

    def body(*refs):
        land_refs = refs[:n]
        send_sems, recv_sems = refs[n], refs[n + 1]
        mx, my, mc = _place()
        for a in range(n):
            for j, chip in enumerate(_other_chips(mx, my)):
                copy = pltpu.make_async_remote_copy(
                    src_ref=land_refs[a].at[_dev_index((mx, my, mc))], dst_ref=land_refs[a].at[_dev_index((*chip, mc))],
                    send_sem=send_sems.at[3 * a + j], recv_sem=recv_sems.at[3 * a + j],
                    device_id=(*chip, mc), device_id_type=MESH)
                copy.wait_send()
                copy.wait_recv()

    return _split_copy_call(name, body, n, (3 * n, 3 * n), lands, first=False)(*lands, *sems, after)


def _gather_finish(name, lands):
    n = len(lands)

    def body(*refs):
        o_refs = refs[n:2 * n]
        send_sems, recv_sems = refs[2 * n:]
        mx, my, mc = _place()
        blocks = [(mx, my)] + _other_chips(mx, my)

        def copy(a, k, core):
            slot = o_refs[a].at[_dev_index((*blocks[k], core))]
            return pltpu.make_async_remote_copy(
                src_ref=slot, dst_ref=slot, send_sem=send_sems.at[N_CHIP * a + k], recv_sem=recv_sems.at[N_CHIP * a + k],
                device_id=(mx, my, 1 - mc), device_id_type=MESH)

        sends = [copy(a, k, mc) for a in range(n) for k in range(N_CHIP)]
        for cp in sends:
            cp.start()
        for a in range(n):
            for k in range(N_CHIP):
                copy(a, k, 1 - mc).wait_recv()
        for cp in sends:
            cp.wait_send()

    return pl.pallas_call(
        body, name=name,
        out_shape=[jax.ShapeDtypeStruct(t.shape, t.dtype) for t in lands],
        in_specs=[ANY] * n, out_specs=[ANY] * n,
        input_output_aliases={a: a for a in range(n)},
        scratch_shapes=[pltpu.SemaphoreType.DMA((N_CHIP * n,)), pltpu.SemaphoreType.DMA((N_CHIP * n,))],
        compiler_params=_params(),
    )(*lands)


def _exchange_sibling(name, grads):
    n = len(grads)

    def body(*refs):
        g_refs, r_refs = refs[:n], refs[n:2 * n]
        send_sems, recv_sems = refs[2 * n:]
        mx, my, mc = _place()

        def copy(a, q):
            return pltpu.make_async_remote_copy(
                src_ref=g_refs[a].at[q, 1 - mc], dst_ref=r_refs[a].at[q],
                send_sem=send_sems.at[N_CHIP * a + q], recv_sem=recv_sems.at[N_CHIP * a + q],
                device_id=(mx, my, 1 - mc), device_id_type=MESH)

        copies = [copy(a, q) for a in range(n) for q in range(N_CHIP)]
        for cp in copies:
            cp.start()
        for cp in copies:
            cp.wait_recv()
        for cp in copies:
            cp.wait_send()

    return pl.pallas_call(
        body, name=name,
        out_shape=[jax.ShapeDtypeStruct((N_CHIP, *g.shape[2:]), g.dtype) for g in grads],
        in_specs=[ANY] * n, out_specs=[ANY] * n,
        scratch_shapes=[pltpu.SemaphoreType.DMA((N_CHIP * n,)), pltpu.SemaphoreType.DMA((N_CHIP * n,))],
        compiler_params=_params(),
    )(*grads)


def _add_sibling(name, grad, recv, place):
    _, _, rows, cols = grad.shape
    tr = _tile(rows, max(2 * SUBLANE, (1 << 19) // cols), 2 * SUBLANE)

    def body(p_ref, g_ref, r_ref, o_ref, land_ref):
        total = (g_ref[...].astype(F32) + r_ref[...].astype(F32)).astype(o_ref.dtype)
        o_ref[...] = total

        @pl.when(pl.program_id(1) == p_ref[1])
        def _():
            land_ref[...] = total

    out = jax.ShapeDtypeStruct(recv.shape, recv.dtype)
    return pl.pallas_call(
        body, name=name,
        grid_spec=pltpu.PrefetchScalarGridSpec(
            num_scalar_prefetch=1, grid=(rows // tr, N_CHIP),
            in_specs=[pl.BlockSpec((None, None, tr, cols), lambda i, q, p: (q, p[0], i, 0)),
                      pl.BlockSpec((None, tr, cols), lambda i, q, p: (q, i, 0))],
            out_specs=[pl.BlockSpec((None, tr, cols), lambda i, q, p: (q, i, 0)),
                       pl.BlockSpec((None, tr, cols), lambda i, q, p: (p[1], i, 0))]),
        out_shape=[out, out],
        compiler_params=_params("parallel", "arbitrary"),
    )(place, grad, recv)


def _scatter_start(name, parts, lands):
    n = len(parts)

    def body(*refs):
        p_refs, land_refs = refs[:n], refs[n:2 * n]
        send_sems, recv_sems = refs[2 * n], refs[2 * n + 1]
        token = refs[-1]
        mx, my, mc = _place()
        for a in range(n):
            for j, chip in enumerate(_other_chips(mx, my)):
                pltpu.make_async_remote_copy(
                    src_ref=p_refs[a].at[2 * chip[0] + chip[1]], dst_ref=land_refs[a].at[2 * mx + my],
                    send_sem=send_sems.at[3 * a + j], recv_sem=recv_sems.at[3 * a + j],
                    device_id=(*chip, mc), device_id_type=MESH).start()
        token[...] = jnp.zeros_like(token)

    operands = [pltpu.with_memory_space_constraint(t, pltpu.HBM) for t in (*parts, *lands)]
    res = _split_copy_call(name, body, 2 * n, (3 * n, 3 * n), operands,
                           extra_out=(jax.ShapeDtypeStruct((SUBLANE, LANE), F32),))(*operands)
    return res[:2], res[2:2 + n], res[2 + n:2 + 2 * n], res[-1]


def _scatter_wait(name, sems, parts, lands, after):
    n = len(parts)

    def body(*refs):
        p_refs, land_refs = refs[:n], refs[n:2 * n]
        send_sems, recv_sems = refs[2 * n], refs[2 * n + 1]
        mx, my, mc = _place()
        for a in range(n):
            for j, chip in enumerate(_other_chips(mx, my)):
                copy = pltpu.make_async_remote_copy(
                    src_ref=p_refs[a].at[2 * chip[0] + chip[1]], dst_ref=land_refs[a].at[2 * chip[0] + chip[1]],
                    send_sem=send_sems.at[3 * a + j], recv_sem=recv_sems.at[3 * a + j],
                    device_id=(*chip, mc), device_id_type=MESH)
                copy.wait_send()
                copy.wait_recv()

    res = _split_copy_call(name, body, 2 * n, (3 * n, 3 * n), [*parts, *lands], first=False)(
        *parts, *lands, *sems, after)
    return res[n:]


def _reduce_scatter_start(tag, grads, place):
    wide = [g.reshape(N_CHIP, 2, *g.shape[1:]) for g in grads]
    from_sibling = _exchange_sibling(f"{tag}_reduce_sibling", wide)
    added = [_add_sibling(f"{tag}_add_sibling_{n}", g, r, place) for n, (g, r) in enumerate(zip(wide, from_sibling))]
    return _scatter_start(f"{tag}_reduce_chips_start", [p for p, _ in added], [l for _, l in added])


def _row(v):
    return v.reshape(1, -1)


def _out_proj(name, act, w_out, x, gate, tm, tn):
    t_len, d = x.shape
    k = act.shape[1]
    return _mm(name, (t_len // tm, d // tn),
               [(act, (tm, k), lambda i, j: (i, 0))], [(w_out, (k, tn), lambda i, j: (0, j))], NN,
               [((t_len, d), BF16, (tm, tn), lambda i, j: (i, j)), ((t_len, d), F32, (tm, tn), lambda i, j: (i, j))],
               epi=lambda accs, e: [accs[0], e[0] + e[1] * accs[0]],
               extras=[(x, (tm, tn), lambda i, j: (i, j)), (gate, (1, tn), lambda i, j: (0, j))])


def _proj_bwd(name, dy, w_out, dtype, tm, tn):
    t_len, d = dy.shape
    k = w_out.shape[0]
    return _mm(name, (t_len // tm, k // tn),
               [(dy, (tm, d), lambda i, j: (i, 0))], [(w_out, (tn, d), lambda i, j: (j, 0))], NT,
               [((t_len, k), dtype, (tm, tn), lambda i, j: (i, j))])[0]


def _weight_grad(name, act, dy, tm, tn):
    t_len, k = act.shape
    n = dy.shape[1]
    return _mm(name, (k // tm, n // tn),
               [(act, (t_len, tm), lambda i, j: (0, i))], [(dy, (t_len, tn), lambda i, j: (0, j))], TN,
               [((k, n), BF16, (tm, tn), lambda i, j: (i, j))])[0]


def _ffn_fwd(tag, x1, mod, g_norm, w_gate, w_up, w_down, tm):
    t_len, d = x1.shape
    fs = w_gate.shape[1]
    sh2, sc2, g2 = mod[3], mod[4], mod[5]
    h2 = _rms_mod(f"{tag}_ffn_norm", x1, g_norm, sc2, sh2, tm)
    hidden = ((N_DEV, t_len, fs), BF16, (None, tm, fs), lambda j, i: (j, i, 0))

    def swiglu(accs, _):
        a, b = accs
        return [a, b, a * jax.nn.sigmoid(a) * b]

    a, b, s = _mm(f"{tag}_ffn_up", (N_DEV, t_len // tm),
                  [(h2, (tm, d), lambda j, i: (i, 0))],
                  [(w_gate, (None, fs, d), lambda j, i: (j, 0, 0)), (w_up, (None, fs, d), lambda j, i: (j, 0, 0))],
                  NT, [hidden] * 3, epi=swiglu, summed=False)
    f, x2 = _mm(f"{tag}_ffn_down", (t_len // tm, 1, N_DEV),
                [(s, (None, tm, fs), lambda i, j, k: (k, i, 0))], [(w_down, (None, fs, d), lambda i, j, k: (k, 0, 0))],
                NN,
                [((t_len, d), BF16, (tm, d), lambda i, j, k: (i, 0)), ((t_len, d), F32, (tm, d), lambda i, j, k: (i, 0))],
                epi=lambda accs, e: [accs[0], e[0] + e[1] * accs[0]],
                extras=[(x1, (tm, d), lambda i, j, k: (i, 0)), (g2, (1, d), lambda i, j, k: (0, 0))], k_axis=2)
    return x2, (h2, a, b, s, f)


def _ffn_bwd(tag, dx2, x1, saved, mod, g_norm, w_gate, w_up, w_down, tm):
    t_len, d = x1.shape
    fs = w_gate.shape[1]
    h2, a, b, s, f = saved
    sc2, g2 = mod[4], mod[5]
    df, dg2 = _gate_bwd(f"{tag}_ffn_gate_bwd", dx2, f, g2, tm)
    hidden = ((N_DEV, t_len, fs), BF16, (None, tm, fs), lambda j, i: (j, i, 0))
    hid_in = lambda arr: (arr, (None, tm, fs), lambda j, i: (j, i, 0))

    def swiglu_bwd(accs, e):
        a_, b_ = e[0].astype(F32), e[1].astype(F32)
        sig = jax.nn.sigmoid(a_)
        return [accs[0] * b_ * sig * (1.0 + a_ * (1.0 - sig)), accs[0] * a_ * sig]

    da, db = _mm(f"{tag}_ffn_down_bwd", (N_DEV, t_len // tm),
                 [(df, (tm, d), lambda j, i: (i, 0))], [(w_down, (None, fs, d), lambda j, i: (j, 0, 0))], NT,
                 [hidden] * 2, epi=swiglu_bwd, extras=[hid_in(a), hid_in(b)])
    tn = _tile(d, 512)
    d_wd = _mm(f"{tag}_ffn_wdown_grad", (N_DEV, d // tn),
               [(s, (None, t_len, fs), lambda j, i: (j, 0, 0))], [(df, (t_len, tn), lambda j, i: (0, i))], TN,
               [((N_DEV, fs, d), BF16, (None, fs, tn), lambda j, i: (j, 0, i))])[0]
    w_grad = ((N_DEV, fs, d), BF16, (None, fs, tn), lambda j, i: (j, 0, i))
    d_wg, d_wu = _mm(f"{tag}_ffn_wup_grad", (N_DEV, d // tn),
                     [(da, (None, t_len, fs), lambda j, i: (j, 0, 0)), (db, (None, t_len, fs), lambda j, i: (j, 0, 0))],
                     [(h2, (t_len, tn), lambda j, i: (0, i))],
                     TN, [w_grad] * 2, summed=False)
    dh2 = _mm(f"{tag}_ffn_up_bwd", (t_len // tm, 1, N_DEV),
              [(da, (None, tm, fs), lambda i, j, k: (k, i, 0)), (db, (None, tm, fs), lambda i, j, k: (k, i, 0))],
              [(w_gate, (None, fs, d), lambda i, j, k: (k, 0, 0)), (w_up, (None, fs, d), lambda i, j, k: (k, 0, 0))],
              NN, [((t_len, d), F32, (tm, d), lambda i, j, k: (i, 0))], k_axis=2)[0]
    dx1, dsh2, dsc2, dgn = _rms_mod_bwd(f"{tag}_ffn_norm_bwd", x1, dh2, dx2, g_norm, sc2, tm // 2)
    return dx1, (d_wg, d_wu, d_wd), (dsh2, dsc2, dg2, dgn)


def kernel(x, c, ada_w, ada_b, norm_mix_g, norm_ffn_g, a_w_in, a_b_in, a_ln_g, a_ln_b, a_w_s, a_b_s, a_w_out, b_w_in, b_b_f, b_w_out, ffn_w_gate, ffn_w_up, ffn_w_down, final_g, loss_target, m_ada_w, m_ada_b, m_norm_mix_g, m_norm_ffn_g, m_a_w_in, m_a_b_in, m_a_ln_g, m_a_ln_b, m_a_w_s, m_a_b_s, m_a_w_out, m_b_w_in, m_b_b_f, m_b_w_out, m_ffn_w_gate, m_ffn_w_up, m_ffn_w_down, m_final_g, v_ada_w, v_ada_b, v_norm_mix_g, v_norm_ffn_g, v_a_w_in, v_a_b_in, v_a_ln_g, v_a_ln_b, v_a_w_s, v_a_b_s, v_a_w_out, v_b_w_in, v_b_b_f, v_b_w_out, v_ffn_w_gate, v_ffn_w_up, v_ffn_w_down, v_final_g):
    t_len, d = x.shape[1], x.shape[2]
    heads = d // CHUNK
    groups = a_w_s.shape[1]
    d_mod = 6 * d
    mod_cols = ada_w.shape[2]
    tm = _tile(t_len, 512)
    tn = _tile(d, 512)
    tq = _tile(t_len, 256)
    mx, my, mc = _place()
    me = _dev_index((mx, my, mc))
    x0, target = x[0], loss_target[0]

    t_last = lambda arrs: [a.transpose(0, 2, 1) for a in arrs]
    gate_t, m_gate_t, v_gate_t = t_last([ffn_w_gate, m_ffn_w_gate, v_ffn_w_gate])
    up_t, m_up_t, v_up_t = t_last([ffn_w_up, m_ffn_w_up, v_ffn_w_up])
    b_in_t, m_b_in_t, v_b_in_t = t_last([b_w_in, m_b_w_in, v_b_w_in])
    groups_w = {"sgu": [a_w_in[0], a_w_out[0]], "ffn0": [gate_t[0], up_t[0], ffn_w_down[0]],
                "fox": [b_in_t[0], b_w_out[0]], "ffn1": [gate_t[1], up_t[1], ffn_w_down[1]]}
    c_all = _all_gather_small("gather_c", jnp.pad(c, ((0, SUBLANE - 1), (0, 0))))[:, 0, :]
    c_act = _silu_rows("silu_c", jnp.pad(c_all, ((0, 2 * SUBLANE - N_DEV), (0, 0))))
    mod_part = _mm("mod_matmul", (2, mod_cols // _tile(mod_cols, 512)),
                   [(c_act, (2 * SUBLANE, d), lambda l, j: (0, 0))],
                   [(ada_w, (None, d, _tile(mod_cols, 512)), lambda l, j: (l, 0, j))], NN,
                   [((2, 2 * SUBLANE, mod_cols), F32, (None, 2 * SUBLANE, _tile(mod_cols, 512)), lambda l, j: (l, 0, j))])[0]
    mod_all = _all_gather_small("gather_mod", mod_part[:, :N_DEV, :].reshape(2 * N_DEV, mod_cols))
    mod_mine = lax.dynamic_index_in_dim(mod_all.reshape(N_DEV, 2, N_DEV, mod_cols), me, axis=2, keepdims=False)
    mod = mod_mine.transpose(1, 0, 2).reshape(2, d_mod) + ada_b
    me_op = me.astype(jnp.int32).reshape(1)
    started, token = {}, jnp.zeros((SUBLANE, LANE), F32) + mod_all[0, 0, 0] * 0.0
    for key, group in groups_w.items():
        lands = [_own_slot(f"own_slot_{key}_{n}", s, me_op, token) for n, s in enumerate(group)]
        started[key] = _gather_start(f"gather_{key}_start", lands)
        token = started[key][2]
    mod = mod + token[0, 0]
    mods = [[_row(mod[l, k * d:(k + 1) * d]) for k in range(6)] for l in range(2)]

    def gathered(key, after):
        sems, lands, _ = started[key]
        return _gather_finish(f"gather_{key}_finish", _gather_wait(f"gather_{key}_wait", sems, lands, after))

    g_mix0, g_ffn0 = _row(norm_mix_g[0]), _row(norm_ffn_g[0])
    w_a_in, w_a_out = gathered("sgu", mod)
    w_a_in = w_a_in.transpose(1, 0, 2).reshape(d, 2 * d)
    w_a_out = w_a_out.reshape(d, d)
    na = a_w_in.shape[2]
    h_a = _rms_mod("l0_mix_norm", x0, g_mix0, mods[0][1], mods[0][0], tm)
    pre = _mm("l0_sgu_in", (t_len // tm, 2 * d // tn),
              [(h_a, (tm, d), lambda i, j: (i, 0))], [(w_a_in, (d, tn), lambda i, j: (0, j))], NN,
              [((t_len, 2 * d), F32, (tm, tn), lambda i, j: (i, j))],
              epi=lambda accs, e: [accs[0] + e[0]], extras=[(a_b_in, (1, tn), lambda i, j: (0, j))])[0]
    w_s, b_s_t = a_w_s[0], jnp.pad(a_b_s[0].T, ((0, 0), (0, LANE - groups)))
    ln_g, ln_b = a_ln_g, a_ln_b
    yy = _sgu_fwd("l0_sgu_mix", pre, w_s, b_s_t, ln_g, ln_b)
    y_a, x1 = _out_proj("l0_sgu_out", yy, w_a_out, x0, mods[0][2], tm, tn)
    w_ffn0 = gathered("ffn0", x1)
    x2, ffn0_saved = _ffn_fwd("l0", x1, mods[0], g_ffn0, *w_ffn0, tm)

    g_mix1, g_ffn1 = _row(norm_mix_g[1]), _row(norm_ffn_g[1])
    h_b = _rms_mod("l1_mix_norm", x2, g_mix1, mods[1][1], mods[1][0], tm)
    w_b_in, w_b_out = gathered("fox", h_b)
    w_b_out = w_b_out.reshape(d, d)
    w_b_t = w_b_in.reshape(3 * d + heads, d)
    w_f_t = jnp.pad(w_b_t[3 * d:], ((0, LANE - heads), (0, 0)))
    qkv = _mm("l1_qkv", (t_len // tm, 3 * d // tn),
              [(h_b, (tm, d), lambda i, j: (i, 0))], [(w_b_t, (tn, d), lambda i, j: (j, 0))], NT,
              [((t_len, 3 * d), BF16, (tm, tn), lambda i, j: (i, j))])[0]
    f_logit = _mm("l1_forget_logit", (t_len // tm, 1),
                  [(h_b, (tm, d), lambda i, j: (i, 0))], [(w_f_t, (LANE, d), lambda i, j: (0, 0))], NT,
                  [((t_len, LANE), F32, (tm, LANE), lambda i, j: (i, 0))])[0]
    b_f = jnp.pad(b_b_f, ((0, 0), (0, LANE - heads)))
    f_cum, f_cum_t = _forget_cumsum("l1_forget_cumsum", f_logit, b_f)
    f_keys = f_cum_t[:heads].reshape(heads, 1, t_len)
    o, lse = _attn_fwd("l1_attn", qkv, f_cum, f_keys, heads, tq)
    y_b, x3 = _out_proj("l1_attn_out", o, w_b_out, x2, mods[1][2], tm, tn)
    w_ffn1 = gathered("ffn1", x3)
    x4, ffn1_saved = _ffn_fwd("l1", x3, mods[1], g_ffn1, *w_ffn1, tm)

    dx4, loss_cols, d_final_g = _final_loss("loss_head", x4, target, _row(final_g), tm // 2)
    loss = lax.psum(jnp.sum(loss_cols), ("x", "y", "c"))

    place = jnp.stack([mc, 2 * mx + my]).astype(jnp.int32)
    dx3, ffn1_grads, (dsh2_1, dsc2_1, dg2_1, dgf_1) = _ffn_bwd("l1", dx4, x3, ffn1_saved, mods[1], g_ffn1, *w_ffn1, tm)
    rs_ffn1 = _reduce_scatter_start("ffn1", ffn1_grads, place)
    dy_b, dg1_1 = _gate_bwd("l1_attn_gate_bwd", dx3, y_b, mods[1][2] + rs_ffn1[3][0, 0], tm)
    heads_tn = tn // CHUNK

    def with_delta(accs, e):
        prod = accs[0] * e[0].astype(F32)
        sums = [jnp.sum(prod[:, h * CHUNK:(h + 1) * CHUNK], axis=1, keepdims=True) for h in range(heads_tn)]
        return [accs[0], jnp.stack([jnp.broadcast_to(v, (tm, LANE)) for v in sums])]

    d_o, delta = _mm("l1_attn_out_bwd", (t_len // tm, d // tn),
                     [(dy_b, (tm, d), lambda i, j: (i, 0))], [(w_b_out, (tn, d), lambda i, j: (j, 0))], NT,
                     [((t_len, d), BF16, (tm, tn), lambda i, j: (i, j)),
                      ((heads, t_len, LANE), F32, (heads_tn, tm, LANE), lambda i, j: (j, i, 0))],
                     epi=with_delta, extras=[(o, (tm, tn), lambda i, j: (i, j))])
    d_w_b_out = _weight_grad("l1_attn_wout_grad", o, dy_b, tn, tn)
    dq, dk, dv, dfq, dfk = _attn_bwd("l1_attn_bwd", qkv, d_o, f_cum, f_keys, lse, delta, heads, tq)
    d_cum = dfq - jnp.pad(dfk.reshape(heads, t_len).T, ((0, 0), (0, LANE - heads)))
    d_logit, d_b_f = _forget_bwd("l1_forget_bwd", d_cum, f_logit, b_f)
    d_logit = d_logit.astype(BF16)
    w_grad = ((d, d), BF16, (tn, tn), lambda i, j: (i, j))
    d_w_qkv = _mm("l1_wqkv_grad", (d // tn, d // tn),
                  [(g, (t_len, tn), lambda i, j: (0, i)) for g in (dq, dk, dv)], [(h_b, (t_len, tn), lambda i, j: (0, j))],
                  TN, [w_grad] * 3, summed=False)
    d_w_f = _weight_grad("l1_wf_grad", d_logit, h_b, LANE, tn)
    dh_b = _mm("l1_qkv_bwd", (t_len // tm, d // tn),
               [(g, (tm, d), lambda i, j: (i, 0)) for g in (dq, dk, dv)] + [(d_logit, (tm, LANE), lambda i, j: (i, 0))],
               [(w_b_t, (d, tn), lambda i, j, p=p: (p, j)) for p in range(3)] + [(w_f_t, (LANE, tn), lambda i, j: (0, j))], NN,
               [((t_len, d), F32, (tm, tn), lambda i, j: (i, j))])[0]
    dx2, dsh1_1, dsc1_1, dgm_1 = _rms_mod_bwd("l1_mix_norm_bwd", x2, dh_b, dx3, g_mix1, mods[1][1], tm // 2)
    d_w_b_in = jnp.concatenate([*d_w_qkv, d_w_f[:heads]], axis=0).reshape(N_DEV, b_w_in.shape[2], d)
    rs_fox = _reduce_scatter_start("fox", [d_w_b_in, d_w_b_out.reshape(N_DEV, d // N_DEV, d)], place)

    mods[0][5] = mods[0][5] + rs_fox[3][0, 0]
    dx1, ffn0_grads, (dsh2_0, dsc2_0, dg2_0, dgf_0) = _ffn_bwd("l0", dx2, x1, ffn0_saved, mods[0], g_ffn0, *w_ffn0, tm)
    rs_ffn0 = _reduce_scatter_start("ffn0", ffn0_grads, place)
    dy_a, dg1_0 = _gate_bwd("l0_sgu_gate_bwd", dx1, y_a, mods[0][2] + rs_ffn0[3][0, 0], tm)
    dyy = _proj_bwd("l0_sgu_out_bwd", dy_a, w_a_out, F32, tm, tn)
    d_w_a_out = _weight_grad("l0_sgu_wout_grad", yy, dy_a, tn, tn)
    dpre, d_w_s, d_b_s_t, d_ln_g, d_ln_b, d_b_in = _sgu_bwd("l0_sgu_mix_bwd", pre, dyy, w_s, b_s_t, ln_g, ln_b)
    d_w_a_in = _mm("l0_sgu_win_grad", (N_DEV, d // tn),
                   [(h_a, (t_len, tn), lambda j, i: (0, i))], [(dpre, (t_len, na), lambda j, i: (0, j))], TN,
                   [((N_DEV, d, na), BF16, (None, tn, na), lambda j, i: (j, i, 0))])[0]
    dh_a = _proj_bwd("l0_sgu_in_bwd", dpre, w_a_in, F32, tm, tn)
    dx0, dsh1_0, dsc1_0, dgm_0 = _rms_mod_bwd("l0_mix_norm_bwd", x0, dh_a, dx1, g_mix0, mods[0][1], tm // 2)

    dmod = jnp.concatenate([dsh1_0, dsc1_0, dg1_0, dsh2_0, dsc2_0, dg2_0,
                            dsh1_1, dsc1_1, dg1_1, dsh2_1, dsc2_1, dg2_1], axis=1)
    small_grads = [dmod, jnp.concatenate([dgm_0, dgm_1], axis=1), jnp.concatenate([dgf_0, dgf_1], axis=1),
                   d_b_in, d_ln_g, d_ln_b, d_w_s, d_b_s_t[:, :groups].T, d_b_f[:, :heads], d_final_g]
    small_w = [ada_b, norm_mix_g, norm_ffn_g, a_b_in, a_ln_g, a_ln_b, a_w_s, a_b_s, b_b_f, final_g]
    small_m = [m_ada_b, m_norm_mix_g, m_norm_ffn_g, m_a_b_in, m_a_ln_g, m_a_ln_b, m_a_w_s, m_a_b_s, m_b_b_f, m_final_g]
    small_v = [v_ada_b, v_norm_mix_g, v_norm_ffn_g, v_a_b_in, v_a_ln_g, v_a_ln_b, v_a_w_s, v_a_b_s, v_b_b_f, v_final_g]
    n_small = sum(w.size for w in small_w)
    pack_rows = -(-n_small // (PACK_W * SUBLANE)) * SUBLANE

    def pack(arrs):
        flat = jnp.concatenate([a.reshape(-1) for a in arrs])
        return jnp.pad(flat, (0, pack_rows * PACK_W - n_small)).reshape(pack_rows, PACK_W)

    def unpack(packed):
        flat, out, pos = packed.reshape(-1), [], 0
        for w in small_w:
            out.append(flat[pos:pos + w.size].reshape(w.shape))
            pos += w.size
        return out

    all_small = _all_gather_small("gather_small_grads", pack(small_grads))

    behind_small = d_w_a_out + (all_small[0, 0, 0] * 0.0).astype(BF16)
    rs_sgu = _reduce_scatter_start("sgu", [d_w_a_in, behind_small.reshape(N_DEV, d // N_DEV, d)], place)
    small_out = [unpack(t) for t in _adamw("adamw_small", pack(small_w)[None], all_small[:, None],
                                           pack(small_m)[None], pack(small_v)[None])]

    dmod_all = all_small.reshape(N_DEV, -1)[:, :2 * d_mod].reshape(N_DEV, 2, d_mod)
    dmod_cols = lax.dynamic_slice_in_dim(dmod_all, me * mod_cols, mod_cols, axis=2).transpose(1, 0, 2)
    c_act_t = jnp.pad(c_act[:N_DEV].T, ((0, 0), (0, LANE - N_DEV)))
    g_ada = _ada_grad("ada_w_grad", c_act_t, dmod_cols, _tile(d, 256))
    r_ada = _adamw("adamw_ada_w", ada_w, g_ada[None], m_ada_w, v_ada_w)

    def arrived(tag, started_rs, after):
        sems, parts, lands, _ = started_rs
        return _scatter_wait(f"{tag}_reduce_chips_wait", sems, parts, lands, after)

    g_gate1, g_up1, g_down1 = arrived("ffn1", rs_ffn1, rs_sgu[3])
    g_b_in, g_b_out = arrived("fox", rs_fox, rs_sgu[3])
    g_gate0, g_up0, g_down0 = arrived("ffn0", rs_ffn0, rs_sgu[3])
    r_b_in = t_last(_adamw("adamw_b_w_in", b_in_t, [g_b_in], m_b_in_t, v_b_in_t))
    r_b_out = _adamw("adamw_b_w_out", b_w_out, [g_b_out], m_b_w_out, v_b_w_out)
    r_gate = t_last(_adamw("adamw_ffn_w_gate", gate_t, [g_gate0, g_gate1], m_gate_t, v_gate_t))
    r_up = t_last(_adamw("adamw_ffn_w_up", up_t, [g_up0, g_up1], m_up_t, v_up_t))
    r_down = _adamw("adamw_ffn_w_down", ffn_w_down, [g_down0, g_down1], m_ffn_w_down, v_ffn_w_down)
    first = lambda t: t[(0,) * t.ndim]
    done = [r_b_in[0], r_b_out[0], r_gate[0], r_up[0], r_down[0], r_ada[0], small_out[0][0]]
    g_a_in, g_a_out = arrived("sgu", rs_sgu, jnp.stack([first(t) for t in done]))
    r_a_in = _adamw("adamw_a_w_in", a_w_in, [g_a_in], m_a_w_in, v_a_w_in)
    r_a_out = _adamw("adamw_a_w_out", a_w_out, [g_a_out], m_a_w_out, v_a_w_out)

    def leaf(k):
        s = small_out[k]
        return [r_ada[k], s[0], s[1], s[2], r_a_in[k], s[3], s[4], s[5], s[6], s[7], r_a_out[k],
                r_b_in[k], s[8], r_b_out[k], r_gate[k], r_up[k], r_down[k], s[9]]

    return (loss, dx0[None], *leaf(0), *leaf(1), *leaf(2), *leaf(3))
```

```python
import functools
import math

import jax
import jax.numpy as jnp
from jax import lax
from jax.experimental import pallas as pl
from jax.experimental.pallas import tpu as pltpu

F32 = jnp.float32
BF16 = jnp.bfloat16
MESH = pl.DeviceIdType.MESH
ANY = pl.BlockSpec(memory_space=pl.ANY)
HBM = pl.BlockSpec(memory_space=pltpu.HBM)
SEM = pl.BlockSpec(memory_space=pltpu.SEMAPHORE)
VMEM_SPEC = pl.BlockSpec(memory_space=pltpu.VMEM)
EFFECT = pltpu.SideEffectType.DATAFLOW_SIDE_EFFECTING

N_DEV = 8
N_CHIP = 4
LANE = 128
SUBLANE = 8
CHUNK = 128
VMEM_LIMIT_BYTES = 52 * 1024 * 1024
NORM_EPS = 1e-6
NEG = -1e30
PACK_W = 1024

ADAM_LR = 0.001
ADAM_B1 = 0.9
ADAM_B2 = 0.999
ADAM_EPS = 1e-08
ADAM_WD = 0.01
ADAM_STEP = 10

NN = ((1,), (0,))
NT = ((1,), (1,))
TN = ((0,), (0,))


def _params(*sem):
    return pltpu.CompilerParams(dimension_semantics=sem or None, vmem_limit_bytes=VMEM_LIMIT_BYTES)


def _tile(n, target, align=LANE):
    best = None
    for t in range(align, min(n, target) + 1, align):
        if n % t == 0:
            best = t
    return best or n


def _dot(a, b, dims):
    return lax.dot_general(a, b, (dims, ((), ())), preferred_element_type=F32)


def _place():
    return lax.axis_index("x"), lax.axis_index("y"), lax.axis_index("c")


def _mm(name, grid, lhs, rhs, dims, outs, epi=None, extras=(), summed=True, k_axis=None):
    n_a, n_b, n_e, n_o = len(lhs), len(rhs), len(extras), len(outs)
    n_p = max(n_a, n_b)
    n_acc = 1 if summed else n_p
    nk = grid[k_axis] if k_axis is not None else 1
    acc_shape = tuple(d for d in outs[0][2] if d is not None)

    def body(*refs):
        a_refs, b_refs = refs[:n_a], refs[n_a:n_a + n_b]
        e_refs = refs[n_a + n_b:n_a + n_b + n_e]
        o_refs = refs[n_a + n_b + n_e:n_a + n_b + n_e + n_o]
        acc_refs = refs[n_a + n_b + n_e + n_o:]

        def products():
            a_vals = [r[...].astype(BF16) for r in a_refs]
            b_vals = [r[...].astype(BF16) for r in b_refs]
            ps = [_dot(a_vals[p % n_a], b_vals[p % n_b], dims) for p in range(n_p)]
            return [functools.reduce(lambda u, w: u + w, ps)] if summed else ps

        def finish(accs):
            res = epi(accs, [e[...] for e in e_refs]) if epi is not None else accs
            for o_ref, r in zip(o_refs, res):
                o_ref[...] = r.astype(o_ref.dtype)

        if nk == 1:
            finish(products())
        else:
            k = pl.program_id(k_axis)

            @pl.when(k == 0)
            def _():
                for acc in acc_refs:
                    acc[...] = jnp.zeros_like(acc)

            for acc, p in zip(acc_refs, products()):
                acc[...] += p

            @pl.when(k == nk - 1)
            def _():
                finish([acc[...] for acc in acc_refs])

    sem = ["parallel"] * len(grid)
    if k_axis is not None:
        sem[k_axis] = "arbitrary"
    res = pl.pallas_call(
        body, name=name, grid=grid,
        in_specs=[pl.BlockSpec(bs, im) for _, bs, im in (*lhs, *rhs, *extras)],
        out_specs=[pl.BlockSpec(bs, im) for _, _, bs, im in outs],
        out_shape=[jax.ShapeDtypeStruct(s, d) for s, d, _, _ in outs],
        scratch_shapes=[pltpu.VMEM(acc_shape, F32)] * (n_acc if nk > 1 else 0),
        compiler_params=_params(*sem),
    )(*[a for a, _, _ in (*lhs, *rhs, *extras)])
    return res


def _rowwise(name, fn, tiled, whole, out_tiled, out_sums, tm):
    rows = tiled[0].shape[0]
    n_t, n_w, n_o, n_s = len(tiled), len(whole), len(out_tiled), len(out_sums)

    def body(*refs):
        t_refs, w_refs = refs[:n_t], refs[n_t:n_t + n_w]
        o_refs = refs[n_t + n_w:n_t + n_w + n_o]
        s_refs = refs[n_t + n_w + n_o:]
        outs, sums = fn([r[...] for r in t_refs], [r[...] for r in w_refs])
        for o_ref, val in zip(o_refs, outs):
            o_ref[...] = val.astype(o_ref.dtype)

        @pl.when(pl.program_id(0) == 0)
        def _():
            for s_ref in s_refs:
                s_ref[...] = jnp.zeros_like(s_ref)

        for s_ref, val in zip(s_refs, sums):
            s_ref[...] += val

    full = lambda a: pl.BlockSpec(a.shape, lambda i, nd=a.ndim: (0,) * nd)
    res = pl.pallas_call(
        body, name=name, grid=(rows // tm,),
        in_specs=[pl.BlockSpec((tm, a.shape[1]), lambda i: (i, 0)) for a in tiled] + [full(a) for a in whole],
        out_specs=[pl.BlockSpec((tm, n), lambda i: (i, 0)) for n, _ in out_tiled]
        + [pl.BlockSpec(s, lambda i, nd=len(s): (0,) * nd) for s in out_sums],
        out_shape=[jax.ShapeDtypeStruct((rows, n), d) for n, d in out_tiled]
        + [jax.ShapeDtypeStruct(s, F32) for s in out_sums],
        compiler_params=_params("arbitrary"),
    )(*tiled, *whole)
    return res


def _colsum(v):
    return jnp.sum(v, axis=0, keepdims=True)


def _rms_parts(x):
    inv = lax.rsqrt(jnp.mean(x * x, axis=-1, keepdims=True) + NORM_EPS)
    return inv, x * inv


def _rms_mod(name, x, g, sc, sh, tm):
    def fn(t, w):
        _, xhat = _rms_parts(t[0])
        return [xhat * w[0] * (1.0 + w[1]) + w[2]], []
    return _rowwise(name, fn, [x], [g, sc, sh], [(x.shape[1], BF16)], [], tm)[0]


def _rms_mod_bwd(name, x, dh, dres, g, sc, tm):
    d = x.shape[1]

    def fn(t, w):
        x_, dh_, dres_ = t
        g_, sc_ = w
        inv, xhat = _rms_parts(x_)
        dn = dh_ * (1.0 + sc_)
        dxhat = dn * g_
        dx = dres_ + inv * (dxhat - xhat * jnp.mean(dxhat * xhat, axis=-1, keepdims=True))
        return [dx], [_colsum(dh_), _colsum(dh_ * (xhat * g_)), _colsum(dn * xhat)]
    return _rowwise(name, fn, [x, dh, dres], [g, sc], [(d, F32)], [(1, d)] * 3, tm)


def _gate_bwd(name, dx, y, gate, tm):
    d = dx.shape[1]

    def fn(t, w):
        return [t[0] * w[0]], [_colsum(t[0] * t[1].astype(F32))]
    return _rowwise(name, fn, [dx, y], [gate], [(d, BF16)], [(1, d)], tm)


def _final_loss(name, x, target, g, tm):
    d = x.shape[1]

    def fn(t, w):
        inv, xhat = _rms_parts(t[0])
        err = xhat * w[0] - t[1]
        dout = err * (1.0 / d)
        dxhat = dout * w[0]
        dx = inv * (dxhat - xhat * jnp.mean(dxhat * xhat, axis=-1, keepdims=True))
        return [dx], [_colsum(err * err) * (0.5 / d), _colsum(dout * xhat)]
    return _rowwise(name, fn, [x, target], [g], [(d, F32)], [(1, d)] * 2, tm)


def _silu_rows(name, c):
    def fn(t, w):
        return [t[0] * jax.nn.sigmoid(t[0])], []
    return _rowwise(name, fn, [c], [], [(c.shape[1], F32)], [], c.shape[0])[0]


def _gelu(x):
    return 0.5 * x * (1.0 + lax.erf(x * (1.0 / math.sqrt(2.0))))


def _gelu_grad(x):
    cdf = 0.5 * (1.0 + lax.erf(x * (1.0 / math.sqrt(2.0))))
    return cdf + x * jnp.exp(-0.5 * x * x) * (1.0 / math.sqrt(2.0 * math.pi))


def _layer_norm_parts(v):
    mu = jnp.mean(v, axis=-1, keepdims=True)
    cen = v - mu
    rstd = lax.rsqrt(jnp.mean(cen * cen, axis=-1, keepdims=True) + NORM_EPS)
    return rstd, cen * rstd


def _causal(n):
    return lax.broadcasted_iota(jnp.int32, (n, n), 0) >= lax.broadcasted_iota(jnp.int32, (n, n), 1)


def _sgu_fwd(name, pre, w_s, b_s_t, ln_g, ln_b):
    t_len, d2 = pre.shape
    d = d2 // 2
    groups = w_s.shape[0]

    def body(pre_ref, w_ref, bs_ref, g_ref, b_ref, yy_ref):
        z = _gelu(pre_ref[...])
        u, v = z[:, :d], z[:, d:]
        _, vhat = _layer_norm_parts(v)
        vn = (vhat * g_ref[...] + b_ref[...]).astype(BF16)
        mask = _causal(CHUNK)
        bs = bs_ref[...]
        for g in range(groups):
            cols = slice(g * CHUNK, (g + 1) * CHUNK)
            w = jnp.where(mask, w_ref[g], 0.0).astype(BF16)
            sv = _dot(w, vn[:, cols], NN) + bs[:, g:g + 1]
            yy_ref[:, cols] = (u[:, cols] * sv).astype(BF16)

    full = lambda a: pl.BlockSpec(a.shape, lambda i, nd=a.ndim: (0,) * nd)
    return pl.pallas_call(
        body, name=name, grid=(t_len // CHUNK,),
        in_specs=[pl.BlockSpec((CHUNK, d2), lambda i: (i, 0)), full(w_s), full(b_s_t), full(ln_g), full(ln_b)],
        out_specs=pl.BlockSpec((CHUNK, d), lambda i: (i, 0)),
        out_shape=jax.ShapeDtypeStruct((t_len, d), BF16),
        compiler_params=_params("parallel"),
    )(pre, w_s, b_s_t, ln_g, ln_b)


def _sgu_bwd(name, pre, dyy, w_s, b_s_t, ln_g, ln_b):
    t_len, d2 = pre.shape
    d = d2 // 2
    groups = w_s.shape[0]

    def body(pre_ref, dyy_ref, w_ref, bs_ref, g_ref, b_ref, dpre_ref, dw_ref, dbs_ref, dg_ref, db_ref, dbin_ref, dvn_ref):
        @pl.when(pl.program_id(0) == 0)
        def _():
            for r in (dw_ref, dbs_ref, dg_ref, db_ref, dbin_ref):
                r[...] = jnp.zeros_like(r)

        pre_v = pre_ref[...]
        z = _gelu(pre_v)
        u, v = z[:, :d], z[:, d:]
        rstd, vhat = _layer_norm_parts(v)
        vn = (vhat * g_ref[...] + b_ref[...]).astype(BF16)
        mask = _causal(CHUNK)
        bs = bs_ref[...]
        lane = lax.broadcasted_iota(jnp.int32, (CHUNK, LANE), 1)
        dbs = jnp.zeros((CHUNK, LANE), F32)
        for g in range(groups):
            cols = slice(g * CHUNK, (g + 1) * CHUNK)
            w = jnp.where(mask, w_ref[g], 0.0).astype(BF16)
            sv = _dot(w, vn[:, cols], NN) + bs[:, g:g + 1]
            dyy_g = dyy_ref[:, cols]
            dpre_ref[:, cols] = (dyy_g * sv * _gelu_grad(pre_v[:, cols])).astype(BF16)
            dsv = dyy_g * u[:, cols]
            dbs = jnp.where(lane == g, jnp.sum(dsv, axis=1, keepdims=True), dbs)
            dsv_b = dsv.astype(BF16)
            dw_ref[g] += jnp.where(mask, _dot(dsv_b, vn[:, cols], NT), 0.0)
            dvn_ref[:, cols] = _dot(w, dsv_b, TN)
        dbs_ref[...] += dbs
        dvn = dvn_ref[...]
        dg_ref[...] += _colsum(dvn * vhat)
        db_ref[...] += _colsum(dvn)
        dvhat = dvn * g_ref[...]
        dv = rstd * (dvhat - jnp.mean(dvhat, axis=-1, keepdims=True)
                     - vhat * jnp.mean(dvhat * vhat, axis=-1, keepdims=True))
        dpre_ref[:, d:] = (dv * _gelu_grad(pre_v[:, d:])).astype(BF16)
        dbin_ref[...] += _colsum(dpre_ref[...].astype(F32))

    full = lambda a: pl.BlockSpec(a.shape, lambda i, nd=a.ndim: (0,) * nd)
    acc = lambda s: pl.BlockSpec(s, lambda i, nd=len(s): (0,) * nd)
    sums = [(groups, CHUNK, CHUNK), (CHUNK, LANE), (1, d), (1, d), (1, d2)]
    return pl.pallas_call(
        body, name=name, grid=(t_len // CHUNK,),
        in_specs=[pl.BlockSpec((CHUNK, d2), lambda i: (i, 0)), pl.BlockSpec((CHUNK, d), lambda i: (i, 0)),
                  full(w_s), full(b_s_t), full(ln_g), full(ln_b)],
        out_specs=[pl.BlockSpec((CHUNK, d2), lambda i: (i, 0))] + [acc(s) for s in sums],
        out_shape=[jax.ShapeDtypeStruct((t_len, d2), BF16)] + [jax.ShapeDtypeStruct(s, F32) for s in sums],
        scratch_shapes=[pltpu.VMEM((CHUNK, d), F32)],
        compiler_params=_params("arbitrary"),
    )(pre, dyy, w_s, b_s_t, ln_g, ln_b)


def _whole(rows, cols):
    return pl.BlockSpec((rows, cols), lambda i: (0, 0))


def _cum_matrix(reverse):
    r = lax.broadcasted_iota(jnp.int32, (CHUNK, CHUNK), 0)
    c = lax.broadcasted_iota(jnp.int32, (CHUNK, CHUNK), 1)
    return jnp.where((r <= c) if reverse else (r >= c), 1.0, 0.0).astype(F32)


def _forget_cumsum(name, logits, bias):
    t_len = logits.shape[0]

    def body(fl_ref, b_ref, f_ref, ft_ref):
        tri = _cum_matrix(False)

        def step(n, carry):
            off = pl.multiple_of(n * CHUNK, CHUNK)
            xv = fl_ref[pl.ds(off, CHUNK), :] + b_ref[...]
            log_f = jnp.minimum(xv, 0.0) - jnp.log1p(jnp.exp(-jnp.abs(xv)))
            cs = jnp.dot(tri, log_f, precision=lax.Precision.HIGHEST, preferred_element_type=F32) + carry
            f_ref[pl.ds(off, CHUNK), :] = cs
            ft_ref[:, pl.ds(off, CHUNK)] = cs.T
            return cs[CHUNK - 1:CHUNK, :]

        lax.fori_loop(0, t_len // CHUNK, step, jnp.zeros((1, LANE), F32))

    return pl.pallas_call(
        body, name=name, grid=(1,),
        in_specs=[_whole(t_len, LANE), _whole(1, LANE)],
        out_specs=[_whole(t_len, LANE), _whole(LANE, t_len)],
        out_shape=[jax.ShapeDtypeStruct((t_len, LANE), F32), jax.ShapeDtypeStruct((LANE, t_len), F32)],
        compiler_params=_params("arbitrary"),
    )(logits, bias)


def _forget_bwd(name, d_cum, logits, bias):
    t_len = logits.shape[0]
    n_chunks = t_len // CHUNK

    def body(dc_ref, fl_ref, b_ref, dl_ref, db_ref, run_ref):
        @pl.when(pl.program_id(0) == 0)
        def _():
            run_ref[...] = jnp.zeros_like(run_ref)
            db_ref[...] = jnp.zeros_like(db_ref)

        rc = jnp.dot(_cum_matrix(True), dc_ref[...], precision=lax.Precision.HIGHEST,
                     preferred_element_type=F32) + run_ref[0:1, :]
        dl = rc * jax.nn.sigmoid(-(fl_ref[...] + b_ref[...]))
        dl_ref[...] = dl
        db_ref[...] += _colsum(dl)
        run_ref[...] = jnp.broadcast_to(rc[0:1, :], run_ref.shape)

    back = pl.BlockSpec((CHUNK, LANE), lambda i: (n_chunks - 1 - i, 0))
    return pl.pallas_call(
        body, name=name, grid=(n_chunks,),
        in_specs=[back, back, _whole(1, LANE)],
        out_specs=[back, _whole(1, LANE)],
        out_shape=[jax.ShapeDtypeStruct((t_len, LANE), F32), jax.ShapeDtypeStruct((1, LANE), F32)],
        scratch_shapes=[pltpu.VMEM((SUBLANE, LANE), F32)],
        compiler_params=_params("arbitrary"),
    )(d_cum, logits, bias)


def _head_column(f_tile, head):
    lane = lax.broadcasted_iota(jnp.int32, f_tile.shape, 1)
    return jnp.sum(jnp.where(lane == head, f_tile, 0.0), axis=1, keepdims=True)


def _attn_fwd(name, qkv, f_cum, f_keys, heads, tq):
    t_len = qkv.shape[0]
    scale = 1.0 / math.sqrt(CHUNK)

    def body(q_ref, k_ref, v_ref, f_ref, fk_ref, o_ref, lse_ref):
        head, i = pl.program_id(0), pl.program_id(1)
        q = q_ref[...]
        fq = _head_column(f_ref[...], head)
        causal = _causal(tq)

        def keys(j):
            return pl.ds(pl.multiple_of(j * tq, tq), tq)

        def update(s_raw, j, m, l, diagonal):
            s = s_raw * scale + fq - fk_ref[:, keys(j)]
            if diagonal:
                s = jnp.where(causal, s, NEG)
            m_new = jnp.maximum(m, jnp.max(s, axis=1, keepdims=True))
            p = jnp.exp(s - m_new)
            alpha = jnp.exp(m - m_new)
            return p.astype(BF16), alpha, m_new, alpha * l + jnp.sum(p, axis=1, keepdims=True)

        def step(j, carry):
            m, l, acc, s_raw, p_prev, alpha_prev = carry
            pv = _dot(p_prev, v_ref[keys(jnp.maximum(j - 1, 0)), :], NN)
            s_next = _dot(q, k_ref[keys(j + 1), :], NT)
            p, alpha, m, l = update(s_raw, j, m, l, False)
            return m, l, alpha_prev * acc + pv, s_next, p, alpha

        init = (jnp.full((tq, 1), NEG, F32), jnp.zeros((tq, 1), F32), jnp.zeros((tq, CHUNK), F32),
                _dot(q, k_ref[keys(0), :], NT), jnp.zeros((tq, tq), BF16), jnp.ones((tq, 1), F32))
        m, l, acc, s_raw, p_prev, alpha_prev = lax.fori_loop(0, i, step, init)
        pv = _dot(p_prev, v_ref[keys(jnp.maximum(i - 1, 0)), :], NN)
        p, alpha, m, l = update(s_raw, i, m, l, True)
        acc = alpha * (alpha_prev * acc + pv) + _dot(p, v_ref[keys(i), :], NN)
        o_ref[...] = (acc / l).astype(BF16)
        lse_ref[...] = jnp.broadcast_to(m + jnp.log(l), (tq, LANE))

    return pl.pallas_call(
        body, name=name, grid=(heads, t_len // tq),
        in_specs=[pl.BlockSpec((tq, CHUNK), lambda h, i: (i, h)),
                  pl.BlockSpec((t_len, CHUNK), lambda h, i: (0, heads + h)),
                  pl.BlockSpec((t_len, CHUNK), lambda h, i: (0, 2 * heads + h)),
                  pl.BlockSpec((tq, LANE), lambda h, i: (i, 0)),
                  pl.BlockSpec((None, 1, t_len), lambda h, i: (h, 0, 0))],
        out_specs=[pl.BlockSpec((tq, CHUNK), lambda h, i: (i, h)),
                   pl.BlockSpec((None, tq, LANE), lambda h, i: (h, i, 0))],
        out_shape=[jax.ShapeDtypeStruct((t_len, heads * CHUNK), BF16),
                   jax.ShapeDtypeStruct((heads, t_len, LANE), F32)],
        compiler_params=_params("parallel", "parallel"),
    )(qkv, qkv, qkv, f_cum, f_keys)


def _attn_bwd(name, qkv, d_o, f_cum, f_keys, lse, delta, heads, tq):
    t_len = qkv.shape[0]
    d = heads * CHUNK
    n_q = t_len // tq
    scale = 1.0 / math.sqrt(CHUNK)

    def body(q_ref, k_ref, v_ref, do_ref, f_ref, fk_ref, lse_ref, dl_ref, dq_ref, dk_ref, dv_ref, dfq_ref, dfk_ref,
             dq_acc):
        head, j = pl.program_id(0), pl.program_id(1)

        @pl.when(j == 0)
        def _():
            dq_acc[...] = jnp.zeros_like(dq_acc)

        @pl.when((j == 0) & (head == 0))
        def _():
            dfq_ref[...] = jnp.zeros_like(dfq_ref)

        k, v, fk = k_ref[...], v_ref[...], fk_ref[...]
        lane = lax.broadcasted_iota(jnp.int32, (tq, LANE), 1)
        causal = _causal(tq)

        def rows(i):
            return pl.ds(pl.multiple_of(i * tq, tq), tq)

        def products(i):
            r = rows(i)
            return _dot(q_ref[r, :], k, NT), _dot(do_ref[r, :], v, NT)

        def elementwise(i, s_raw, dp, diagonal):
            r = rows(i)
            s = s_raw * scale + _head_column(f_ref[r, :], head) - fk
            if diagonal:
                s = jnp.where(causal, s, NEG)
            p = jnp.exp(s - lse_ref[r, :][:, 0:1])
            ds = p * (dp - dl_ref[r, :][:, 0:1])
            dfq_ref[r, :] += jnp.where(lane == head, jnp.sum(ds, axis=1, keepdims=True), 0.0)
            return ds.astype(BF16), p.astype(BF16), _colsum(ds)

        def flush(i, ds_b, p_b, dk, dv):
            r = rows(i)
            dq_acc[r, :] += _dot(ds_b, k, NN)
            return dk + _dot(ds_b, q_ref[r, :], TN), dv + _dot(p_b, do_ref[r, :], TN)

        def step(i, carry):
            dk, dv, dfk, ds_prev, p_prev = carry
            s_raw, dp = products(i)
            dk, dv = flush(i - 1, ds_prev, p_prev, dk, dv)
            ds_b, p_b, col = elementwise(i, s_raw, dp, False)
            return dk, dv, dfk + col, ds_b, p_b

        s_raw, dp = products(j)
        ds_b, p_b, dfk = elementwise(j, s_raw, dp, True)
        zero = jnp.zeros((tq, CHUNK), F32)
        dk, dv, dfk, ds_b, p_b = lax.fori_loop(j + 1, n_q, step, (zero, zero, dfk, ds_b, p_b))
        dk, dv = flush(n_q - 1, ds_b, p_b, dk, dv)
        dk_ref[...] = (dk * scale).astype(BF16)
        dv_ref[...] = dv.astype(BF16)
        dfk_ref[...] = dfk

        @pl.when(j == n_q - 1)
        def _():
            dq_ref[...] = (dq_acc[...] * scale).astype(BF16)

    whole_head = lambda c0: pl.BlockSpec((t_len, CHUNK), lambda h, j: (0, c0 + h))
    per_head = pl.BlockSpec((None, t_len, LANE), lambda h, j: (h, 0, 0))
    key_block = lambda c0: pl.BlockSpec((tq, CHUNK), lambda h, j: (j, c0 + h))
    return pl.pallas_call(
        body, name=name, grid=(heads, n_q),
        in_specs=[whole_head(0), key_block(heads), key_block(2 * heads), whole_head(0),
                  pl.BlockSpec((t_len, LANE), lambda h, j: (0, 0)),
                  pl.BlockSpec((None, 1, tq), lambda h, j: (h, 0, j)),
                  per_head, per_head],
        out_specs=[whole_head(0), key_block(0), key_block(0),
                   pl.BlockSpec((t_len, LANE), lambda h, j: (0, 0)),
                   pl.BlockSpec((None, 1, tq), lambda h, j: (h, 0, j))],
        out_shape=[jax.ShapeDtypeStruct((t_len, d), BF16), jax.ShapeDtypeStruct((t_len, d), BF16),
                   jax.ShapeDtypeStruct((t_len, d), BF16), jax.ShapeDtypeStruct((t_len, LANE), F32),
                   jax.ShapeDtypeStruct((heads, 1, t_len), F32)],
        scratch_shapes=[pltpu.VMEM((t_len, CHUNK), F32)],
        compiler_params=_params("arbitrary", "arbitrary"),
    )(qkv, qkv, qkv, d_o, f_cum, f_keys, lse, delta)


def _ada_grad(name, c_act_t, dmod, tm):
    d = c_act_t.shape[0]
    n_layer, n_b, n_col = dmod.shape

    def body(c_ref, dm_ref, o_ref):
        c, dm = c_ref[...], dm_ref[...]
        acc = c[:, 0:1] * dm[0:1, :]
        for b in range(1, N_DEV):
            acc = acc + c[:, b:b + 1] * dm[b:b + 1, :]
        o_ref[...] = acc

    return pl.pallas_call(
        body, name=name, grid=(n_layer, d // tm),
        in_specs=[pl.BlockSpec((tm, LANE), lambda l, i: (i, 0)),
                  pl.BlockSpec((None, n_b, n_col), lambda l, i: (l, 0, 0))],
        out_specs=pl.BlockSpec((None, tm, n_col), lambda l, i: (l, i, 0)),
        out_shape=jax.ShapeDtypeStruct((n_layer, d, n_col), F32),
        compiler_params=_params("parallel", "parallel"),
    )(c_act_t, dmod)


def _adamw(name, w, parts, m, v):
    n_layer, rows, cols = w.shape
    per_layer = isinstance(parts, (list, tuple))
    parts = list(parts) if per_layer else [parts]
    by_rows = rows % (2 * SUBLANE) == 0
    tr = _tile(rows, max(2 * SUBLANE, (1 << 19) // cols), 2 * SUBLANE) if by_rows else rows
    tc = cols if by_rows else _tile(cols, max(LANE, (1 << 19) // rows))
    n_tiles = rows // tr if by_rows else cols // tc
    at = (lambda i: (i, 0)) if by_rows else (lambda i: (0, i))

    def body(*refs):
        w_ref, m_ref, v_ref = refs[:3]
        p_refs = refs[3:3 + len(parts)]
        g_ref, d_ref, mo_ref, vo_ref = refs[3 + len(parts):]
        layer = pl.program_id(0)
        g = None
        for n, p_ref in enumerate(p_refs):
            g_n = p_ref[0].astype(F32)
            for p in range(1, p_ref.shape[0]):
                g_n = g_n + p_ref[p].astype(F32)
            g = g_n if g is None else jnp.where(layer == n, g_n, g)
        m_new = ADAM_B1 * m_ref[...] + (1.0 - ADAM_B1) * g
        v_new = ADAM_B2 * v_ref[...] + (1.0 - ADAM_B2) * jnp.square(g)
        m_hat = m_new / (1.0 - ADAM_B1 ** ADAM_STEP)
        v_hat = v_new / (1.0 - ADAM_B2 ** ADAM_STEP)
        g_ref[...] = g
        d_ref[...] = -ADAM_LR * (m_hat / (jnp.sqrt(v_hat) + ADAM_EPS) + ADAM_WD * w_ref[...])
        mo_ref[...] = m_new
        vo_ref[...] = v_new

    blk = pl.BlockSpec((None, tr, tc), lambda l, i: (l, *at(i)))
    if per_layer:
        p_specs = [pl.BlockSpec((p.shape[0], tr, tc), lambda l, i, n=n: (0, *at(jnp.where(l == n, i, 0))))
                   for n, p in enumerate(parts)]
    else:
        p_specs = [pl.BlockSpec((parts[0].shape[0], None, tr, tc), lambda l, i: (0, l, *at(i)))]
    return pl.pallas_call(
        body, name=name, grid=(n_layer, n_tiles),
        in_specs=[blk] * 3 + p_specs,
        out_specs=[blk] * 4,
        out_shape=[jax.ShapeDtypeStruct(w.shape, F32)] * 4,
        compiler_params=_params("parallel", "parallel"),
    )(w, m, v, *parts)


def _dev_index(p):
    return 4 * p[0] + 2 * p[1] + p[2]


def _other_chips(mx, my):
    return [(1 - mx, my), (mx, 1 - my), (1 - mx, 1 - my)]


def _all_gather_small(name, x):
    rows, cols = x.shape

    def body(x_ref, o_ref, send_sems, recv_sems):
        mx, my, mc = _place()
        me = _dev_index((mx, my, mc))
        o_ref[me] = x_ref[...]

        def copy(dist, slot, peer):
            return pltpu.make_async_remote_copy(
                src_ref=x_ref, dst_ref=o_ref.at[slot], send_sem=send_sems.at[dist - 1], recv_sem=recv_sems.at[dist - 1],
                device_id=(peer // 4, (peer // 2) % 2, peer % 2), device_id_type=MESH)

        sends = [copy(dist, me, (me + dist) % N_DEV) for dist in range(1, N_DEV)]
        for cp in sends:
            cp.start()
        for dist in range(1, N_DEV):
            src = (me + N_DEV - dist) % N_DEV
            copy(dist, src, src).wait_recv()
        for cp in sends:
            cp.wait_send()

    return pl.pallas_call(
        body, name=name,
        out_shape=jax.ShapeDtypeStruct((N_DEV, rows, cols), x.dtype),
        in_specs=[pl.BlockSpec(memory_space=pltpu.VMEM)],
        out_specs=pl.BlockSpec(memory_space=pltpu.VMEM),
        scratch_shapes=[pltpu.SemaphoreType.DMA((N_DEV - 1,)), pltpu.SemaphoreType.DMA((N_DEV - 1,))],
        compiler_params=_params(),
    )(x)


def _split_copy_call(name, body, n_in, sems, through, extra_out=(), first=True):
    sem_shapes = [pltpu.SemaphoreType.DMA((k,)) for k in sems]
    if first:
        return pl.pallas_call(
            body, name=name,
            out_shape=(*sem_shapes, *[pltpu.HBM(t.shape, t.dtype) for t in through], *extra_out),
            in_specs=[HBM] * n_in,
            out_specs=(*[SEM] * len(sems), *[HBM] * len(through), *[VMEM_SPEC] * len(extra_out)),
            input_output_aliases={i: len(sems) + i for i in range(len(through))},
            compiler_params=pltpu.CompilerParams(has_side_effects=EFFECT),
        )
    return pl.pallas_call(
        body, name=name,
        out_shape=tuple(pltpu.HBM(t.shape, t.dtype) for t in through),
        in_specs=[HBM] * len(through) + [SEM] * len(sems) + [ANY],
        out_specs=tuple([HBM] * len(through)),
        input_output_aliases={i: i for i in range(len(through))},
        compiler_params=pltpu.CompilerParams(has_side_effects=EFFECT),
    )


def _own_slot(name, shard, me, token):
    rows, cols = shard.shape
    tr = _tile(rows, max(2 * SUBLANE, (1 << 19) // cols), 2 * SUBLANE)

    def body(me_ref, x_ref, t_ref, o_ref):
        o_ref[...] = (x_ref[...] + t_ref[0:1, 0:1]).astype(BF16)

    return pl.pallas_call(
        body, name=name,
        grid_spec=pltpu.PrefetchScalarGridSpec(
            num_scalar_prefetch=1, grid=(rows // tr,),
            in_specs=[pl.BlockSpec((tr, cols), lambda i, me: (i, 0)),
                      pl.BlockSpec((SUBLANE, LANE), lambda i, me: (0, 0))],
            out_specs=pl.BlockSpec((None, tr, cols), lambda i, me: (me[0], i, 0))),
        out_shape=jax.ShapeDtypeStruct((N_DEV, rows, cols), BF16),
        compiler_params=_params("parallel"),
    )(me, shard, token)


def _gather_start(name, lands):
    n = len(lands)

    def body(*refs):
        land_refs = refs[:n]
        send_sems, recv_sems = refs[n], refs[n + 1]
        token = refs[-1]
        mx, my, mc = _place()
        for a in range(n):
            own = land_refs[a].at[_dev_index((mx, my, mc))]
            for j, chip in enumerate(_other_chips(mx, my)):
                pltpu.make_async_remote_copy(
                    src_ref=own, dst_ref=own, send_sem=send_sems.at[3 * a + j], recv_sem=recv_sems.at[3 * a + j],
                    device_id=(*chip, mc), device_id_type=MESH).start()
        token[...] = jnp.zeros_like(token)

    operands = [pltpu.with_memory_space_constraint(t, pltpu.HBM) for t in lands]
    res = _split_copy_call(name, body, n, (3 * n, 3 * n), operands,
                           extra_out=(jax.ShapeDtypeStruct((SUBLANE, LANE), F32),))(*operands)
    return res[:2], res[2:2 + n], res[-1]


def _gather_wait(name, sems, lands, after):
    n = len(lands)

    def body(*refs):
        land_refs = refs[:n]
        send_sems, recv_sems = refs[n], refs[n + 1]
        mx, my, mc = _place()
        for a in range(n):
            for j, chip in enumerate(_other_chips(mx, my)):
                copy = pltpu.make_async_remote_copy(
                    src_ref=land_refs[a].at[_dev_index((mx, my, mc))], dst_ref=land_refs[a].at[_dev_index((*chip, mc))],
                    send_sem=send_sems.at[3 * a + j], recv_sem=recv_sems.at[3 * a + j],
                    device_id=(*chip, mc), device_id_type=MESH)
                copy.wait_send()
                copy.wait_recv()

    return _split_copy_call(name, body, n, (3 * n, 3 * n), lands, first=False)(*lands, *sems, after)


def _gather_finish(name, lands):
    n = len(lands)

    def body(*refs):
        o_refs = refs[n:2 * n]
        send_sems, recv_sems = refs[2 * n:]
        mx, my, mc = _place()
        blocks = [(mx, my)] + _other_chips(mx, my)

        def copy(a, k, core):
            slot = o_refs[a].at[_dev_index((*blocks[k], core))]
            return pltpu.make_async_remote_copy(
                src_ref=slot, dst_ref=slot, send_sem=send_sems.at[N_CHIP * a + k], recv_sem=recv_sems.at[N_CHIP * a + k],
                device_id=(mx, my, 1 - mc), device_id_type=MESH)

        sends = [copy(a, k, mc) for a in range(n) for k in range(N_CHIP)]
        for cp in sends:
            cp.start()
        for a in range(n):
            for k in range(N_CHIP):
                copy(a, k, 1 - mc).wait_recv()
        for cp in sends:
            cp.wait_send()

    return pl.pallas_call(
        body, name=name,
        out_shape=[jax.ShapeDtypeStruct(t.shape, t.dtype) for t in lands],
        in_specs=[ANY] * n, out_specs=[ANY] * n,
        input_output_aliases={a: a for a in range(n)},
        scratch_shapes=[pltpu.SemaphoreType.DMA((N_CHIP * n,)), pltpu.SemaphoreType.DMA((N_CHIP * n,))],
        compiler_params=_params(),
    )(*lands)


def _exchange_sibling(name, grads):
    n = len(grads)

    def body(*refs):
        g_refs, r_refs = refs[:n], refs[n:2 * n]
        send_sems, recv_sems = refs[2 * n:]
        mx, my, mc = _place()

        def copy(a, q):
            return pltpu.make_async_remote_copy(
                src_ref=g_refs[a].at[q, 1 - mc], dst_ref=r_refs[a].at[q],
                send_sem=send_sems.at[N_CHIP * a + q], recv_sem=recv_sems.at[N_CHIP * a + q],
                device_id=(mx, my, 1 - mc), device_id_type=MESH)

        copies = [copy(a, q) for a in range(n) for q in range(N_CHIP)]
        for cp in copies:
            cp.start()
        for cp in copies:
            cp.wait_recv()
        for cp in copies:
            cp.wait_send()

    return pl.pallas_call(
        body, name=name,
        out_shape=[jax.ShapeDtypeStruct((N_CHIP, *g.shape[2:]), g.dtype) for g in grads],
        in_specs=[ANY] * n, out_specs=[ANY] * n,
        scratch_shapes=[pltpu.SemaphoreType.DMA((N_CHIP * n,)), pltpu.SemaphoreType.DMA((N_CHIP * n,))],
        compiler_params=_params(),
    )(*grads)


def _add_sibling(name, grad, recv, place):
    _, _, rows, cols = grad.shape
    tr = _tile(rows, max(2 * SUBLANE, (1 << 19) // cols), 2 * SUBLANE)

    def body(p_ref, g_ref, r_ref, o_ref, land_ref):
        total = (g_ref[...].astype(F32) + r_ref[...].astype(F32)).astype(o_ref.dtype)
        o_ref[...] = total

        @pl.when(pl.program_id(1) == p_ref[1])
        def _():
            land_ref[...] = total

    out = jax.ShapeDtypeStruct(recv.shape, recv.dtype)
    return pl.pallas_call(
        body, name=name,
        grid_spec=pltpu.PrefetchScalarGridSpec(
            num_scalar_prefetch=1, grid=(rows // tr, N_CHIP),
            in_specs=[pl.BlockSpec((None, None, tr, cols), lambda i, q, p: (q, p[0], i, 0)),
                      pl.BlockSpec((None, tr, cols), lambda i, q, p: (q, i, 0))],
            out_specs=[pl.BlockSpec((None, tr, cols), lambda i, q, p: (q, i, 0)),
                       pl.BlockSpec((None, tr, cols), lambda i, q, p: (p[1], i, 0))]),
        out_shape=[out, out],
        compiler_params=_params("parallel", "arbitrary"),
    )(place, grad, recv)


def _scatter_start(name, parts, lands):
    n = len(parts)

    def body(*refs):
        p_refs, land_refs = refs[:n], refs[n:2 * n]
        send_sems, recv_sems = refs[2 * n], refs[2 * n + 1]
        token = refs[-1]
        mx, my, mc = _place()
        for a in range(n):
            for j, chip in enumerate(_other_chips(mx, my)):
                pltpu.make_async_remote_copy(
                    src_ref=p_refs[a].at[2 * chip[0] + chip[1]], dst_ref=land_refs[a].at[2 * mx + my],
                    send_sem=send_sems.at[3 * a + j], recv_sem=recv_sems.at[3 * a + j],
                    device_id=(*chip, mc), device_id_type=MESH).start()
        token[...] = jnp.zeros_like(token)

    operands = [pltpu.with_memory_space_constraint(t, pltpu.HBM) for t in (*parts, *lands)]
    res = _split_copy_call(name, body, 2 * n, (3 * n, 3 * n), operands,
                           extra_out=(jax.ShapeDtypeStruct((SUBLANE, LANE), F32),))(*operands)
    return res[:2], res[2:2 + n], res[2 + n:2 + 2 * n], res[-1]


def _scatter_wait(name, sems, parts, lands, after):
    n = len(parts)

    def body(*refs):
        p_refs, land_refs = refs[:n], refs[n:2 * n]
        send_sems, recv_sems = refs[2 * n], refs[2 * n + 1]
        mx, my, mc = _place()
        for a in range(n):
            for j, chip in enumerate(_other_chips(mx, my)):
                copy = pltpu.make_async_remote_copy(
                    src_ref=p_refs[a].at[2 * chip[0] + chip[1]], dst_ref=land_refs[a].at[2 * chip[0] + chip[1]],
                    send_sem=send_sems.at[3 * a + j], recv_sem=recv_sems.at[3 * a + j],
                    device_id=(*chip, mc), device_id_type=MESH)
                copy.wait_send()
                copy.wait_recv()

    res = _split_copy_call(name, body, 2 * n, (3 * n, 3 * n), [*parts, *lands], first=False)(
        *parts, *lands, *sems, after)
    return res[n:]


def _reduce_scatter_start(tag, grads, place):
    wide = [g.reshape(N_CHIP, 2, *g.shape[1:]) for g in grads]
    from_sibling = _exchange_sibling(f"{tag}_reduce_sibling", wide)
    added = [_add_sibling(f"{tag}_add_sibling_{n}", g, r, place) for n, (g, r) in enumerate(zip(wide, from_sibling))]
    return _scatter_start(f"{tag}_reduce_chips_start", [p for p, _ in added], [l for _, l in added])


def _row(v):
    return v.reshape(1, -1)


def _out_proj(name, act, w_out, x, gate, tm, tn):
    t_len, d = x.shape
    k = act.shape[1]
    return _mm(name, (t_len // tm, d // tn),
               [(act, (tm, k), lambda i, j: (i, 0))], [(w_out, (k, tn), lambda i, j: (0, j))], NN,
               [((t_len, d), BF16, (tm, tn), lambda i, j: (i, j)), ((t_len, d), F32, (tm, tn), lambda i, j: (i, j))],
               epi=lambda accs, e: [accs[0], e[0] + e[1] * accs[0]],
               extras=[(x, (tm, tn), lambda i, j: (i, j)), (gate, (1, tn), lambda i, j: (0, j))])


def _proj_bwd(name, dy, w_out, dtype, tm, tn):
    t_len, d = dy.shape
    k = w_out.shape[0]
    return _mm(name, (t_len // tm, k // tn),
               [(dy, (tm, d), lambda i, j: (i, 0))], [(w_out, (tn, d), lambda i, j: (j, 0))], NT,
               [((t_len, k), dtype, (tm, tn), lambda i, j: (i, j))])[0]


def _weight_grad(name, act, dy, tm, tn):
    t_len, k = act.shape
    n = dy.shape[1]
    return _mm(name, (k // tm, n // tn),
               [(act, (t_len, tm), lambda i, j: (0, i))], [(dy, (t_len, tn), lambda i, j: (0, j))], TN,
               [((k, n), BF16, (tm, tn), lambda i, j: (i, j))])[0]


def _ffn_fwd(tag, x1, mod, g_norm, w_gate, w_up, w_down, tm):
    t_len, d = x1.shape
    fs = w_gate.shape[1]
    sh2, sc2, g2 = mod[3], mod[4], mod[5]
    h2 = _rms_mod(f"{tag}_ffn_norm", x1, g_norm, sc2, sh2, tm)
    hidden = ((N_DEV, t_len, fs), BF16, (None, tm, fs), lambda j, i: (j, i, 0))

    def swiglu(accs, _):
        a, b = accs
        return [a, b, a * jax.nn.sigmoid(a) * b]

    a, b, s = _mm(f"{tag}_ffn_up", (N_DEV, t_len // tm),
                  [(h2, (tm, d), lambda j, i: (i, 0))],
                  [(w_gate, (None, fs, d), lambda j, i: (j, 0, 0)), (w_up, (None, fs, d), lambda j, i: (j, 0, 0))],
                  NT, [hidden] * 3, epi=swiglu, summed=False)
    f, x2 = _mm(f"{tag}_ffn_down", (t_len // tm, 1, N_DEV // 2),
                [(s, (None, tm, fs), lambda i, j, k, p=p: (2 * k + p, i, 0)) for p in range(2)],
                [(w_down, (None, fs, d), lambda i, j, k, p=p: (2 * k + p, 0, 0)) for p in range(2)],
                NN,
                [((t_len, d), BF16, (tm, d), lambda i, j, k: (i, 0)), ((t_len, d), F32, (tm, d), lambda i, j, k: (i, 0))],
                epi=lambda accs, e: [accs[0], e[0] + e[1] * accs[0]],
                extras=[(x1, (tm, d), lambda i, j, k: (i, 0)), (g2, (1, d), lambda i, j, k: (0, 0))], k_axis=2)
    return x2, (h2, a, b, s, f)


def _ffn_bwd(tag, dx2, x1, saved, mod, g_norm, w_gate, w_up, w_down, tm):
    t_len, d = x1.shape
    fs = w_gate.shape[1]
    h2, a, b, s, f = saved
    sc2, g2 = mod[4], mod[5]
    df, dg2 = _gate_bwd(f"{tag}_ffn_gate_bwd", dx2, f, g2, tm)
    hidden = ((N_DEV, t_len, fs), BF16, (None, tm, fs), lambda j, i: (j, i, 0))
    hid_in = lambda arr: (arr, (None, tm, fs), lambda j, i: (j, i, 0))

    def swiglu_bwd(accs, e):
        a_, b_ = e[0].astype(F32), e[1].astype(F32)
        sig = jax.nn.sigmoid(a_)
        return [accs[0] * b_ * sig * (1.0 + a_ * (1.0 - sig)), accs[0] * a_ * sig]

    da, db = _mm(f"{tag}_ffn_down_bwd", (N_DEV, t_len // tm),
                 [(df, (tm, d), lambda j, i: (i, 0))], [(w_down, (None, fs, d), lambda j, i: (j, 0, 0))], NT,
                 [hidden] * 2, epi=swiglu_bwd, extras=[hid_in(a), hid_in(b)])
    tn = _tile(d, 512)
    d_wd = _mm(f"{tag}_ffn_wdown_grad", (N_DEV, d // tn),
               [(s, (None, t_len, fs), lambda j, i: (j, 0, 0))], [(df, (t_len, tn), lambda j, i: (0, i))], TN,
               [((N_DEV, fs, d), BF16, (None, fs, tn), lambda j, i: (j, 0, i))])[0]
    w_grad = ((N_DEV, fs, d), BF16, (None, fs, tn), lambda j, i: (j, 0, i))
    d_wg, d_wu = _mm(f"{tag}_ffn_wup_grad", (N_DEV, d // tn),
                     [(da, (None, t_len, fs), lambda j, i: (j, 0, 0)), (db, (None, t_len, fs), lambda j, i: (j, 0, 0))],
                     [(h2, (t_len, tn), lambda j, i: (0, i))],
                     TN, [w_grad] * 2, summed=False)
    dh2 = _mm(f"{tag}_ffn_up_bwd", (t_len // tm, 1, N_DEV // 2),
              [(g, (None, tm, fs), lambda i, j, k, p=p: (2 * k + p, i, 0)) for p in range(2) for g in (da, db)],
              [(w, (None, fs, d), lambda i, j, k, p=p: (2 * k + p, 0, 0)) for p in range(2) for w in (w_gate, w_up)],
              NN, [((t_len, d), F32, (tm, d), lambda i, j, k: (i, 0))], k_axis=2)[0]
    dx1, dsh2, dsc2, dgn = _rms_mod_bwd(f"{tag}_ffn_norm_bwd", x1, dh2, dx2, g_norm, sc2, tm // 2)
    return dx1, (d_wg, d_wu, d_wd), (dsh2, dsc2, dg2, dgn)


def kernel(x, c, ada_w, ada_b, norm_mix_g, norm_ffn_g, a_w_in, a_b_in, a_ln_g, a_ln_b, a_w_s, a_b_s, a_w_out, b_w_in, b_b_f, b_w_out, ffn_w_gate, ffn_w_up, ffn_w_down, final_g, loss_target, m_ada_w, m_ada_b, m_norm_mix_g, m_norm_ffn_g, m_a_w_in, m_a_b_in, m_a_ln_g, m_a_ln_b, m_a_w_s, m_a_b_s, m_a_w_out, m_b_w_in, m_b_b_f, m_b_w_out, m_ffn_w_gate, m_ffn_w_up, m_ffn_w_down, m_final_g, v_ada_w, v_ada_b, v_norm_mix_g, v_norm_ffn_g, v_a_w_in, v_a_b_in, v_a_ln_g, v_a_ln_b, v_a_w_s, v_a_b_s, v_a_w_out, v_b_w_in, v_b_b_f, v_b_w_out, v_ffn_w_gate, v_ffn_w_up, v_ffn_w_down, v_final_g):
    t_len, d = x.shape[1], x.shape[2]
    heads = d // CHUNK
    groups = a_w_s.shape[1]
    d_mod = 6 * d
    mod_cols = ada_w.shape[2]
    tm = _tile(t_len, 512)
    tn = _tile(d, 512)
    tn_wide = _tile(d, 1024)
    tq = _tile(t_len, 256)
    mx, my, mc = _place()
    me = _dev_index((mx, my, mc))
    x0, target = x[0], loss_target[0]

    t_last = lambda arrs: [a.transpose(0, 2, 1) for a in arrs]
    gate_t, m_gate_t, v_gate_t = t_last([ffn_w_gate, m_ffn_w_gate, v_ffn_w_gate])
    up_t, m_up_t, v_up_t = t_last([ffn_w_up, m_ffn_w_up, v_ffn_w_up])
    b_in_t, m_b_in_t, v_b_in_t = t_last([b_w_in, m_b_w_in, v_b_w_in])
    groups_w = {"sgu": [a_w_in[0], a_w_out[0]], "ffn0": [gate_t[0], up_t[0], ffn_w_down[0]],
                "fox": [b_in_t[0], b_w_out[0]], "ffn1": [gate_t[1], up_t[1], ffn_w_down[1]]}
    c_all = _all_gather_small("gather_c", jnp.pad(c, ((0, SUBLANE - 1), (0, 0))))[:, 0, :]
    c_act = _silu_rows("silu_c", jnp.pad(c_all, ((0, 2 * SUBLANE - N_DEV), (0, 0))))
    mod_part = _mm("mod_matmul", (2, mod_cols // _tile(mod_cols, 512)),
                   [(c_act, (2 * SUBLANE, d), lambda l, j: (0, 0))],
                   [(ada_w, (None, d, _tile(mod_cols, 512)), lambda l, j: (l, 0, j))], NN,
                   [((2, 2 * SUBLANE, mod_cols), F32, (None, 2 * SUBLANE, _tile(mod_cols, 512)), lambda l, j: (l, 0, j))])[0]
    mod_all = _all_gather_small("gather_mod", mod_part[:, :N_DEV, :].reshape(2 * N_DEV, mod_cols))
    mod_mine = lax.dynamic_index_in_dim(mod_all.reshape(N_DEV, 2, N_DEV, mod_cols), me, axis=2, keepdims=False)
    mod = mod_mine.transpose(1, 0, 2).reshape(2, d_mod) + ada_b
    me_op = me.astype(jnp.int32).reshape(1)
    started, token = {}, jnp.zeros((SUBLANE, LANE), F32) + mod_all[0, 0, 0] * 0.0
    for key, group in groups_w.items():
        lands = [_own_slot(f"own_slot_{key}_{n}", s, me_op, token) for n, s in enumerate(group)]
        started[key] = _gather_start(f"gather_{key}_start", lands)
        token = started[key][2]
    mod = mod + token[0, 0]
    mods = [[_row(mod[l, k * d:(k + 1) * d]) for k in range(6)] for l in range(2)]

    def gathered(key, after):
        sems, lands, _ = started[key]
        return _gather_finish(f"gather_{key}_finish", _gather_wait(f"gather_{key}_wait", sems, lands, after))

    g_mix0, g_ffn0 = _row(norm_mix_g[0]), _row(norm_ffn_g[0])
    w_a_in, w_a_out = gathered("sgu", mod)
    w_a_in = w_a_in.transpose(1, 0, 2).reshape(d, 2 * d)
    w_a_out = w_a_out.reshape(d, d)
    na = a_w_in.shape[2]
    h_a = _rms_mod("l0_mix_norm", x0, g_mix0, mods[0][1], mods[0][0], tm)
    pre = _mm("l0_sgu_in", (t_len // tm, 2 * d // tn_wide),
              [(h_a, (tm, d), lambda i, j: (i, 0))], [(w_a_in, (d, tn_wide), lambda i, j: (0, j))], NN,
              [((t_len, 2 * d), F32, (tm, tn_wide), lambda i, j: (i, j))],
              epi=lambda accs, e: [accs[0] + e[0]], extras=[(a_b_in, (1, tn_wide), lambda i, j: (0, j))])[0]
    w_s, b_s_t = a_w_s[0], jnp.pad(a_b_s[0].T, ((0, 0), (0, LANE - groups)))
    ln_g, ln_b = a_ln_g, a_ln_b
    yy = _sgu_fwd("l0_sgu_mix", pre, w_s, b_s_t, ln_g, ln_b)
    y_a, x1 = _out_proj("l0_sgu_out", yy, w_a_out, x0, mods[0][2], tm, tn_wide)
    w_ffn0 = gathered("ffn0", x1)
    x2, ffn0_saved = _ffn_fwd("l0", x1, mods[0], g_ffn0, *w_ffn0, tm)

    g_mix1, g_ffn1 = _row(norm_mix_g[1]), _row(norm_ffn_g[1])
    h_b = _rms_mod("l1_mix_norm", x2, g_mix1, mods[1][1], mods[1][0], tm)
    w_b_in, w_b_out = gathered("fox", h_b)
    w_b_out = w_b_out.reshape(d, d)
    w_b_t = w_b_in.reshape(3 * d + heads, d)
    w_f_t = jnp.pad(w_b_t[3 * d:], ((0, LANE - heads), (0, 0)))
    qkv = _mm("l1_qkv", (t_len // tm, 3 * d // tn_wide),
              [(h_b, (tm, d), lambda i, j: (i, 0))], [(w_b_t, (tn_wide, d), lambda i, j: (j, 0))], NT,
              [((t_len, 3 * d), BF16, (tm, tn_wide), lambda i, j: (i, j))])[0]
    f_logit = _mm("l1_forget_logit", (t_len // tm, 1),
                  [(h_b, (tm, d), lambda i, j: (i, 0))], [(w_f_t, (LANE, d), lambda i, j: (0, 0))], NT,
                  [((t_len, LANE), F32, (tm, LANE), lambda i, j: (i, 0))])[0]
    b_f = jnp.pad(b_b_f, ((0, 0), (0, LANE - heads)))
    f_cum, f_cum_t = _forget_cumsum("l1_forget_cumsum", f_logit, b_f)
    f_keys = f_cum_t[:heads].reshape(heads, 1, t_len)
    o, lse = _attn_fwd("l1_attn", qkv, f_cum, f_keys, heads, tq)
    y_b, x3 = _out_proj("l1_attn_out", o, w_b_out, x2, mods[1][2], tm, tn_wide)
    w_ffn1 = gathered("ffn1", x3)
    x4, ffn1_saved = _ffn_fwd("l1", x3, mods[1], g_ffn1, *w_ffn1, tm)

    dx4, loss_cols, d_final_g = _final_loss("loss_head", x4, target, _row(final_g), tm // 2)
    loss = lax.psum(jnp.sum(loss_cols), ("x", "y", "c"))

    place = jnp.stack([mc, 2 * mx + my]).astype(jnp.int32)
    dx3, ffn1_grads, (dsh2_1, dsc2_1, dg2_1, dgf_1) = _ffn_bwd("l1", dx4, x3, ffn1_saved, mods[1], g_ffn1, *w_ffn1, tm)
    rs_ffn1 = _reduce_scatter_start("ffn1", ffn1_grads, place)
    dy_b, dg1_1 = _gate_bwd("l1_attn_gate_bwd", dx3, y_b, mods[1][2] + rs_ffn1[3][0, 0], tm)
    heads_tn = tn // CHUNK

    def with_delta(accs, e):
        prod = accs[0] * e[0].astype(F32)
        sums = [jnp.sum(prod[:, h * CHUNK:(h + 1) * CHUNK], axis=1, keepdims=True) for h in range(heads_tn)]
        return [accs[0], jnp.stack([jnp.broadcast_to(v, (tm, LANE)) for v in sums])]

    d_o, delta = _mm("l1_attn_out_bwd", (t_len // tm, d // tn),
                     [(dy_b, (tm, d), lambda i, j: (i, 0))], [(w_b_out, (tn, d), lambda i, j: (j, 0))], NT,
                     [((t_len, d), BF16, (tm, tn), lambda i, j: (i, j)),
                      ((heads, t_len, LANE), F32, (heads_tn, tm, LANE), lambda i, j: (j, i, 0))],
                     epi=with_delta, extras=[(o, (tm, tn), lambda i, j: (i, j))])
    d_w_b_out = _weight_grad("l1_attn_wout_grad", o, dy_b, tn, tn)
    dq, dk, dv, dfq, dfk = _attn_bwd("l1_attn_bwd", qkv, d_o, f_cum, f_keys, lse, delta, heads, tq)
    d_cum = dfq - jnp.pad(dfk.reshape(heads, t_len).T, ((0, 0), (0, LANE - heads)))
    d_logit, d_b_f = _forget_bwd("l1_forget_bwd", d_cum, f_logit, b_f)
    d_logit = d_logit.astype(BF16)
    w_grad = ((d, d), BF16, (tn, tn), lambda i, j: (i, j))
    d_w_qkv = _mm("l1_wqkv_grad", (d // tn, d // tn),
                  [(g, (t_len, tn), lambda i, j: (0, i)) for g in (dq, dk, dv)], [(h_b, (t_len, tn), lambda i, j: (0, j))],
                  TN, [w_grad] * 3, summed=False)
    d_w_f = _weight_grad("l1_wf_grad", d_logit, h_b, LANE, tn)
    dh_b = _mm("l1_qkv_bwd", (t_len // tm, d // tn),
               [(g, (tm, d), lambda i, j: (i, 0)) for g in (dq, dk, dv)] + [(d_logit, (tm, LANE), lambda i, j: (i, 0))],
               [(w_b_t, (d, tn), lambda i, j, p=p: (p, j)) for p in range(3)] + [(w_f_t, (LANE, tn), lambda i, j: (0, j))], NN,
               [((t_len, d), F32, (tm, tn), lambda i, j: (i, j))])[0]
    dx2, dsh1_1, dsc1_1, dgm_1 = _rms_mod_bwd("l1_mix_norm_bwd", x2, dh_b, dx3, g_mix1, mods[1][1], tm // 2)
    d_w_b_in = jnp.concatenate([*d_w_qkv, d_w_f[:heads]], axis=0).reshape(N_DEV, b_w_in.shape[2], d)
    rs_fox = _reduce_scatter_start("fox", [d_w_b_in, d_w_b_out.reshape(N_DEV, d // N_DEV, d)], place)

    mods[0][5] = mods[0][5] + rs_fox[3][0, 0]
    dx1, ffn0_grads, (dsh2_0, dsc2_0, dg2_0, dgf_0) = _ffn_bwd("l0", dx2, x1, ffn0_saved, mods[0], g_ffn0, *w_ffn0, tm)
    rs_ffn0 = _reduce_scatter_start("ffn0", ffn0_grads, place)
    dy_a, dg1_0 = _gate_bwd("l0_sgu_gate_bwd", dx1, y_a, mods[0][2] + rs_ffn0[3][0, 0], tm)
    dyy = _proj_bwd("l0_sgu_out_bwd", dy_a, w_a_out, F32, tm, tn_wide)
    d_w_a_out = _weight_grad("l0_sgu_wout_grad", yy, dy_a, tn, tn)
    dpre, d_w_s, d_b_s_t, d_ln_g, d_ln_b, d_b_in = _sgu_bwd("l0_sgu_mix_bwd", pre, dyy, w_s, b_s_t, ln_g, ln_b)
    d_w_a_in = _mm("l0_sgu_win_grad", (N_DEV, d // tn),
                   [(h_a, (t_len, tn), lambda j, i: (0, i))], [(dpre, (t_len, na), lambda j, i: (0, j))], TN,
                   [((N_DEV, d, na), BF16, (None, tn, na), lambda j, i: (j, i, 0))])[0]
    dh_a = _proj_bwd("l0_sgu_in_bwd", dpre, w_a_in, F32, tm, tn_wide)
    dx0, dsh1_0, dsc1_0, dgm_0 = _rms_mod_bwd("l0_mix_norm_bwd", x0, dh_a, dx1, g_mix0, mods[0][1], tm // 2)

    dmod = jnp.concatenate([dsh1_0, dsc1_0, dg1_0, dsh2_0, dsc2_0, dg2_0,
                            dsh1_1, dsc1_1, dg1_1, dsh2_1, dsc2_1, dg2_1], axis=1)
    small_grads = [dmod, jnp.concatenate([dgm_0, dgm_1], axis=1), jnp.concatenate([dgf_0, dgf_1], axis=1),
                   d_b_in, d_ln_g, d_ln_b, d_w_s, d_b_s_t[:, :groups].T, d_b_f[:, :heads], d_final_g]
    small_w = [ada_b, norm_mix_g, norm_ffn_g, a_b_in, a_ln_g, a_ln_b, a_w_s, a_b_s, b_b_f, final_g]
    small_m = [m_ada_b, m_norm_mix_g, m_norm_ffn_g, m_a_b_in, m_a_ln_g, m_a_ln_b, m_a_w_s, m_a_b_s, m_b_b_f, m_final_g]
    small_v = [v_ada_b, v_norm_mix_g, v_norm_ffn_g, v_a_b_in, v_a_ln_g, v_a_ln_b, v_a_w_s, v_a_b_s, v_b_b_f, v_final_g]
    n_small = sum(w.size for w in small_w)
    pack_rows = -(-n_small // (PACK_W * SUBLANE)) * SUBLANE

    def pack(arrs):
        flat = jnp.concatenate([a.reshape(-1) for a in arrs])
        return jnp.pad(flat, (0, pack_rows * PACK_W - n_small)).reshape(pack_rows, PACK_W)

    def unpack(packed):
        flat, out, pos = packed.reshape(-1), [], 0
        for w in small_w:
            out.append(flat[pos:pos + w.size].reshape(w.shape))
            pos += w.size
        return out

    all_small = _all_gather_small("gather_small_grads", pack(small_grads))

    behind_small = d_w_a_out + (all_small[0, 0, 0] * 0.0).astype(BF16)
    rs_sgu = _reduce_scatter_start("sgu", [d_w_a_in, behind_small.reshape(N_DEV, d // N_DEV, d)], place)
    small_out = [unpack(t) for t in _adamw("adamw_small", pack(small_w)[None], all_small[:, None],
                                           pack(small_m)[None], pack(small_v)[None])]

    dmod_all = all_small.reshape(N_DEV, -1)[:, :2 * d_mod].reshape(N_DEV, 2, d_mod)
    dmod_cols = lax.dynamic_slice_in_dim(dmod_all, me * mod_cols, mod_cols, axis=2).transpose(1, 0, 2)
    c_act_t = jnp.pad(c_act[:N_DEV].T, ((0, 0), (0, LANE - N_DEV)))
    g_ada = _ada_grad("ada_w_grad", c_act_t, dmod_cols, _tile(d, 256))
    r_ada = _adamw("adamw_ada_w", ada_w, g_ada[None], m_ada_w, v_ada_w)

    def arrived(tag, started_rs, after):
        sems, parts, lands, _ = started_rs
        return _scatter_wait(f"{tag}_reduce_chips_wait", sems, parts, lands, after)

    g_gate1, g_up1, g_down1 = arrived("ffn1", rs_ffn1, rs_sgu[3])
    g_b_in, g_b_out = arrived("fox", rs_fox, rs_sgu[3])
    g_gate0, g_up0, g_down0 = arrived("ffn0", rs_ffn0, rs_sgu[3])
    r_b_in = t_last(_adamw("adamw_b_w_in", b_in_t, [g_b_in], m_b_in_t, v_b_in_t))
    r_b_out = _adamw("adamw_b_w_out", b_w_out, [g_b_out], m_b_w_out, v_b_w_out)
    r_gate = t_last(_adamw("adamw_ffn_w_gate", gate_t, [g_gate0, g_gate1], m_gate_t, v_gate_t))
    r_up = t_last(_adamw("adamw_ffn_w_up", up_t, [g_up0, g_up1], m_up_t, v_up_t))
    r_down = _adamw("adamw_ffn_w_down", ffn_w_down, [g_down0, g_down1], m_ffn_w_down, v_ffn_w_down)
    first = lambda t: t[(0,) * t.ndim]
    done = [r_b_in[0], r_b_out[0], r_gate[0], r_up[0], r_down[0], r_ada[0], small_out[0][0]]
    g_a_in, g_a_out = arrived("sgu", rs_sgu, jnp.stack([first(t) for t in done]))
    r_a_in = _adamw("adamw_a_w_in", a_w_in, [g_a_in], m_a_w_in, v_a_w_in)
    r_a_out = _adamw("adamw_a_w_out", a_w_out, [g_a_out], m_a_w_out, v_a_w_out)

    def leaf(k):
        s = small_out[k]
        return [r_ada[k], s[0], s[1], s[2], r_a_in[k], s[3], s[4], s[5], s[6], s[7], r_a_out[k],
                r_b_in[k], s[8], r_b_out[k], r_gate[k], r_up[k], r_down[k], s[9]]

    return (loss, dx0[None], *leaf(0), *leaf(1), *leaf(2), *leaf(3))
```

```python
import functools
import math

import jax
import jax.numpy as jnp
from jax import lax
from jax.experimental import pallas as pl
from jax.experimental.pallas import tpu as pltpu

F32 = jnp.float32
BF16 = jnp.bfloat16
MESH = pl.DeviceIdType.MESH
ANY = pl.BlockSpec(memory_space=pl.ANY)
HBM = pl.BlockSpec(memory_space=pltpu.HBM)
SEM = pl.BlockSpec(memory_space=pltpu.SEMAPHORE)
VMEM_SPEC = pl.BlockSpec(memory_space=pltpu.VMEM)
EFFECT = pltpu.SideEffectType.DATAFLOW_SIDE_EFFECTING

N_DEV = 8
N_CHIP = 4
LANE = 128
SUBLANE = 8
CHUNK = 128
VMEM_LIMIT_BYTES = 52 * 1024 * 1024
NORM_EPS = 1e-6
NEG = -1e30
PACK_W = 1024

ADAM_LR = 0.001
ADAM_B1 = 0.9
ADAM_B2 = 0.999
ADAM_EPS = 1e-08
ADAM_WD = 0.01
ADAM_STEP = 10

NN = ((1,), (0,))
NT = ((1,), (1,))
TN = ((0,), (0,))


def _params(*sem):
    return pltpu.CompilerParams(dimension_semantics=sem or None, vmem_limit_bytes=VMEM_LIMIT_BYTES)


def _tile(n, target, align=LANE):
    best = None
    for t in range(align, min(n, target) + 1, align):
        if n % t == 0:
            best = t
    return best or n


def _dot(a, b, dims):
    return lax.dot_general(a, b, (dims, ((), ())), preferred_element_type=F32)


def _place():
    return lax.axis_index("x"), lax.axis_index("y"), lax.axis_index("c")


def _mm(name, grid, lhs, rhs, dims, outs, epi=None, extras=(), summed=True, k_axis=None):
    n_a, n_b, n_e, n_o = len(lhs), len(rhs), len(extras), len(outs)
    n_p = max(n_a, n_b)
    n_acc = 1 if summed else n_p
    nk = grid[k_axis] if k_axis is not None else 1
    acc_shape = tuple(d for d in outs[0][2] if d is not None)

    def body(*refs):
        a_refs, b_refs = refs[:n_a], refs[n_a:n_a + n_b]
        e_refs = refs[n_a + n_b:n_a + n_b + n_e]
        o_refs = refs[n_a + n_b + n_e:n_a + n_b + n_e + n_o]
        acc_refs = refs[n_a + n_b + n_e + n_o:]

        def products():
            a_vals = [r[...].astype(BF16) for r in a_refs]
            b_vals = [r[...].astype(BF16) for r in b_refs]
            ps = [_dot(a_vals[p % n_a], b_vals[p % n_b], dims) for p in range(n_p)]
            return [functools.reduce(lambda u, w: u + w, ps)] if summed else ps

        def finish(accs):
            res = epi(accs, [e[...] for e in e_refs]) if epi is not None else accs
            for o_ref, r in zip(o_refs, res):
                o_ref[...] = r.astype(o_ref.dtype)

        if nk == 1:
            finish(products())
        else:
            k = pl.program_id(k_axis)

            @pl.when(k == 0)
            def _():
                for acc in acc_refs:
                    acc[...] = jnp.zeros_like(acc)

            for acc, p in zip(acc_refs, products()):
                acc[...] += p

            @pl.when(k == nk - 1)
            def _():
                finish([acc[...] for acc in acc_refs])

    sem = ["parallel"] * len(grid)
    if k_axis is not None:
        sem[k_axis] = "arbitrary"
    res = pl.pallas_call(
        body, name=name, grid=grid,
        in_specs=[pl.BlockSpec(bs, im) for _, bs, im in (*lhs, *rhs, *extras)],
        out_specs=[pl.BlockSpec(bs, im) for _, _, bs, im in outs],
        out_shape=[jax.ShapeDtypeStruct(s, d) for s, d, _, _ in outs],
        scratch_shapes=[pltpu.VMEM(acc_shape, F32)] * (n_acc if nk > 1 else 0),
        compiler_params=_params(*sem),
    )(*[a for a, _, _ in (*lhs, *rhs, *extras)])
    return res


def _rowwise(name, fn, tiled, whole, out_tiled, out_sums, tm):
    rows = tiled[0].shape[0]
    n_t, n_w, n_o, n_s = len(tiled), len(whole), len(out_tiled), len(out_sums)

    def body(*refs):
        t_refs, w_refs = refs[:n_t], refs[n_t:n_t + n_w]
        o_refs = refs[n_t + n_w:n_t + n_w + n_o]
        s_refs = refs[n_t + n_w + n_o:]
        outs, sums = fn([r[...] for r in t_refs], [r[...] for r in w_refs])
        for o_ref, val in zip(o_refs, outs):
            o_ref[...] = val.astype(o_ref.dtype)

        @pl.when(pl.program_id(0) == 0)
        def _():
            for s_ref in s_refs:
                s_ref[...] = jnp.zeros_like(s_ref)

        for s_ref, val in zip(s_refs, sums):
            s_ref[...] += val

    full = lambda a: pl.BlockSpec(a.shape, lambda i, nd=a.ndim: (0,) * nd)
    res = pl.pallas_call(
        body, name=name, grid=(rows // tm,),
        in_specs=[pl.BlockSpec((tm, a.shape[1]), lambda i: (i, 0)) for a in tiled] + [full(a) for a in whole],
        out_specs=[pl.BlockSpec((tm, n), lambda i: (i, 0)) for n, _ in out_tiled]
        + [pl.BlockSpec(s, lambda i, nd=len(s): (0,) * nd) for s in out_sums],
        out_shape=[jax.ShapeDtypeStruct((rows, n), d) for n, d in out_tiled]
        + [jax.ShapeDtypeStruct(s, F32) for s in out_sums],
        compiler_params=_params("arbitrary"),
    )(*tiled, *whole)
    return res


def _colsum(v):
    return jnp.sum(v, axis=0, keepdims=True)


def _rms_parts(x):
    inv = lax.rsqrt(jnp.mean(x * x, axis=-1, keepdims=True) + NORM_EPS)
    return inv, x * inv


def _rms_mod(name, x, g, sc, sh, tm):
    def fn(t, w):
        _, xhat = _rms_parts(t[0])
        return [xhat * w[0] * (1.0 + w[1]) + w[2]], []
    return _rowwise(name, fn, [x], [g, sc, sh], [(x.shape[1], BF16)], [], tm)[0]


def _rms_mod_bwd(name, x, dh, dres, g, sc, tm):
    d = x.shape[1]

    def fn(t, w):
        x_, dh_, dres_ = t
        g_, sc_ = w
        inv, xhat = _rms_parts(x_)
        dn = dh_ * (1.0 + sc_)
        dxhat = dn * g_
        dx = dres_ + inv * (dxhat - xhat * jnp.mean(dxhat * xhat, axis=-1, keepdims=True))
        return [dx], [_colsum(dh_), _colsum(dh_ * (xhat * g_)), _colsum(dn * xhat)]
    return _rowwise(name, fn, [x, dh, dres], [g, sc], [(d, F32)], [(1, d)] * 3, tm)


def _gate_bwd(name, dx, y, gate, tm):
    d = dx.shape[1]

    def fn(t, w):
        return [t[0] * w[0]], [_colsum(t[0] * t[1].astype(F32))]
    return _rowwise(name, fn, [dx, y], [gate], [(d, BF16)], [(1, d)], tm)


def _final_loss(name, x, target, g, tm):
    d = x.shape[1]

    def fn(t, w):
        inv, xhat = _rms_parts(t[0])
        err = xhat * w[0] - t[1]
        dout = err * (1.0 / d)
        dxhat = dout * w[0]
        dx = inv * (dxhat - xhat * jnp.mean(dxhat * xhat, axis=-1, keepdims=True))
        return [dx], [_colsum(err * err) * (0.5 / d), _colsum(dout * xhat)]
    return _rowwise(name, fn, [x, target], [g], [(d, F32)], [(1, d)] * 2, tm)


def _silu_rows(name, c):
    def fn(t, w):
        return [t[0] * jax.nn.sigmoid(t[0])], []
    return _rowwise(name, fn, [c], [], [(c.shape[1], F32)], [], c.shape[0])[0]


def _gelu(x):
    return 0.5 * x * (1.0 + lax.erf(x * (1.0 / math.sqrt(2.0))))


def _gelu_grad(x):
    cdf = 0.5 * (1.0 + lax.erf(x * (1.0 / math.sqrt(2.0))))
    return cdf + x * jnp.exp(-0.5 * x * x) * (1.0 / math.sqrt(2.0 * math.pi))


def _layer_norm_parts(v):
    mu = jnp.mean(v, axis=-1, keepdims=True)
    cen = v - mu
    rstd = lax.rsqrt(jnp.mean(cen * cen, axis=-1, keepdims=True) + NORM_EPS)
    return rstd, cen * rstd


def _causal(n):
    return lax.broadcasted_iota(jnp.int32, (n, n), 0) >= lax.broadcasted_iota(jnp.int32, (n, n), 1)


def _sgu_fwd(name, pre, w_s, b_s_t, ln_g, ln_b):
    t_len, d2 = pre.shape
    d = d2 // 2
    groups = w_s.shape[0]

    def body(pre_ref, w_ref, bs_ref, g_ref, b_ref, yy_ref):
        z = _gelu(pre_ref[...])
        u, v = z[:, :d], z[:, d:]
        _, vhat = _layer_norm_parts(v)
        vn = (vhat * g_ref[...] + b_ref[...]).astype(BF16)
        mask = _causal(CHUNK)
        bs = bs_ref[...]
        for g in range(groups):
            cols = slice(g * CHUNK, (g + 1) * CHUNK)
            w = jnp.where(mask, w_ref[g], 0.0).astype(BF16)
            sv = _dot(w, vn[:, cols], NN) + bs[:, g:g + 1]
            yy_ref[:, cols] = (u[:, cols] * sv).astype(BF16)

    full = lambda a: pl.BlockSpec(a.shape, lambda i, nd=a.ndim: (0,) * nd)
    return pl.pallas_call(
        body, name=name, grid=(t_len // CHUNK,),
        in_specs=[pl.BlockSpec((CHUNK, d2), lambda i: (i, 0)), full(w_s), full(b_s_t), full(ln_g), full(ln_b)],
        out_specs=pl.BlockSpec((CHUNK, d), lambda i: (i, 0)),
        out_shape=jax.ShapeDtypeStruct((t_len, d), BF16),
        compiler_params=_params("parallel"),
    )(pre, w_s, b_s_t, ln_g, ln_b)


def _sgu_bwd(name, pre, dyy, w_s, b_s_t, ln_g, ln_b):
    t_len, d2 = pre.shape
    d = d2 // 2
    groups = w_s.shape[0]

    def body(pre_ref, dyy_ref, w_ref, bs_ref, g_ref, b_ref, dpre_ref, dw_ref, dbs_ref, dg_ref, db_ref, dbin_ref, dvn_ref):
        @pl.when(pl.program_id(0) == 0)
        def _():
            for r in (dw_ref, dbs_ref, dg_ref, db_ref, dbin_ref):
                r[...] = jnp.zeros_like(r)

        pre_v = pre_ref[...]
        z = _gelu(pre_v)
        u, v = z[:, :d], z[:, d:]
        rstd, vhat = _layer_norm_parts(v)
        vn = (vhat * g_ref[...] + b_ref[...]).astype(BF16)
        mask = _causal(CHUNK)
        bs = bs_ref[...]
        lane = lax.broadcasted_iota(jnp.int32, (CHUNK, LANE), 1)
        dbs = jnp.zeros((CHUNK, LANE), F32)
        for g in range(groups):
            cols = slice(g * CHUNK, (g + 1) * CHUNK)
            w = jnp.where(mask, w_ref[g], 0.0).astype(BF16)
            sv = _dot(w, vn[:, cols], NN) + bs[:, g:g + 1]
            dyy_g = dyy_ref[:, cols]
            dpre_ref[:, cols] = (dyy_g * sv * _gelu_grad(pre_v[:, cols])).astype(BF16)
            dsv = dyy_g * u[:, cols]
            dbs = jnp.where(lane == g, jnp.sum(dsv, axis=1, keepdims=True), dbs)
            dsv_b = dsv.astype(BF16)
            dw_ref[g] += jnp.where(mask, _dot(dsv_b, vn[:, cols], NT), 0.0)
            dvn_ref[:, cols] = _dot(w, dsv_b, TN)
        dbs_ref[...] += dbs
        dvn = dvn_ref[...]
        dg_ref[...] += _colsum(dvn * vhat)
        db_ref[...] += _colsum(dvn)
        dvhat = dvn * g_ref[...]
        dv = rstd * (dvhat - jnp.mean(dvhat, axis=-1, keepdims=True)
                     - vhat * jnp.mean(dvhat * vhat, axis=-1, keepdims=True))
        dpre_ref[:, d:] = (dv * _gelu_grad(pre_v[:, d:])).astype(BF16)
        dbin_ref[...] += _colsum(dpre_ref[...].astype(F32))

    full = lambda a: pl.BlockSpec(a.shape, lambda i, nd=a.ndim: (0,) * nd)
    acc = lambda s: pl.BlockSpec(s, lambda i, nd=len(s): (0,) * nd)
    sums = [(groups, CHUNK, CHUNK), (CHUNK, LANE), (1, d), (1, d), (1, d2)]
    return pl.pallas_call(
        body, name=name, grid=(t_len // CHUNK,),
        in_specs=[pl.BlockSpec((CHUNK, d2), lambda i: (i, 0)), pl.BlockSpec((CHUNK, d), lambda i: (i, 0)),
                  full(w_s), full(b_s_t), full(ln_g), full(ln_b)],
        out_specs=[pl.BlockSpec((CHUNK, d2), lambda i: (i, 0))] + [acc(s) for s in sums],
        out_shape=[jax.ShapeDtypeStruct((t_len, d2), BF16)] + [jax.ShapeDtypeStruct(s, F32) for s in sums],
        scratch_shapes=[pltpu.VMEM((CHUNK, d), F32)],
        compiler_params=_params("arbitrary"),
    )(pre, dyy, w_s, b_s_t, ln_g, ln_b)


def _whole(rows, cols):
    return pl.BlockSpec((rows, cols), lambda i: (0, 0))


def _cum_matrix(reverse):
    r = lax.broadcasted_iota(jnp.int32, (CHUNK, CHUNK), 0)
    c = lax.broadcasted_iota(jnp.int32, (CHUNK, CHUNK), 1)
    return jnp.where((r <= c) if reverse else (r >= c), 1.0, 0.0).astype(F32)


def _forget_cumsum(name, logits, bias):
    t_len = logits.shape[0]

    def body(fl_ref, b_ref, f_ref, ft_ref):
        tri = _cum_matrix(False)

        def step(n, carry):
            off = pl.multiple_of(n * CHUNK, CHUNK)
            xv = fl_ref[pl.ds(off, CHUNK), :] + b_ref[...]
            log_f = jnp.minimum(xv, 0.0) - jnp.log1p(jnp.exp(-jnp.abs(xv)))
            cs = jnp.dot(tri, log_f, precision=lax.Precision.HIGHEST, preferred_element_type=F32) + carry
            f_ref[pl.ds(off, CHUNK), :] = cs
            ft_ref[:, pl.ds(off, CHUNK)] = cs.T
            return cs[CHUNK - 1:CHUNK, :]

        lax.fori_loop(0, t_len // CHUNK, step, jnp.zeros((1, LANE), F32))

    return pl.pallas_call(
        body, name=name, grid=(1,),
        in_specs=[_whole(t_len, LANE), _whole(1, LANE)],
        out_specs=[_whole(t_len, LANE), _whole(LANE, t_len)],
        out_shape=[jax.ShapeDtypeStruct((t_len, LANE), F32), jax.ShapeDtypeStruct((LANE, t_len), F32)],
        compiler_params=_params("arbitrary"),
    )(logits, bias)


def _forget_bwd(name, d_cum, logits, bias):
    t_len = logits.shape[0]
    n_chunks = t_len // CHUNK

    def body(dc_ref, fl_ref, b_ref, dl_ref, db_ref, run_ref):
        @pl.when(pl.program_id(0) == 0)
        def _():
            run_ref[...] = jnp.zeros_like(run_ref)
            db_ref[...] = jnp.zeros_like(db_ref)

        rc = jnp.dot(_cum_matrix(True), dc_ref[...], precision=lax.Precision.HIGHEST,
                     preferred_element_type=F32) + run_ref[0:1, :]
        dl = rc * jax.nn.sigmoid(-(fl_ref[...] + b_ref[...]))
        dl_ref[...] = dl
        db_ref[...] += _colsum(dl)
        run_ref[...] = jnp.broadcast_to(rc[0:1, :], run_ref.shape)

    back = pl.BlockSpec((CHUNK, LANE), lambda i: (n_chunks - 1 - i, 0))
    return pl.pallas_call(
        body, name=name, grid=(n_chunks,),
        in_specs=[back, back, _whole(1, LANE)],
        out_specs=[back, _whole(1, LANE)],
        out_shape=[jax.ShapeDtypeStruct((t_len, LANE), F32), jax.ShapeDtypeStruct((1, LANE), F32)],
        scratch_shapes=[pltpu.VMEM((SUBLANE, LANE), F32)],
        compiler_params=_params("arbitrary"),
    )(d_cum, logits, bias)


def _head_column(f_tile, head):
    lane = lax.broadcasted_iota(jnp.int32, f_tile.shape, 1)
    return jnp.sum(jnp.where(lane == head, f_tile, 0.0), axis=1, keepdims=True)


def _attn_fwd(name, qkv, f_cum, f_keys, heads, tq):
    t_len = qkv.shape[0]
    scale = 1.0 / math.sqrt(CHUNK)

    def body(q_ref, k_ref, v_ref, f_ref, fk_ref, o_ref, lse_ref):
        head, i = pl.program_id(0), pl.program_id(1)
        q = q_ref[...]
        fq = _head_column(f_ref[...], head)
        causal = _causal(tq)

        def keys(j):
            return pl.ds(pl.multiple_of(j * tq, tq), tq)

        def update(s_raw, j, m, l, diagonal):
            s = s_raw * scale + fq - fk_ref[:, keys(j)]
            if diagonal:
                s = jnp.where(causal, s, NEG)
            m_new = jnp.maximum(m, jnp.max(s, axis=1, keepdims=True))
            p = jnp.exp(s - m_new)
            alpha = jnp.exp(m - m_new)
            return p.astype(BF16), alpha, m_new, alpha * l + jnp.sum(p, axis=1, keepdims=True)

        def step(j, carry):
            m, l, acc, s_raw, p_prev, alpha_prev = carry
            pv = _dot(p_prev, v_ref[keys(jnp.maximum(j - 1, 0)), :], NN)
            s_next = _dot(q, k_ref[keys(j + 1), :], NT)
            p, alpha, m, l = update(s_raw, j, m, l, False)
            return m, l, alpha_prev * acc + pv, s_next, p, alpha

        init = (jnp.full((tq, 1), NEG, F32), jnp.zeros((tq, 1), F32), jnp.zeros((tq, CHUNK), F32),
                _dot(q, k_ref[keys(0), :], NT), jnp.zeros((tq, tq), BF16), jnp.ones((tq, 1), F32))
        m, l, acc, s_raw, p_prev, alpha_prev = lax.fori_loop(0, i, step, init)
        pv = _dot(p_prev, v_ref[keys(jnp.maximum(i - 1, 0)), :], NN)
        p, alpha, m, l = update(s_raw, i, m, l, True)
        acc = alpha * (alpha_prev * acc + pv) + _dot(p, v_ref[keys(i), :], NN)
        o_ref[...] = (acc / l).astype(BF16)
        lse_ref[...] = jnp.broadcast_to(m + jnp.log(l), (tq, LANE))

    return pl.pallas_call(
        body, name=name, grid=(heads, t_len // tq),
        in_specs=[pl.BlockSpec((tq, CHUNK), lambda h, i: (i, h)),
                  pl.BlockSpec((t_len, CHUNK), lambda h, i: (0, heads + h)),
                  pl.BlockSpec((t_len, CHUNK), lambda h, i: (0, 2 * heads + h)),
                  pl.BlockSpec((tq, LANE), lambda h, i: (i, 0)),
                  pl.BlockSpec((None, 1, t_len), lambda h, i: (h, 0, 0))],
        out_specs=[pl.BlockSpec((tq, CHUNK), lambda h, i: (i, h)),
                   pl.BlockSpec((None, tq, LANE), lambda h, i: (h, i, 0))],
        out_shape=[jax.ShapeDtypeStruct((t_len, heads * CHUNK), BF16),
                   jax.ShapeDtypeStruct((heads, t_len, LANE), F32)],
        compiler_params=_params("parallel", "parallel"),
    )(qkv, qkv, qkv, f_cum, f_keys)


def _attn_bwd(name, qkv, d_o, f_cum, f_keys, lse, delta, heads, tq):
    t_len = qkv.shape[0]
    d = heads * CHUNK
    n_q = t_len // tq
    scale = 1.0 / math.sqrt(CHUNK)

    def body(q_ref, k_ref, v_ref, do_ref, f_ref, fk_ref, lse_ref, dl_ref, dq_ref, dk_ref, dv_ref, dfq_ref, dfk_ref,
             dq_acc):
        head, j = pl.program_id(0), pl.program_id(1)

        @pl.when(j == 0)
        def _():
            dq_acc[...] = jnp.zeros_like(dq_acc)

        @pl.when((j == 0) & (head == 0))
        def _():
            dfq_ref[...] = jnp.zeros_like(dfq_ref)

        k, v, fk = k_ref[...], v_ref[...], fk_ref[...]
        lane = lax.broadcasted_iota(jnp.int32, (tq, LANE), 1)
        causal = _causal(tq)

        def rows(i):
            return pl.ds(pl.multiple_of(i * tq, tq), tq)

        def products(i):
            r = rows(i)
            return _dot(q_ref[r, :], k, NT), _dot(do_ref[r, :], v, NT)

        def elementwise(i, s_raw, dp, diagonal):
            r = rows(i)
            s = s_raw * scale + _head_column(f_ref[r, :], head) - fk
            if diagonal:
                s = jnp.where(causal, s, NEG)
            p = jnp.exp(s - lse_ref[r, :][:, 0:1])
            ds = p * (dp - dl_ref[r, :][:, 0:1])
            dfq_ref[r, :] += jnp.where(lane == head, jnp.sum(ds, axis=1, keepdims=True), 0.0)
            return ds.astype(BF16), p.astype(BF16), _colsum(ds)

        def flush(i, ds_b, p_b, dk, dv):
            r = rows(i)
            dq_acc[r, :] += _dot(ds_b, k, NN)
            return dk + _dot(ds_b, q_ref[r, :], TN), dv + _dot(p_b, do_ref[r, :], TN)

        def step(i, carry):
            dk, dv, dfk, ds_prev, p_prev = carry
            s_raw, dp = products(i)
            dk, dv = flush(i - 1, ds_prev, p_prev, dk, dv)
            ds_b, p_b, col = elementwise(i, s_raw, dp, False)
            return dk, dv, dfk + col, ds_b, p_b

        s_raw, dp = products(j)
        ds_b, p_b, dfk = elementwise(j, s_raw, dp, True)
        zero = jnp.zeros((tq, CHUNK), F32)
        dk, dv, dfk, ds_b, p_b = lax.fori_loop(j + 1, n_q, step, (zero, zero, dfk, ds_b, p_b))
        dk, dv = flush(n_q - 1, ds_b, p_b, dk, dv)
        dk_ref[...] = (dk * scale).astype(BF16)
        dv_ref[...] = dv.astype(BF16)
        dfk_ref[...] = dfk

        @pl.when(j == n_q - 1)
        def _():
            dq_ref[...] = (dq_acc[...] * scale).astype(BF16)

    whole_head = lambda c0: pl.BlockSpec((t_len, CHUNK), lambda h, j: (0, c0 + h))
    per_head = pl.BlockSpec((None, t_len, LANE), lambda h, j: (h, 0, 0))
    key_block = lambda c0: pl.BlockSpec((tq, CHUNK), lambda h, j: (j, c0 + h))
    return pl.pallas_call(
        body, name=name, grid=(heads, n_q),
        in_specs=[whole_head(0), key_block(heads), key_block(2 * heads), whole_head(0),
                  pl.BlockSpec((t_len, LANE), lambda h, j: (0, 0)),
                  pl.BlockSpec((None, 1, tq), lambda h, j: (h, 0, j)),
                  per_head, per_head],
        out_specs=[whole_head(0), key_block(0), key_block(0),
                   pl.BlockSpec((t_len, LANE), lambda h, j: (0, 0)),
                   pl.BlockSpec((None, 1, tq), lambda h, j: (h, 0, j))],
        out_shape=[jax.ShapeDtypeStruct((t_len, d), BF16), jax.ShapeDtypeStruct((t_len, d), BF16),
                   jax.ShapeDtypeStruct((t_len, d), BF16), jax.ShapeDtypeStruct((t_len, LANE), F32),
                   jax.ShapeDtypeStruct((heads, 1, t_len), F32)],
        scratch_shapes=[pltpu.VMEM((t_len, CHUNK), F32)],
        compiler_params=_params("arbitrary", "arbitrary"),
    )(qkv, qkv, qkv, d_o, f_cum, f_keys, lse, delta)


def _adamw_ada(name, w, m, v, c_act_t, dmod, tm):
    n_layer, n_b, n_col = dmod.shape
    d = c_act_t.shape[0]

    def body(c_ref, dm_ref, w_ref, m_ref, v_ref, g_ref, d_ref, mo_ref, vo_ref):
        c, dm = c_ref[...], dm_ref[...]
        g = c[:, 0:1] * dm[0:1, :]
        for b in range(1, N_DEV):
            g = g + c[:, b:b + 1] * dm[b:b + 1, :]
        m_new = ADAM_B1 * m_ref[...] + (1.0 - ADAM_B1) * g
        v_new = ADAM_B2 * v_ref[...] + (1.0 - ADAM_B2) * jnp.square(g)
        m_hat = m_new / (1.0 - ADAM_B1 ** ADAM_STEP)
        v_hat = v_new / (1.0 - ADAM_B2 ** ADAM_STEP)
        g_ref[...] = g
        d_ref[...] = -ADAM_LR * (m_hat / (jnp.sqrt(v_hat) + ADAM_EPS) + ADAM_WD * w_ref[...])
        mo_ref[...] = m_new
        vo_ref[...] = v_new

    blk = pl.BlockSpec((None, tm, n_col), lambda l, i: (l, i, 0))
    return pl.pallas_call(
        body, name=name, grid=(n_layer, d // tm),
        in_specs=[pl.BlockSpec((tm, LANE), lambda l, i: (i, 0)),
                  pl.BlockSpec((None, n_b, n_col), lambda l, i: (l, 0, 0)), blk, blk, blk],
        out_specs=[blk] * 4,
        out_shape=[jax.ShapeDtypeStruct(w.shape, F32)] * 4,
        compiler_params=_params("parallel", "parallel"),
    )(c_act_t, dmod, w, m, v)


def _adamw(name, w, parts, m, v):
    n_layer, rows, cols = w.shape
    per_layer = isinstance(parts, (list, tuple))
    parts = list(parts) if per_layer else [parts]
    by_rows = rows % (2 * SUBLANE) == 0
    tr = _tile(rows, max(2 * SUBLANE, (1 << 19) // cols), 2 * SUBLANE) if by_rows else rows
    tc = cols if by_rows else _tile(cols, max(LANE, (1 << 19) // rows))
    n_tiles = rows // tr if by_rows else cols // tc
    at = (lambda i: (i, 0)) if by_rows else (lambda i: (0, i))

    def body(*refs):
        w_ref, m_ref, v_ref = refs[:3]
        p_refs = refs[3:3 + len(parts)]
        g_ref, d_ref, mo_ref, vo_ref = refs[3 + len(parts):]
        layer = pl.program_id(0)
        g = None
        for n, p_ref in enumerate(p_refs):
            g_n = p_ref[0].astype(F32)
            for p in range(1, p_ref.shape[0]):
                g_n = g_n + p_ref[p].astype(F32)
            g = g_n if g is None else jnp.where(layer == n, g_n, g)
        m_new = ADAM_B1 * m_ref[...] + (1.0 - ADAM_B1) * g
        v_new = ADAM_B2 * v_ref[...] + (1.0 - ADAM_B2) * jnp.square(g)
        m_hat = m_new / (1.0 - ADAM_B1 ** ADAM_STEP)
        v_hat = v_new / (1.0 - ADAM_B2 ** ADAM_STEP)
        g_ref[...] = g
        d_ref[...] = -ADAM_LR * (m_hat / (jnp.sqrt(v_hat) + ADAM_EPS) + ADAM_WD * w_ref[...])
        mo_ref[...] = m_new
        vo_ref[...] = v_new

    blk = pl.BlockSpec((None, tr, tc), lambda l, i: (l, *at(i)))
    if per_layer:
        p_specs = [pl.BlockSpec((p.shape[0], tr, tc), lambda l, i, n=n: (0, *at(jnp.where(l == n, i, 0))))
                   for n, p in enumerate(parts)]
    else:
        p_specs = [pl.BlockSpec((parts[0].shape[0], None, tr, tc), lambda l, i: (0, l, *at(i)))]
    return pl.pallas_call(
        body, name=name, grid=(n_layer, n_tiles),
        in_specs=[blk] * 3 + p_specs,
        out_specs=[blk] * 4,
        out_shape=[jax.ShapeDtypeStruct(w.shape, F32)] * 4,
        compiler_params=_params("parallel", "parallel"),
    )(w, m, v, *parts)


def _dev_index(p):
    return 4 * p[0] + 2 * p[1] + p[2]


def _other_chips(mx, my):
    return [(1 - mx, my), (mx, 1 - my), (1 - mx, 1 - my)]


def _all_gather_small(name, x):
    rows, cols = x.shape

    def body(x_ref, o_ref, send_sems, recv_sems):
        mx, my, mc = _place()
        me = _dev_index((mx, my, mc))
        o_ref[me] = x_ref[...]

        def copy(dist, slot, peer):
            return pltpu.make_async_remote_copy(
                src_ref=x_ref, dst_ref=o_ref.at[slot], send_sem=send_sems.at[dist - 1], recv_sem=recv_sems.at[dist - 1],
                device_id=(peer // 4, (peer // 2) % 2, peer % 2), device_id_type=MESH)

        sends = [copy(dist, me, (me + dist) % N_DEV) for dist in range(1, N_DEV)]
        for cp in sends:
            cp.start()
        for dist in range(1, N_DEV):
            src = (me + N_DEV - dist) % N_DEV
            copy(dist, src, src).wait_recv()
        for cp in sends:
            cp.wait_send()

    return pl.pallas_call(
        body, name=name,
        out_shape=jax.ShapeDtypeStruct((N_DEV, rows, cols), x.dtype),
        in_specs=[pl.BlockSpec(memory_space=pltpu.VMEM)],
        out_specs=pl.BlockSpec(memory_space=pltpu.VMEM),
        scratch_shapes=[pltpu.SemaphoreType.DMA((N_DEV - 1,)), pltpu.SemaphoreType.DMA((N_DEV - 1,))],
        compiler_params=_params(),
    )(x)


def _split_copy_call(name, body, n_in, sems, through, extra_out=(), first=True):
    sem_shapes = [pltpu.SemaphoreType.DMA((k,)) for k in sems]
    if first:
        return pl.pallas_call(
            body, name=name,
            out_shape=(*sem_shapes, *[pltpu.HBM(t.shape, t.dtype) for t in through], *extra_out),
            in_specs=[HBM] * n_in,
            out_specs=(*[SEM] * len(sems), *[HBM] * len(through), *[VMEM_SPEC] * len(extra_out)),
            input_output_aliases={i: len(sems) + i for i in range(len(through))},
            compiler_params=pltpu.CompilerParams(has_side_effects=EFFECT),
        )
    return pl.pallas_call(
        body, name=name,
        out_shape=tuple(pltpu.HBM(t.shape, t.dtype) for t in through),
        in_specs=[HBM] * len(through) + [SEM] * len(sems) + [ANY],
        out_specs=tuple([HBM] * len(through)),
        input_output_aliases={i: i for i in range(len(through))},
        compiler_params=pltpu.CompilerParams(has_side_effects=EFFECT),
    )


def _own_slot(name, shard, me, token):
    rows, cols = shard.shape
    tr = _tile(rows, max(2 * SUBLANE, (1 << 19) // cols), 2 * SUBLANE)

    def body(me_ref, x_ref, t_ref, o_ref):
        o_ref[...] = (x_ref[...] + t_ref[0:1, 0:1]).astype(BF16)

    return pl.pallas_call(
        body, name=name,
        grid_spec=pltpu.PrefetchScalarGridSpec(
            num_scalar_prefetch=1, grid=(rows // tr,),
            in_specs=[pl.BlockSpec((tr, cols), lambda i, me: (i, 0)),
                      pl.BlockSpec((SUBLANE, LANE), lambda i, me: (0, 0))],
            out_specs=pl.BlockSpec((None, tr, cols), lambda i, me: (me[0], i, 0))),
        out_shape=jax.ShapeDtypeStruct((N_DEV, rows, cols), BF16),
        compiler_params=_params("parallel"),
    )(me, shard, token)


def _gather_start(name, lands):
    n = len(lands)

    def body(*refs):
        land_refs = refs[:n]
        send_sems, recv_sems = refs[n], refs[n + 1]
        token = refs[-1]
        mx, my, mc = _place()
        for a in range(n):
            own = land_refs[a].at[_dev_index((mx, my, mc))]
            for j, chip in enumerate(_other_chips(mx, my)):
                pltpu.make_async_remote_copy(
                    src_ref=own, dst_ref=own, send_sem=send_sems.at[3 * a + j], recv_sem=recv_sems.at[3 * a + j],
                    device_id=(*chip, mc), device_id_type=MESH).start()
        token[...] = jnp.zeros_like(token)

    operands = [pltpu.with_memory_space_constraint(t, pltpu.HBM) for t in lands]
    res = _split_copy_call(name, body, n, (3 * n, 3 * n), operands,
                           extra_out=(jax.ShapeDtypeStruct((SUBLANE, LANE), F32),))(*operands)
    return res[:2], res[2:2 + n], res[-1]


def _gather_wait(name, sems, lands, after):
    n = len(lands)

    def body(*refs):
        land_refs = refs[:n]
        send_sems, recv_sems = refs[n], refs[n + 1]
        mx, my, mc = _place()
        for a in range(n):
            for j, chip in enumerate(_other_chips(mx, my)):
                copy = pltpu.make_async_remote_copy(
                    src_ref=land_refs[a].at[_dev_index((mx, my, mc))], dst_ref=land_refs[a].at[_dev_index((*chip, mc))],
                    send_sem=send_sems.at[3 * a + j], recv_sem=recv_sems.at[3 * a + j],
                    device_id=(*chip, mc), device_id_type=MESH)
                copy.wait_send()
                copy.wait_recv()

    return _split_copy_call(name, body, n, (3 * n, 3 * n), lands, first=False)(*lands, *sems, after)


def _gather_finish(name, lands):
    n = len(lands)

    def body(*refs):
        o_refs = refs[n:2 * n]
        send_sems, recv_sems = refs[2 * n:]
        mx, my, mc = _place()
        blocks = [(mx, my)] + _other_chips(mx, my)

        def copy(a, k, core):
            slot = o_refs[a].at[_dev_index((*blocks[k], core))]
            return pltpu.make_async_remote_copy(
                src_ref=slot, dst_ref=slot, send_sem=send_sems.at[N_CHIP * a + k], recv_sem=recv_sems.at[N_CHIP * a + k],
                device_id=(mx, my, 1 - mc), device_id_type=MESH)

        sends = [copy(a, k, mc) for a in range(n) for k in range(N_CHIP)]
        for cp in sends:
            cp.start()
        for a in range(n):
            for k in range(N_CHIP):
                copy(a, k, 1 - mc).wait_recv()
        for cp in sends:
            cp.wait_send()

    return pl.pallas_call(
        body, name=name,
        out_shape=[jax.ShapeDtypeStruct(t.shape, t.dtype) for t in lands],
        in_specs=[ANY] * n, out_specs=[ANY] * n,
        input_output_aliases={a: a for a in range(n)},
        scratch_shapes=[pltpu.SemaphoreType.DMA((N_CHIP * n,)), pltpu.SemaphoreType.DMA((N_CHIP * n,))],
        compiler_params=_params(),
    )(*lands)


def _exchange_sibling(name, grads):
    n = len(grads)

    def body(*refs):
        g_refs, r_refs = refs[:n], refs[n:2 * n]
        send_sems, recv_sems = refs[2 * n:]
        mx, my, mc = _place()

        def copy(a, q):
            return pltpu.make_async_remote_copy(
                src_ref=g_refs[a].at[q, 1 - mc], dst_ref=r_refs[a].at[q],
                send_sem=send_sems.at[N_CHIP * a + q], recv_sem=recv_sems.at[N_CHIP * a + q],
                device_id=(mx, my, 1 - mc), device_id_type=MESH)

        copies = [copy(a, q) for a in range(n) for q in range(N_CHIP)]
        for cp in copies:
            cp.start()
        for cp in copies:
            cp.wait_recv()
        for cp in copies:
            cp.wait_send()

    return pl.pallas_call(
        body, name=name,
        out_shape=[jax.ShapeDtypeStruct((N_CHIP, *g.shape[2:]), g.dtype) for g in grads],
        in_specs=[ANY] * n, out_specs=[ANY] * n,
        scratch_shapes=[pltpu.SemaphoreType.DMA((N_CHIP * n,)), pltpu.SemaphoreType.DMA((N_CHIP * n,))],
        compiler_params=_params(),
    )(*grads)


def _add_sibling(name, grad, recv, place):
    _, _, rows, cols = grad.shape
    tr = _tile(rows, max(2 * SUBLANE, (1 << 19) // cols), 2 * SUBLANE)

    def body(p_ref, g_ref, r_ref, o_ref, land_ref):
        total = (g_ref[...].astype(F32) + r_ref[...].astype(F32)).astype(o_ref.dtype)
        o_ref[...] = total

        @pl.when(pl.program_id(1) == p_ref[1])
        def _():
            land_ref[...] = total

    out = jax.ShapeDtypeStruct(recv.shape, recv.dtype)
    return pl.pallas_call(
        body, name=name,
        grid_spec=pltpu.PrefetchScalarGridSpec(
            num_scalar_prefetch=1, grid=(rows // tr, N_CHIP),
            in_specs=[pl.BlockSpec((None, None, tr, cols), lambda i, q, p: (q, p[0], i, 0)),
                      pl.BlockSpec((None, tr, cols), lambda i, q, p: (q, i, 0))],
            out_specs=[pl.BlockSpec((None, tr, cols), lambda i, q, p: (q, i, 0)),
                       pl.BlockSpec((None, tr, cols), lambda i, q, p: (p[1], i, 0))]),
        out_shape=[out, out],
        compiler_params=_params("parallel", "arbitrary"),
    )(place, grad, recv)


def _scatter_start(name, parts, lands):
    n = len(parts)

    def body(*refs):
        p_refs, land_refs = refs[:n], refs[n:2 * n]
        send_sems, recv_sems = refs[2 * n], refs[2 * n + 1]
        token = refs[-1]
        mx, my, mc = _place()
        for a in range(n):
            for j, chip in enumerate(_other_chips(mx, my)):
                pltpu.make_async_remote_copy(
                    src_ref=p_refs[a].at[2 * chip[0] + chip[1]], dst_ref=land_refs[a].at[2 * mx + my],
                    send_sem=send_sems.at[3 * a + j], recv_sem=recv_sems.at[3 * a + j],
                    device_id=(*chip, mc), device_id_type=MESH).start()
        token[...] = jnp.zeros_like(token)

    operands = [pltpu.with_memory_space_constraint(t, pltpu.HBM) for t in (*parts, *lands)]
    res = _split_copy_call(name, body, 2 * n, (3 * n, 3 * n), operands,
                           extra_out=(jax.ShapeDtypeStruct((SUBLANE, LANE), F32),))(*operands)
    return res[:2], res[2:2 + n], res[2 + n:2 + 2 * n], res[-1]


def _scatter_wait(name, sems, parts, lands, after):
    n = len(parts)

    def body(*refs):
        p_refs, land_refs = refs[:n], refs[n:2 * n]
        send_sems, recv_sems = refs[2 * n], refs[2 * n + 1]
        mx, my, mc = _place()
        for a in range(n):
            for j, chip in enumerate(_other_chips(mx, my)):
                copy = pltpu.make_async_remote_copy(
                    src_ref=p_refs[a].at[2 * chip[0] + chip[1]], dst_ref=land_refs[a].at[2 * chip[0] + chip[1]],
                    send_sem=send_sems.at[3 * a + j], recv_sem=recv_sems.at[3 * a + j],
                    device_id=(*chip, mc), device_id_type=MESH)
                copy.wait_send()
                copy.wait_recv()

    res = _split_copy_call(name, body, 2 * n, (3 * n, 3 * n), [*parts, *lands], first=False)(
        *parts, *lands, *sems, after)
    return res[n:]


def _reduce_scatter_start(tag, grads, place):
    wide = [g.reshape(N_CHIP, 2, *g.shape[1:]) for g in grads]
    from_sibling = _exchange_sibling(f"{tag}_reduce_sibling", wide)
    added = [_add_sibling(f"{tag}_add_sibling_{n}", g, r, place) for n, (g, r) in enumerate(zip(wide, from_sibling))]
    return _scatter_start(f"{tag}_reduce_chips_start", [p for p, _ in added], [l for _, l in added])


def _row(v):
    return v.reshape(1, -1)


def _out_proj(name, act, w_out, x, gate, tm, tn):
    t_len, d = x.shape
    k = act.shape[1]
    return _mm(name, (t_len // tm, d // tn),
               [(act, (tm, k), lambda i, j: (i, 0))], [(w_out, (k, tn), lambda i, j: (0, j))], NN,
               [((t_len, d), BF16, (tm, tn), lambda i, j: (i, j)), ((t_len, d), F32, (tm, tn), lambda i, j: (i, j))],
               epi=lambda accs, e: [accs[0], e[0] + e[1] * accs[0]],
               extras=[(x, (tm, tn), lambda i, j: (i, j)), (gate, (1, tn), lambda i, j: (0, j))])


def _proj_bwd(name, dy, w_out, dtype, tm, tn):
    t_len, d = dy.shape
    k = w_out.shape[0]
    return _mm(name, (t_len // tm, k // tn),
               [(dy, (tm, d), lambda i, j: (i, 0))], [(w_out, (tn, d), lambda i, j: (j, 0))], NT,
               [((t_len, k), dtype, (tm, tn), lambda i, j: (i, j))])[0]


def _weight_grad(name, act, dy, tm, tn):
    t_len, k = act.shape
    n = dy.shape[1]
    return _mm(name, (k // tm, n // tn),
               [(act, (t_len, tm), lambda i, j: (0, i))], [(dy, (t_len, tn), lambda i, j: (0, j))], TN,
               [((k, n), BF16, (tm, tn), lambda i, j: (i, j))])[0]


def _ffn_fwd(tag, x1, mod, g_norm, w_gate, w_up, w_down, tm):
    t_len, d = x1.shape
    fs = w_gate.shape[1]
    sh2, sc2, g2 = mod[3], mod[4], mod[5]
    h2 = _rms_mod(f"{tag}_ffn_norm", x1, g_norm, sc2, sh2, tm)
    hidden = ((N_DEV, t_len, fs), BF16, (None, tm, fs), lambda j, i: (j, i, 0))

    def swiglu(accs, _):
        a, b = accs
        return [a, b, a * jax.nn.sigmoid(a) * b]

    a, b, s = _mm(f"{tag}_ffn_up", (N_DEV, t_len // tm),
                  [(h2, (tm, d), lambda j, i: (i, 0))],
                  [(w_gate, (None, fs, d), lambda j, i: (j, 0, 0)), (w_up, (None, fs, d), lambda j, i: (j, 0, 0))],
                  NT, [hidden] * 3, epi=swiglu, summed=False)
    f, x2 = _mm(f"{tag}_ffn_down", (t_len // tm, 1, N_DEV // 2),
                [(s, (None, tm, fs), lambda i, j, k, p=p: (2 * k + p, i, 0)) for p in range(2)],
                [(w_down, (None, fs, d), lambda i, j, k, p=p: (2 * k + p, 0, 0)) for p in range(2)],
                NN,
                [((t_len, d), BF16, (tm, d), lambda i, j, k: (i, 0)), ((t_len, d), F32, (tm, d), lambda i, j, k: (i, 0))],
                epi=lambda accs, e: [accs[0], e[0] + e[1] * accs[0]],
                extras=[(x1, (tm, d), lambda i, j, k: (i, 0)), (g2, (1, d), lambda i, j, k: (0, 0))], k_axis=2)
    return x2, (h2, a, b, s, f)


def _ffn_bwd(tag, dx2, x1, saved, mod, g_norm, w_gate, w_up, w_down, tm):
    t_len, d = x1.shape
    fs = w_gate.shape[1]
    h2, a, b, s, f = saved
    sc2, g2 = mod[4], mod[5]
    df, dg2 = _gate_bwd(f"{tag}_ffn_gate_bwd", dx2, f, g2, tm)
    hidden = ((N_DEV, t_len, fs), BF16, (None, tm, fs), lambda j, i: (j, i, 0))
    hid_in = lambda arr: (arr, (None, tm, fs), lambda j, i: (j, i, 0))

    def swiglu_bwd(accs, e):
        a_, b_ = e[0].astype(F32), e[1].astype(F32)
        sig = jax.nn.sigmoid(a_)
        return [accs[0] * b_ * sig * (1.0 + a_ * (1.0 - sig)), accs[0] * a_ * sig]

    da, db = _mm(f"{tag}_ffn_down_bwd", (N_DEV, t_len // tm),
                 [(df, (tm, d), lambda j, i: (i, 0))], [(w_down, (None, fs, d), lambda j, i: (j, 0, 0))], NT,
                 [hidden] * 2, epi=swiglu_bwd, extras=[hid_in(a), hid_in(b)])
    tn = _tile(d, 512)
    d_wd = _mm(f"{tag}_ffn_wdown_grad", (N_DEV, d // tn),
               [(s, (None, t_len, fs), lambda j, i: (j, 0, 0))], [(df, (t_len, tn), lambda j, i: (0, i))], TN,
               [((N_DEV, fs, d), BF16, (None, fs, tn), lambda j, i: (j, 0, i))])[0]
    w_grad = ((N_DEV, fs, d), BF16, (None, fs, tn), lambda j, i: (j, 0, i))
    d_wg, d_wu = _mm(f"{tag}_ffn_wup_grad", (N_DEV, d // tn),
                     [(da, (None, t_len, fs), lambda j, i: (j, 0, 0)), (db, (None, t_len, fs), lambda j, i: (j, 0, 0))],
                     [(h2, (t_len, tn), lambda j, i: (0, i))],
                     TN, [w_grad] * 2, summed=False)
    dh2 = _mm(f"{tag}_ffn_up_bwd", (t_len // tm, 1, N_DEV // 2),
              [(g, (None, tm, fs), lambda i, j, k, p=p: (2 * k + p, i, 0)) for p in range(2) for g in (da, db)],
              [(w, (None, fs, d), lambda i, j, k, p=p: (2 * k + p, 0, 0)) for p in range(2) for w in (w_gate, w_up)],
              NN, [((t_len, d), F32, (tm, d), lambda i, j, k: (i, 0))], k_axis=2)[0]
    dx1, dsh2, dsc2, dgn = _rms_mod_bwd(f"{tag}_ffn_norm_bwd", x1, dh2, dx2, g_norm, sc2, tm // 2)
    return dx1, (d_wg, d_wu, d_wd), (dsh2, dsc2, dg2, dgn)


def kernel(x, c, ada_w, ada_b, norm_mix_g, norm_ffn_g, a_w_in, a_b_in, a_ln_g, a_ln_b, a_w_s, a_b_s, a_w_out, b_w_in, b_b_f, b_w_out, ffn_w_gate, ffn_w_up, ffn_w_down, final_g, loss_target, m_ada_w, m_ada_b, m_norm_mix_g, m_norm_ffn_g, m_a_w_in, m_a_b_in, m_a_ln_g, m_a_ln_b, m_a_w_s, m_a_b_s, m_a_w_out, m_b_w_in, m_b_b_f, m_b_w_out, m_ffn_w_gate, m_ffn_w_up, m_ffn_w_down, m_final_g, v_ada_w, v_ada_b, v_norm_mix_g, v_norm_ffn_g, v_a_w_in, v_a_b_in, v_a_ln_g, v_a_ln_b, v_a_w_s, v_a_b_s, v_a_w_out, v_b_w_in, v_b_b_f, v_b_w_out, v_ffn_w_gate, v_ffn_w_up, v_ffn_w_down, v_final_g):
    t_len, d = x.shape[1], x.shape[2]
    heads = d // CHUNK
    groups = a_w_s.shape[1]
    d_mod = 6 * d
    mod_cols = ada_w.shape[2]
    tm = _tile(t_len, 512)
    tn = _tile(d, 512)
    tn_wide = _tile(d, 1024)
    tq = _tile(t_len, 256)
    mx, my, mc = _place()
    me = _dev_index((mx, my, mc))
    x0, target = x[0], loss_target[0]

    t_last = lambda arrs: [a.transpose(0, 2, 1) for a in arrs]
    gate_t, m_gate_t, v_gate_t = t_last([ffn_w_gate, m_ffn_w_gate, v_ffn_w_gate])
    up_t, m_up_t, v_up_t = t_last([ffn_w_up, m_ffn_w_up, v_ffn_w_up])
    b_in_t, m_b_in_t, v_b_in_t = t_last([b_w_in, m_b_w_in, v_b_w_in])
    groups_w = {"sgu": [a_w_in[0], a_w_out[0]], "ffn0": [gate_t[0], up_t[0], ffn_w_down[0]],
                "fox": [b_in_t[0], b_w_out[0]], "ffn1": [gate_t[1], up_t[1], ffn_w_down[1]]}
    c_all = _all_gather_small("gather_c", jnp.pad(c, ((0, SUBLANE - 1), (0, 0))))[:, 0, :]
    c_act = _silu_rows("silu_c", jnp.pad(c_all, ((0, 2 * SUBLANE - N_DEV), (0, 0))))
    mod_part = _mm("mod_matmul", (2, mod_cols // _tile(mod_cols, 512)),
                   [(c_act, (2 * SUBLANE, d), lambda l, j: (0, 0))],
                   [(ada_w, (None, d, _tile(mod_cols, 512)), lambda l, j: (l, 0, j))], NN,
                   [((2, 2 * SUBLANE, mod_cols), F32, (None, 2 * SUBLANE, _tile(mod_cols, 512)), lambda l, j: (l, 0, j))])[0]
    mod_all = _all_gather_small("gather_mod", mod_part[:, :N_DEV, :].reshape(2 * N_DEV, mod_cols))
    mod_mine = lax.dynamic_index_in_dim(mod_all.reshape(N_DEV, 2, N_DEV, mod_cols), me, axis=2, keepdims=False)
    mod = mod_mine.transpose(1, 0, 2).reshape(2, d_mod) + ada_b
    me_op = me.astype(jnp.int32).reshape(1)
    started, token = {}, jnp.zeros((SUBLANE, LANE), F32) + mod_all[0, 0, 0] * 0.0
    for key, group in groups_w.items():
        lands = [_own_slot(f"own_slot_{key}_{n}", s, me_op, token) for n, s in enumerate(group)]
        started[key] = _gather_start(f"gather_{key}_start", lands)
        token = started[key][2]
    mod = mod + token[0, 0]
    mods = [[_row(mod[l, k * d:(k + 1) * d]) for k in range(6)] for l in range(2)]

    def gathered(key, after):
        sems, lands, _ = started[key]
        return _gather_finish(f"gather_{key}_finish", _gather_wait(f"gather_{key}_wait", sems, lands, after))

    g_mix0, g_ffn0 = _row(norm_mix_g[0]), _row(norm_ffn_g[0])
    w_a_in, w_a_out = gathered("sgu", mod)
    w_a_in = w_a_in.transpose(1, 0, 2).reshape(d, 2 * d)
    w_a_out = w_a_out.reshape(d, d)
    na = a_w_in.shape[2]
    h_a = _rms_mod("l0_mix_norm", x0, g_mix0, mods[0][1], mods[0][0], tm)
    pre = _mm("l0_sgu_in", (t_len // tm, 2 * d // tn_wide),
              [(h_a, (tm, d), lambda i, j: (i, 0))], [(w_a_in, (d, tn_wide), lambda i, j: (0, j))], NN,
              [((t_len, 2 * d), F32, (tm, tn_wide), lambda i, j: (i, j))],
              epi=lambda accs, e: [accs[0] + e[0]], extras=[(a_b_in, (1, tn_wide), lambda i, j: (0, j))])[0]
    w_s, b_s_t = a_w_s[0], jnp.pad(a_b_s[0].T, ((0, 0), (0, LANE - groups)))
    ln_g, ln_b = a_ln_g, a_ln_b
    yy = _sgu_fwd("l0_sgu_mix", pre, w_s, b_s_t, ln_g, ln_b)
    y_a, x1 = _out_proj("l0_sgu_out", yy, w_a_out, x0, mods[0][2], tm, tn_wide)
    w_ffn0 = gathered("ffn0", x1)
    x2, ffn0_saved = _ffn_fwd("l0", x1, mods[0], g_ffn0, *w_ffn0, tm)

    g_mix1, g_ffn1 = _row(norm_mix_g[1]), _row(norm_ffn_g[1])
    h_b = _rms_mod("l1_mix_norm", x2, g_mix1, mods[1][1], mods[1][0], tm)
    w_b_in, w_b_out = gathered("fox", h_b)
    w_b_out = w_b_out.reshape(d, d)
    w_b_t = w_b_in.reshape(3 * d + heads, d)
    w_f_t = jnp.pad(w_b_t[3 * d:], ((0, LANE - heads), (0, 0)))
    qkv = _mm("l1_qkv", (t_len // tm, 3 * d // tn_wide),
              [(h_b, (tm, d), lambda i, j: (i, 0))], [(w_b_t, (tn_wide, d), lambda i, j: (j, 0))], NT,
              [((t_len, 3 * d), BF16, (tm, tn_wide), lambda i, j: (i, j))])[0]
    f_logit = _mm("l1_forget_logit", (t_len // tm, 1),
                  [(h_b, (tm, d), lambda i, j: (i, 0))], [(w_f_t, (LANE, d), lambda i, j: (0, 0))], NT,
                  [((t_len, LANE), F32, (tm, LANE), lambda i, j: (i, 0))])[0]
    b_f = jnp.pad(b_b_f, ((0, 0), (0, LANE - heads)))
    f_cum, f_cum_t = _forget_cumsum("l1_forget_cumsum", f_logit, b_f)
    f_keys = f_cum_t[:heads].reshape(heads, 1, t_len)
    o, lse = _attn_fwd("l1_attn", qkv, f_cum, f_keys, heads, tq)
    y_b, x3 = _out_proj("l1_attn_out", o, w_b_out, x2, mods[1][2], tm, tn_wide)
    w_ffn1 = gathered("ffn1", x3)
    x4, ffn1_saved = _ffn_fwd("l1", x3, mods[1], g_ffn1, *w_ffn1, tm)

    dx4, loss_cols, d_final_g = _final_loss("loss_head", x4, target, _row(final_g), tm // 2)
    loss = lax.psum(jnp.sum(loss_cols), ("x", "y", "c"))

    place = jnp.stack([mc, 2 * mx + my]).astype(jnp.int32)
    dx3, ffn1_grads, (dsh2_1, dsc2_1, dg2_1, dgf_1) = _ffn_bwd("l1", dx4, x3, ffn1_saved, mods[1], g_ffn1, *w_ffn1, tm)
    rs_ffn1 = _reduce_scatter_start("ffn1", ffn1_grads, place)
    dy_b, dg1_1 = _gate_bwd("l1_attn_gate_bwd", dx3, y_b, mods[1][2] + rs_ffn1[3][0, 0], tm)
    heads_tn = tn // CHUNK

    def with_delta(accs, e):
        prod = accs[0] * e[0].astype(F32)
        sums = [jnp.sum(prod[:, h * CHUNK:(h + 1) * CHUNK], axis=1, keepdims=True) for h in range(heads_tn)]
        return [accs[0], jnp.stack([jnp.broadcast_to(v, (tm, LANE)) for v in sums])]

    d_o, delta = _mm("l1_attn_out_bwd", (t_len // tm, d // tn),
                     [(dy_b, (tm, d), lambda i, j: (i, 0))], [(w_b_out, (tn, d), lambda i, j: (j, 0))], NT,
                     [((t_len, d), BF16, (tm, tn), lambda i, j: (i, j)),
                      ((heads, t_len, LANE), F32, (heads_tn, tm, LANE), lambda i, j: (j, i, 0))],
                     epi=with_delta, extras=[(o, (tm, tn), lambda i, j: (i, j))])
    d_w_b_out = _weight_grad("l1_attn_wout_grad", o, dy_b, tn, tn)
    dq, dk, dv, dfq, dfk = _attn_bwd("l1_attn_bwd", qkv, d_o, f_cum, f_keys, lse, delta, heads, tq)
    d_cum = dfq - jnp.pad(dfk.reshape(heads, t_len).T, ((0, 0), (0, LANE - heads)))
    d_logit, d_b_f = _forget_bwd("l1_forget_bwd", d_cum, f_logit, b_f)
    d_logit = d_logit.astype(BF16)
    w_grad = ((d, d), BF16, (tn, tn), lambda i, j: (i, j))
    d_w_qkv = _mm("l1_wqkv_grad", (d // tn, d // tn),
                  [(g, (t_len, tn), lambda i, j: (0, i)) for g in (dq, dk, dv)], [(h_b, (t_len, tn), lambda i, j: (0, j))],
                  TN, [w_grad] * 3, summed=False)
    d_w_f = _weight_grad("l1_wf_grad", d_logit, h_b, LANE, tn)
    dh_b = _mm("l1_qkv_bwd", (t_len // tm, d // tn),
               [(g, (tm, d), lambda i, j: (i, 0)) for g in (dq, dk, dv)] + [(d_logit, (tm, LANE), lambda i, j: (i, 0))],
               [(w_b_t, (d, tn), lambda i, j, p=p: (p, j)) for p in range(3)] + [(w_f_t, (LANE, tn), lambda i, j: (0, j))], NN,
               [((t_len, d), F32, (tm, tn), lambda i, j: (i, j))])[0]
    dx2, dsh1_1, dsc1_1, dgm_1 = _rms_mod_bwd("l1_mix_norm_bwd", x2, dh_b, dx3, g_mix1, mods[1][1], tm // 2)
    d_w_b_in = jnp.concatenate([*d_w_qkv, d_w_f[:heads]], axis=0).reshape(N_DEV, b_w_in.shape[2], d)
    rs_fox = _reduce_scatter_start("fox", [d_w_b_in, d_w_b_out.reshape(N_DEV, d // N_DEV, d)], place)

    mods[0][5] = mods[0][5] + rs_fox[3][0, 0]
    dx1, ffn0_grads, (dsh2_0, dsc2_0, dg2_0, dgf_0) = _ffn_bwd("l0", dx2, x1, ffn0_saved, mods[0], g_ffn0, *w_ffn0, tm)
    rs_ffn0 = _reduce_scatter_start("ffn0", ffn0_grads, place)
    dy_a, dg1_0 = _gate_bwd("l0_sgu_gate_bwd", dx1, y_a, mods[0][2] + rs_ffn0[3][0, 0], tm)
    dyy = _proj_bwd("l0_sgu_out_bwd", dy_a, w_a_out, F32, tm, tn_wide)
    d_w_a_out = _weight_grad("l0_sgu_wout_grad", yy, dy_a, tn, tn)
    dpre, d_w_s, d_b_s_t, d_ln_g, d_ln_b, d_b_in = _sgu_bwd("l0_sgu_mix_bwd", pre, dyy, w_s, b_s_t, ln_g, ln_b)
    d_w_a_in = _mm("l0_sgu_win_grad", (N_DEV, d // tn),
                   [(h_a, (t_len, tn), lambda j, i: (0, i))], [(dpre, (t_len, na), lambda j, i: (0, j))], TN,
                   [((N_DEV, d, na), BF16, (None, tn, na), lambda j, i: (j, i, 0))])[0]
    dh_a = _proj_bwd("l0_sgu_in_bwd", dpre, w_a_in, F32, tm, tn_wide)
    dx0, dsh1_0, dsc1_0, dgm_0 = _rms_mod_bwd("l0_mix_norm_bwd", x0, dh_a, dx1, g_mix0, mods[0][1], tm // 2)

    dmod = jnp.concatenate([dsh1_0, dsc1_0, dg1_0, dsh2_0, dsc2_0, dg2_0,
                            dsh1_1, dsc1_1, dg1_1, dsh2_1, dsc2_1, dg2_1], axis=1)
    small_grads = [dmod, jnp.concatenate([dgm_0, dgm_1], axis=1), jnp.concatenate([dgf_0, dgf_1], axis=1),
                   d_b_in, d_ln_g, d_ln_b, d_w_s, d_b_s_t[:, :groups].T, d_b_f[:, :heads], d_final_g]
    small_w = [ada_b, norm_mix_g, norm_ffn_g, a_b_in, a_ln_g, a_ln_b, a_w_s, a_b_s, b_b_f, final_g]
    small_m = [m_ada_b, m_norm_mix_g, m_norm_ffn_g, m_a_b_in, m_a_ln_g, m_a_ln_b, m_a_w_s, m_a_b_s, m_b_b_f, m_final_g]
    small_v = [v_ada_b, v_norm_mix_g, v_norm_ffn_g, v_a_b_in, v_a_ln_g, v_a_ln_b, v_a_w_s, v_a_b_s, v_b_b_f, v_final_g]
    n_small = sum(w.size for w in small_w)
    pack_rows = -(-n_small // (PACK_W * SUBLANE)) * SUBLANE

    def pack(arrs):
        flat = jnp.concatenate([a.reshape(-1) for a in arrs])
        return jnp.pad(flat, (0, pack_rows * PACK_W - n_small)).reshape(pack_rows, PACK_W)

    def unpack(packed):
        flat, out, pos = packed.reshape(-1), [], 0
        for w in small_w:
            out.append(flat[pos:pos + w.size].reshape(w.shape))
            pos += w.size
        return out

    all_small = _all_gather_small("gather_small_grads", pack(small_grads))

    behind_small = d_w_a_out + (all_small[0, 0, 0] * 0.0).astype(BF16)
    rs_sgu = _reduce_scatter_start("sgu", [d_w_a_in, behind_small.reshape(N_DEV, d // N_DEV, d)], place)
    small_out = [unpack(t) for t in _adamw("adamw_small", pack(small_w)[None], all_small[:, None],
                                           pack(small_m)[None], pack(small_v)[None])]

    dmod_all = all_small.reshape(N_DEV, -1)[:, :2 * d_mod].reshape(N_DEV, 2, d_mod)
    dmod_cols = lax.dynamic_slice_in_dim(dmod_all, me * mod_cols, mod_cols, axis=2).transpose(1, 0, 2)
    c_act_t = jnp.pad(c_act[:N_DEV].T, ((0, 0), (0, LANE - N_DEV)))
    r_ada = _adamw_ada("adamw_ada_w", ada_w, m_ada_w, v_ada_w, c_act_t, dmod_cols, _tile(d, 256))

    def arrived(tag, started_rs, after):
        sems, parts, lands, _ = started_rs
        return _scatter_wait(f"{tag}_reduce_chips_wait", sems, parts, lands, after)

    g_gate1, g_up1, g_down1 = arrived("ffn1", rs_ffn1, rs_sgu[3])
    g_b_in, g_b_out = arrived("fox", rs_fox, rs_sgu[3])
    g_gate0, g_up0, g_down0 = arrived("ffn0", rs_ffn0, rs_sgu[3])
    r_b_in = t_last(_adamw("adamw_b_w_in", b_in_t, [g_b_in], m_b_in_t, v_b_in_t))
    r_b_out = _adamw("adamw_b_w_out", b_w_out, [g_b_out], m_b_w_out, v_b_w_out)
    r_gate = t_last(_adamw("adamw_ffn_w_gate", gate_t, [g_gate0, g_gate1], m_gate_t, v_gate_t))
    r_up = t_last(_adamw("adamw_ffn_w_up", up_t, [g_up0, g_up1], m_up_t, v_up_t))
    r_down = _adamw("adamw_ffn_w_down", ffn_w_down, [g_down0, g_down1], m_ffn_w_down, v_ffn_w_down)
    first = lambda t: t[(0,) * t.ndim]
    done = [r_b_in[0], r_b_out[0], r_gate[0], r_up[0], r_down[0], r_ada[0], small_out[0][0]]
    g_a_in, g_a_out = arrived("sgu", rs_sgu, jnp.stack([first(t) for t in done]))
    r_a_in = _adamw("adamw_a_w_in", a_w_in, [g_a_in], m_a_w_in, v_a_w_in)
    r_a_out = _adamw("adamw_a_w_out", a_w_out, [g_a_out], m_a_w_out, v_a_w_out)

    def leaf(k):
        s = small_out[k]
        return [r_ada[k], s[0], s[1], s[2], r_a_in[k], s[3], s[4], s[5], s[6], s[7], r_a_out[k],
                r_b_in[k], s[8], r_b_out[k], r_gate[k], r_up[k], r_down[k], s[9]]

    return (loss, dx0[None], *leaf(0), *leaf(1), *leaf(2), *leaf(3))
```
